```python
import math
import jax
import jax.numpy as jnp
from jax import lax
import numpy as np

D_MODEL = 2048
BATCH = 4
SEQ = 2048
DEPTH = 4
DEC_BATCH = 8
DEC_SEQ = 4
PAST_LEN = 16384
PAGE_SIZE = 128

N_EVEN = (DEPTH + 1) // 2
N_ODD = DEPTH // 2
A_WIDTH = D_MODEL // 2
A_GROUPS = 8
A_GROUP_DIM = A_WIDTH // A_GROUPS
A_CHUNK = 128
NSA_HEADS = 8
NSA_HEAD_DIM = 128
NSA_KV_HEADS = 2
NSA_GROUP = NSA_HEADS // NSA_KV_HEADS
B_WIDTH = NSA_HEADS * NSA_HEAD_DIM
KV_COLS = 2 * NSA_KV_HEADS * NSA_HEAD_DIM
CMP_BLOCK = 64
SEL_BLOCK = CMP_BLOCK
SEL_TOPN = 16
WINDOW = 512
SEL_QBLOCK = 64
WIN_QBLOCK = 128
ROPE_THETA = 10000.0
ATTN_SCALE = NSA_HEAD_DIM ** -0.5
SPLIT_EVEN = (A_WIDTH, 2 * A_WIDTH, 2 * A_WIDTH + B_WIDTH, 2 * A_WIDTH + B_WIDTH + KV_COLS,
              2 * A_WIDTH + B_WIDTH + 2 * KV_COLS, 2 * A_WIDTH + B_WIDTH + 3 * KV_COLS)
E_IN = 2 * A_WIDTH + B_WIDTH + 3 * KV_COLS + 3 * NSA_HEADS
MIX_OUT = A_WIDTH + B_WIDTH
SSM_INNER = 2 * D_MODEL
SSM_HEAD_DIM = 64
SSM_HEADS = SSM_INNER // SSM_HEAD_DIM
SSM_GROUPS = 8
SSM_STATE = 128
SSM_CONV = 4
SSM_CHUNK = 128
SSM_CONV_DIM = SSM_INNER + 2 * SSM_GROUPS * SSM_STATE
SPLIT_ODD = (SSM_INNER, SSM_INNER + SSM_CONV_DIM)
C_IN = SSM_INNER + SSM_CONV_DIM + SSM_HEADS
D_FF = 5632
FFN_CONV = 3

EPS = 1e-6
NEG = -1e30
FORCE = 1e4
TINY = 1e-30

kernel_name = 'hybrid_gmlp_nsa_ssd_convffn_step'


def rmsnorm(x, g):
    xf = x.astype(jnp.float32)
    y = xf * lax.rsqrt(jnp.mean(xf * xf, axis=-1, keepdims=True) + EPS)
    return (y * g.astype(jnp.float32)).astype(x.dtype)


def rope(x, pos):
    half = x.shape[-1] // 2
    inv = ROPE_THETA ** (-jnp.arange(half, dtype=jnp.float32) / half)
    ang = pos.astype(jnp.float32)[:, None] * inv[None, :]
    cos = jnp.cos(ang)[:, None, :]
    sin = jnp.sin(ang)[:, None, :]
    xf = x.astype(jnp.float32)
    x1, x2 = xf[..., :half], xf[..., half:]
    return jnp.concatenate([x1 * cos - x2 * sin, x2 * cos + x1 * sin], axis=-1).astype(x.dtype)


def causal_dwconv(x, buf, w, b):
    width = w.shape[0]
    L = x.shape[1]
    xp = jnp.concatenate([buf.astype(x.dtype), x], axis=1)
    y = xp[:, 0:L] * w[0]
    for k in range(1, width):
        y = y + xp[:, k:k + L] * w[k]
    return y + b, xp[:, xp.shape[1] - (width - 1):]


def gmlp_mix(u, v, ws, bs):
    bsz, L, _ = v.shape
    nc = L // A_CHUNK
    wm = jnp.where(jnp.tril(jnp.ones((A_CHUNK, A_CHUNK), bool)), ws, 0.0)
    vc = v.reshape(bsz, nc, A_CHUNK, A_GROUPS, A_GROUP_DIM)
    s = jnp.einsum('gij,bnjgc->bnigc', wm.astype(v.dtype), vc) + bs.T[:, :, None].astype(v.dtype)
    return u * s.reshape(bsz, L, A_WIDTH)


def kv_rows(t, gain, pos):
    bsz, L = t.shape[:2]
    t = t.reshape(bsz, L, 2, NSA_KV_HEADS, NSA_HEAD_DIM)
    k = rmsnorm(t[:, :, 0], gain)
    if pos is not None:
        k = rope(k, pos)
    return jnp.stack([k, t[:, :, 1]], axis=2)


def compress(full, pool):
    bsz, T = full.shape[:2]
    nb = T // CMP_BLOCK
    w = jax.nn.softmax(pool.astype(jnp.float32), axis=-1).astype(full.dtype)
    blocks = full.reshape(bsz, nb, CMP_BLOCK, 2, NSA_KV_HEADS, NSA_HEAD_DIM)
    s = jnp.einsum('bnlchd,hl->bnchd', blocks, w)
    return s[:, :, 0], s[:, :, 1]


def cmp_attend(q, kc, vc, pos):
    bsz, L = q.shape[:2]
    nb = kc.shape[1]
    qg = q.reshape(bsz, L, NSA_KV_HEADS, NSA_GROUP, NSA_HEAD_DIM)
    s = jnp.einsum('bqhgd,bnhd->bhgqn', qg, kc).astype(jnp.float32) * ATTN_SCALE
    blk_end = (jnp.arange(nb, dtype=jnp.int32) + 1) * CMP_BLOCK - 1
    valid = blk_end[None, :] <= pos[:, None]
    s = jnp.where(valid, s, NEG)
    e = jnp.where(valid, jnp.exp(s - jnp.max(s, axis=-1, keepdims=True)), 0.0)
    p = e / jnp.maximum(jnp.sum(e, axis=-1, keepdims=True), TINY)
    o = jnp.einsum('bhgqn,bnhd->bqhgd', p.astype(vc.dtype), vc)
    return o.reshape(bsz, L, NSA_HEADS, NSA_HEAD_DIM), p


def select_blocks(p, pos, nb):
    imp = jnp.sum(p, axis=2)
    blk = jnp.arange(nb, dtype=jnp.int32)[None, :]
    cur = (pos // SEL_BLOCK)[:, None]
    forced = (blk == 0) | (blk == cur)
    score = jnp.where(forced, FORCE, jnp.where(blk > cur, NEG, imp))
    return lax.top_k(score, min(SEL_TOPN, nb))[1]


def sel_attend(q, qpos, idx, kb, vb):
    bsz, qb = q.shape[:2]
    n = idx.shape[-1]
    flat = idx.reshape(bsz, NSA_KV_HEADS, qb * n)
    bi = jnp.arange(bsz)[:, None, None]
    hi = jnp.arange(NSA_KV_HEADS)[None, :, None]
    kg = kb[bi, hi, flat].reshape(bsz, NSA_KV_HEADS, qb, n * SEL_BLOCK, NSA_HEAD_DIM)
    vg = vb[bi, hi, flat].reshape(bsz, NSA_KV_HEADS, qb, n * SEL_BLOCK, NSA_HEAD_DIM)
    kpos = (idx[..., None] * SEL_BLOCK + jnp.arange(SEL_BLOCK, dtype=jnp.int32)).reshape(bsz, NSA_KV_HEADS, qb, n * SEL_BLOCK)
    qg = q.reshape(bsz, qb, NSA_KV_HEADS, NSA_GROUP, NSA_HEAD_DIM)
    s = jnp.einsum('bqhgd,bhqsd->bhgqs', qg, kg).astype(jnp.float32) * ATTN_SCALE
    valid = (kpos <= qpos[None, None, :, None])[:, :, None]
    p = jax.nn.softmax(jnp.where(valid, s, NEG), axis=-1).astype(vg.dtype)
    o = jnp.einsum('bhgqs,bhqsd->bqhgd', p, vg)
    return o.reshape(bsz, qb, NSA_HEADS, NSA_HEAD_DIM)


def sel_prompt(q, pos, idx, kb, vb):
    bsz, L = q.shape[:2]
    nqb = L // SEL_QBLOCK
    n = idx.shape[-1]
    qs = q.reshape(bsz, nqb, SEL_QBLOCK, NSA_HEADS, NSA_HEAD_DIM).swapaxes(0, 1)
    ps = pos.reshape(nqb, SEL_QBLOCK)
    ix = idx.reshape(bsz, NSA_KV_HEADS, nqb, SEL_QBLOCK, n).transpose(2, 0, 1, 3, 4)
    ob = lax.map(lambda a: sel_attend(a[0], a[1], a[2], kb, vb), (qs, ps, ix))
    return ob.swapaxes(0, 1).reshape(bsz, L, NSA_HEADS, NSA_HEAD_DIM)


def window_prompt(q, k, v, pos):
    bsz, L = q.shape[:2]
    nqb = L // WIN_QBLOCK
    span = WINDOW + WIN_QBLOCK
    pw = ((0, 0), (WINDOW, 0), (0, 0), (0, 0))
    kp, vp = jnp.pad(k, pw), jnp.pad(v, pw)
    kidx = (jnp.arange(nqb, dtype=jnp.int32) * WIN_QBLOCK)[:, None] + jnp.arange(span, dtype=jnp.int32)[None, :]
    kblk, vblk = kp[:, kidx], vp[:, kidx]
    kpos = (kidx - WINDOW)[:, None, :]
    qpos = pos.reshape(nqb, WIN_QBLOCK)[:, :, None]
    valid = (kpos <= qpos) & (kpos >= qpos - WINDOW) & (kpos >= 0)
    qg = q.reshape(bsz, nqb, WIN_QBLOCK, NSA_KV_HEADS, NSA_GROUP, NSA_HEAD_DIM)
    s = jnp.einsum('bnqhgd,bnshd->bnhgqs', qg, kblk).astype(jnp.float32) * ATTN_SCALE
    p = jax.nn.softmax(jnp.where(valid[None, :, None, None], s, NEG), axis=-1).astype(vblk.dtype)
    o = jnp.einsum('bnhgqs,bnshd->bnqhgd', p, vblk)
    return o.reshape(bsz, L, NSA_HEADS, NSA_HEAD_DIM)


def masked_attend(q, k, v, valid):
    bsz, sq = q.shape[:2]
    qg = q.reshape(bsz, sq, NSA_KV_HEADS, NSA_GROUP, NSA_HEAD_DIM)
    s = jnp.einsum('bqhgd,bshd->bhgqs', qg, k).astype(jnp.float32) * ATTN_SCALE
    p = jax.nn.softmax(jnp.where(valid, s, NEG), axis=-1).astype(v.dtype)
    return jnp.einsum('bhgqs,bshd->bqhgd', p, v).reshape(bsz, sq, NSA_HEADS, NSA_HEAD_DIM)


def gather_pages(pool, page_table):
    g = pool[page_table]
    return g.reshape(page_table.shape[0], page_table.shape[1] * PAGE_SIZE, *pool.shape[2:])


def even_mixer(xn, pos, w_in, w_out, v_gain, ws, bs, qn, kn, pool, past):
    bsz, L, _ = xn.shape
    u, v, q, kvc, kvs, kvw, gl = jnp.split(xn @ w_in, SPLIT_EVEN, axis=-1)
    u = jax.nn.gelu(u)
    v = rmsnorm(jax.nn.gelu(v).reshape(bsz, L, A_GROUPS, A_GROUP_DIM),
                v_gain.reshape(A_GROUPS, A_GROUP_DIM)).reshape(bsz, L, A_WIDTH)
    lpad = -(-L // A_CHUNK) * A_CHUNK - L
    pw = ((0, 0), (0, lpad), (0, 0))
    a_out = gmlp_mix(jnp.pad(u, pw), jnp.pad(v, pw), ws, bs)[:, :L]
    q = rmsnorm(q.reshape(bsz, L, NSA_HEADS, NSA_HEAD_DIM), qn)
    q_rot = rope(q, pos)
    rows_c = kv_rows(kvc, kn[0], None)
    rows_s = kv_rows(kvs, kn[1], pos)
    rows_w = kv_rows(kvw, kn[2], pos)
    if past is None:
        full_c, full_s = rows_c, rows_s
    else:
        past_c, past_s, buf_w, past_len = past
        full_c = jnp.concatenate([past_c, rows_c], axis=1)
        full_s = jnp.concatenate([past_s, rows_s], axis=1)
    t = full_c.shape[1]
    tpad = -(-t // CMP_BLOCK) * CMP_BLOCK - t
    pw5 = ((0, 0), (0, tpad), (0, 0), (0, 0), (0, 0))
    full_c, full_s = jnp.pad(full_c, pw5), jnp.pad(full_s, pw5)
    nb = full_c.shape[1] // CMP_BLOCK
    k_cmp, v_cmp = compress(full_c, pool)
    o_c, p_c = cmp_attend(q, k_cmp, v_cmp, pos)
    idx = select_blocks(p_c, pos, nb)
    kb = full_s[:, :, 0].reshape(bsz, nb, SEL_BLOCK, NSA_KV_HEADS, NSA_HEAD_DIM).transpose(0, 3, 1, 2, 4)
    vb = full_s[:, :, 1].reshape(bsz, nb, SEL_BLOCK, NSA_KV_HEADS, NSA_HEAD_DIM).transpose(0, 3, 1, 2, 4)
    if past is None:
        o_s = sel_prompt(q_rot, pos, idx, kb, vb)
        o_w = window_prompt(q_rot, rows_w[:, :, 0], rows_w[:, :, 1], pos)
        new_w = rows_w[:, L - min(WINDOW, L):]
    else:
        o_s = sel_attend(q_rot, pos, idx, kb, vb)
        wb = buf_w.shape[1]
        kw = jnp.concatenate([buf_w[:, :, 0], rows_w[:, :, 0]], axis=1)
        vw = jnp.concatenate([buf_w[:, :, 1], rows_w[:, :, 1]], axis=1)
        kpos = jnp.concatenate([past_len - wb + jnp.arange(wb, dtype=jnp.int32), pos])
        valid = (kpos[None, :] <= pos[:, None]) & (kpos[None, :] >= pos[:, None] - WINDOW)
        o_w = masked_attend(q_rot, kw, vw, valid)
        new_w = rows_w
    gate = jax.nn.sigmoid(gl.astype(jnp.float32)).reshape(bsz, L, NSA_HEADS, 3, 1)
    b_out = gate[:, :, :, 0] * o_c + gate[:, :, :, 1] * o_s + gate[:, :, :, 2] * o_w
    b_out = b_out.reshape(bsz, L, B_WIDTH).astype(xn.dtype)
    y = jnp.concatenate([a_out, b_out], axis=-1) @ w_out
    return y, rows_c, rows_s, new_w, v


def segsum(a):
    t = a.shape[-1]
    cs = jnp.cumsum(a, axis=-1)
    d = cs[..., :, None] - cs[..., None, :]
    return jnp.where(jnp.tril(jnp.ones((t, t), bool)), d, NEG)


def ssd(x, dt, a, bm, cm, h0):
    f32 = jnp.float32
    b, l, h, p = x.shape
    g, n = bm.shape[2], bm.shape[3]
    e = h // g
    q = SSM_CHUNK if l % SSM_CHUNK == 0 else l
    c = l // q
    xdt = (x.astype(f32) * dt[..., None]).reshape(b, c, q, g, e, p)
    da = (dt * a).reshape(b, c, q, g, e).transpose(0, 3, 4, 1, 2)
    acs = jnp.cumsum(da, axis=-1)
    bc = bm.astype(f32).reshape(b, c, q, g, n)
    cc = cm.astype(f32).reshape(b, c, q, g, n)
    cb = jnp.einsum('bcign,bcjgn->bgcij', cc, bc)
    mix = cb[:, :, None] * jnp.exp(segsum(da))
    y_diag = jnp.einsum('bgecij,bcjgep->bcigep', mix, xdt)
    decay = jnp.exp(acs[..., -1:] - acs).transpose(0, 3, 4, 1, 2)
    st = jnp.einsum('bcjgn,bcjgep->bcgepn', bc, xdt * decay[..., None])
    st = jnp.concatenate([h0.astype(f32).reshape(b, 1, g, e, p, n), st], axis=1)
    tot = jnp.pad(acs[..., -1], ((0, 0), (0, 0), (0, 0), (1, 0)))
    st = jnp.einsum('bgezc,bcgepn->bzgepn', jnp.exp(segsum(tot)), st)
    y_off = jnp.einsum('bcign,bcgepn->bcigep', cc, st[:, :-1]) * jnp.exp(acs).transpose(0, 3, 4, 1, 2)[..., None]
    return (y_diag + y_off).reshape(b, l, h, p), st[:, -1].reshape(b, h, p, n)


def odd_mixer(xn, w_in, conv_w, conv_b, dt_bias, a_log, d_skip, norm_g, w_out, conv_buf, h0):
    bsz, L, _ = xn.shape
    z, xbc, dt = jnp.split(xn @ w_in, SPLIT_ODD, axis=-1)
    xbc, new_buf = causal_dwconv(xbc, conv_buf, conv_w, conv_b)
    xbc = jax.nn.silu(xbc)
    xs, bm, cm = jnp.split(xbc, (SSM_INNER, SSM_INNER + SSM_GROUPS * SSM_STATE), axis=-1)
    dt = jax.nn.softplus(dt.astype(jnp.float32) + dt_bias.astype(jnp.float32))
    a = -jnp.exp(a_log.astype(jnp.float32))
    xh = xs.reshape(bsz, L, SSM_HEADS, SSM_HEAD_DIM)
    y, h_t = ssd(xh, dt, a, bm.reshape(bsz, L, SSM_GROUPS, SSM_STATE), cm.reshape(bsz, L, SSM_GROUPS, SSM_STATE), h0)
    y = y + xh.astype(jnp.float32) * d_skip.astype(jnp.float32)[:, None]
    y = y.reshape(bsz, L, SSM_INNER) * jax.nn.silu(z.astype(jnp.float32))
    y = rmsnorm(y.reshape(bsz, L, SSM_GROUPS, SSM_INNER // SSM_GROUPS), norm_g.reshape(SSM_GROUPS, SSM_INNER // SSM_GROUPS))
    y = y.reshape(bsz, L, SSM_INNER).astype(xn.dtype)
    return y @ w_out, new_buf, h_t.astype(h0.dtype)


def conv_ffn(xn, w_up, conv_w, conv_b, w_down, buf):
    hu = xn @ w_up
    hc, new_buf = causal_dwconv(hu, buf, conv_w, conv_b)
    g, u = jnp.split(hc, 2, axis=-1)
    return (jax.nn.silu(g) * u) @ w_down, new_buf


def setup_inputs(seed: int = 0) -> dict:
    key = jax.random.key(seed)
    ks = jax.random.split(key, 40)
    f32 = jnp.float32

    def nrm(i, shape, scale):
        return jax.random.normal(ks[i], shape, f32) * scale

    n_pages = PAST_LEN // PAGE_SIZE
    n_pool = (DEC_BATCH * n_pages * 5) // 4
    win_buf = min(WINDOW, PAST_LEN)
    page_table = jax.random.permutation(ks[0], n_pool)[: DEC_BATCH * n_pages].reshape(DEC_BATCH, n_pages).astype(jnp.int32)
    dt0 = jnp.exp(jax.random.uniform(ks[1], (N_ODD, SSM_HEADS), f32, math.log(1e-3), math.log(1e-1)))
    dt_bias = dt0 + jnp.log(-jnp.expm1(-dt0))
    a_log = jnp.log(jax.random.uniform(ks[2], (N_ODD, SSM_HEADS), f32, 1.0, 16.0))
    kvshape = (NSA_KV_HEADS, NSA_HEAD_DIM)
    return {
        'x_prompt': nrm(3, (BATCH, SEQ, D_MODEL), 1.0),
        'x_sample': nrm(4, (DEC_BATCH, DEC_SEQ, D_MODEL), 1.0),
        'cache_kv_cmp': nrm(5, (N_EVEN, n_pool, PAGE_SIZE, 2) + kvshape, 1.0),
        'cache_kv_sel': nrm(6, (N_EVEN, n_pool, PAGE_SIZE, 2) + kvshape, 1.0),
        'cache_kv_win': nrm(7, (N_EVEN, DEC_BATCH, win_buf, 2) + kvshape, 1.0),
        'state_ssm_conv': nrm(8, (N_ODD, DEC_BATCH, SSM_CONV - 1, SSM_CONV_DIM), 1.0),
        'state_ssm': nrm(9, (N_ODD, DEC_BATCH, SSM_HEADS, SSM_HEAD_DIM, SSM_STATE), 0.1),
        'state_ffn_conv': nrm(10, (DEPTH, DEC_BATCH, FFN_CONV - 1, 2 * D_FF), 1.0),
        'page_table': page_table,
        'norm_mix': 1.0 + nrm(11, (DEPTH, D_MODEL), 0.1),
        'norm_ffn': 1.0 + nrm(12, (DEPTH, D_MODEL), 0.1),
        'w_in_even': nrm(13, (N_EVEN, D_MODEL, E_IN), D_MODEL ** -0.5),
        'w_out_even': nrm(14, (N_EVEN, MIX_OUT, D_MODEL), MIX_OUT ** -0.5),
        'gmlp_v_norm': 1.0 + nrm(15, (N_EVEN, A_WIDTH), 0.1),
        'gmlp_ws': nrm(16, (N_EVEN, A_GROUPS, A_CHUNK, A_CHUNK), A_CHUNK ** -0.5),
        'gmlp_bs': 1.0 + nrm(17, (N_EVEN, A_GROUPS, A_CHUNK), 0.1),
        'q_norm': 1.0 + nrm(18, (N_EVEN, NSA_HEAD_DIM), 0.1),
        'k_norm': 1.0 + nrm(19, (N_EVEN, 3, NSA_HEAD_DIM), 0.1),
        'cmp_pool': nrm(20, (N_EVEN, NSA_KV_HEADS, CMP_BLOCK), 0.1),
        'w_in_odd': nrm(21, (N_ODD, D_MODEL, C_IN), D_MODEL ** -0.5),
        'ssm_conv_w': nrm(22, (N_ODD, SSM_CONV, SSM_CONV_DIM), SSM_CONV ** -0.5),
        'ssm_conv_b': nrm(23, (N_ODD, SSM_CONV_DIM), 0.01),
        'ssm_dt_bias': dt_bias,
        'ssm_a_log': a_log,
        'ssm_d': 1.0 + nrm(24, (N_ODD, SSM_HEADS), 0.1),
        'ssm_norm': 1.0 + nrm(25, (N_ODD, SSM_INNER), 0.1),
        'w_out_odd': nrm(26, (N_ODD, SSM_INNER, D_MODEL), SSM_INNER ** -0.5),
        'ffn_w_up': nrm(27, (DEPTH, D_MODEL, 2 * D_FF), D_MODEL ** -0.5),
        'ffn_conv_w': nrm(28, (DEPTH, FFN_CONV, 2 * D_FF), FFN_CONV ** -0.5),
        'ffn_conv_b': nrm(29, (DEPTH, 2 * D_FF), 0.01),
        'ffn_w_down': nrm(30, (DEPTH, D_FF, D_MODEL), D_FF ** -0.5),
    }


def reference(x_prompt, x_sample, cache_kv_cmp, cache_kv_sel, cache_kv_win, state_ssm_conv, state_ssm,
              state_ffn_conv, page_table, norm_mix, norm_ffn, w_in_even, w_out_even, gmlp_v_norm, gmlp_ws,
              gmlp_bs, q_norm, k_norm, cmp_pool, w_in_odd, ssm_conv_w, ssm_conv_b, ssm_dt_bias, ssm_a_log,
              ssm_d, ssm_norm, w_out_odd, ffn_w_up, ffn_conv_w, ffn_conv_b, ffn_w_down):
    bp, sp = x_prompt.shape[:2]
    ss = x_sample.shape[1]
    past_len = page_table.shape[1] * PAGE_SIZE
    pos_p = jnp.arange(sp, dtype=jnp.int32)
    pos_s = past_len + jnp.arange(ss, dtype=jnp.int32)
    hp, hs = x_prompt, x_sample
    p_cmp, p_sel, p_win, s_cmp, s_sel, s_win, s_v = [], [], [], [], [], [], []
    p_sconv, p_sst, s_sconv, s_sst, p_fconv, s_fconv = [], [], [], [], [], []
    for layer in range(DEPTH):
        i = layer // 2
        xp = rmsnorm(hp, norm_mix[layer])
        xs = rmsnorm(hs, norm_mix[layer])
        if layer % 2 == 0:
            wts = (w_in_even[i], w_out_even[i], gmlp_v_norm[i], gmlp_ws[i], gmlp_bs[i], q_norm[i], k_norm[i], cmp_pool[i])
            mp, rc, rs, rw, _ = even_mixer(xp, pos_p, *wts, None)
            past = (gather_pages(cache_kv_cmp[i], page_table), gather_pages(cache_kv_sel[i], page_table),
                    cache_kv_win[i], past_len)
            ms, qc, qs, qw, qv = even_mixer(xs, pos_s, *wts, past)
            p_cmp.append(rc)
            p_sel.append(rs)
            p_win.append(rw)
            s_cmp.append(qc)
            s_sel.append(qs)
            s_win.append(qw)
            s_v.append(qv)
        else:
            wts = (w_in_odd[i], ssm_conv_w[i], ssm_conv_b[i], ssm_dt_bias[i], ssm_a_log[i], ssm_d[i], ssm_norm[i], w_out_odd[i])
            mp, cbp, stp = odd_mixer(xp, *wts, jnp.zeros((bp, SSM_CONV - 1, SSM_CONV_DIM), xp.dtype),
                                     jnp.zeros((bp, SSM_HEADS, SSM_HEAD_DIM, SSM_STATE), xp.dtype))
            ms, cbs, sts = odd_mixer(xs, *wts, state_ssm_conv[i], state_ssm[i])
            p_sconv.append(cbp)
            p_sst.append(stp)
            s_sconv.append(cbs)
            s_sst.append(sts)
        hp = hp + mp
        hs = hs + ms
        fw = (ffn_w_up[layer], ffn_conv_w[layer], ffn_conv_b[layer], ffn_w_down[layer])
        fp, fbp = conv_ffn(rmsnorm(hp, norm_ffn[layer]), *fw, jnp.zeros((bp, FFN_CONV - 1, 2 * D_FF), hp.dtype))
        fs, fbs = conv_ffn(rmsnorm(hs, norm_ffn[layer]), *fw, state_ffn_conv[layer])
        p_fconv.append(fbp)
        s_fconv.append(fbs)
        hp = hp + fp
        hs = hs + fs
    return (hp, hs, jnp.stack(p_cmp), jnp.stack(p_sel), jnp.stack(p_win), jnp.stack(p_sconv), jnp.stack(p_sst),
            jnp.stack(p_fconv), jnp.stack(s_cmp), jnp.stack(s_sel), jnp.stack(s_win), jnp.stack(s_v),
            jnp.stack(s_sconv), jnp.stack(s_sst), jnp.stack(s_fconv))
```

```python
import functools
import math

import numpy as np
import jax
import jax.numpy as jnp
from jax import lax
from jax.experimental import pallas as pl
from jax.experimental.pallas import tpu as pltpu

F32 = jnp.float32
BF16 = jnp.bfloat16

D_MODEL = 2048
A_WIDTH = D_MODEL // 2
A_GROUPS = 8
A_CHUNK = 128
NSA_HEADS = 8
HEAD_DIM = 128
KV_HEADS = 2
NSA_GROUP = NSA_HEADS // KV_HEADS
B_WIDTH = NSA_HEADS * HEAD_DIM
KV_COLS = 2 * KV_HEADS * HEAD_DIM
CMP_BLOCK = 64
SEL_TOPN = 16
WINDOW = 512
PAGE_SIZE = 128
ROPE_THETA = 10000.0
ATTN_SCALE = HEAD_DIM ** -0.5
SSM_INNER = 2 * D_MODEL
SSM_HEAD_DIM = 64
SSM_HEADS = SSM_INNER // SSM_HEAD_DIM
SSM_GROUPS = 8
SSM_GROUP_HEADS = SSM_HEADS // SSM_GROUPS
SSM_GROUP_WIDTH = SSM_INNER // SSM_GROUPS
SSM_STATE = 128
SSM_CONV = 4
SSM_CHUNK = 128
SSM_CONV_DIM = SSM_INNER + 2 * SSM_GROUPS * SSM_STATE
D_FF = 5632
FFN_CONV = 3
EPS = 1e-6
NEG = -1e30
FORCE = 1e4
TINY = 1e-30

VMEM_LIMIT_BYTES = 56 * 1024 * 1024
LANES = 128
SUBLANES = 8

E_PAD = 5120
GATE_OFF = 2 * A_WIDTH + B_WIDTH + 3 * KV_COLS
FFN_TILE = 1024
FFN_HALF = FFN_TILE // 2
FFN_TILES = (2 * D_FF) // FFN_TILE


def _cparams(sem):
    return pltpu.CompilerParams(dimension_semantics=sem, vmem_limit_bytes=VMEM_LIMIT_BYTES)


def _dot(a, b):
    return jnp.dot(a, b, preferred_element_type=F32)


def _dot_nt(a, b):
    return lax.dot_general(a, b, (((1,), (1,)), ((), ())), preferred_element_type=F32)


def _iota(shape, dim):
    return lax.broadcasted_iota(jnp.int32, shape, dim)


def _split3(x):
    hi = x.astype(BF16)
    r1 = x - hi.astype(F32)
    mid = r1.astype(BF16)
    lo = (r1 - mid.astype(F32)).astype(BF16)
    return hi, mid, lo


def _dot3_l(x, m):
    hi, mid, lo = _split3(x)
    return _dot(hi, m) + _dot(mid, m) + _dot(lo, m)


def _dot3_r(m, x):
    hi, mid, lo = _split3(x)
    return _dot(m, hi) + _dot(m, mid) + _dot(m, lo)


def _rms_kernel(x_ref, g_ref, o_ref):
    x = x_ref[...]
    ms = jnp.mean(x * x, axis=-1, keepdims=True)
    o_ref[...] = (x * lax.rsqrt(ms + EPS) * g_ref[...]).astype(o_ref.dtype)


def rms_cast(x, g):
    m, d = x.shape
    tr = min(m, 512)
    return pl.pallas_call(
        _rms_kernel,
        grid=(m // tr,),
        in_specs=[pl.BlockSpec((tr, d), lambda i: (i, 0)), pl.BlockSpec((1, d), lambda i: (0, 0))],
        out_specs=pl.BlockSpec((tr, d), lambda i: (i, 0)),
        out_shape=jax.ShapeDtypeStruct((m, d), BF16),
        compiler_params=_cparams(("parallel",)),
        name="rms_cast",
    )(x, g.reshape(1, d))


def _mm_kernel(x_ref, w_ref, o_ref):
    o_ref[...] = _dot(x_ref[...], w_ref[...]).astype(o_ref.dtype)


def _mm_res_kernel(x_ref, w_ref, r_ref, o_ref):
    o_ref[...] = (r_ref[...] + _dot(x_ref[...], w_ref[...])).astype(o_ref.dtype)


def matmul(x, w, *, bn, res=None, out_dtype=F32, name="mm"):
    m, k = x.shape
    n = w.shape[1]
    bm = min(m, 1024)
    in_specs = [pl.BlockSpec((bm, k), lambda i, j: (i, 0)), pl.BlockSpec((k, bn), lambda i, j: (0, j))]
    args = [x, w]
    kern = _mm_kernel
    if res is not None:
        in_specs.append(pl.BlockSpec((bm, bn), lambda i, j: (i, j)))
        args.append(res)
        kern = _mm_res_kernel
    return pl.pallas_call(
        kern,
        grid=(m // bm, n // bn),
        in_specs=in_specs,
        out_specs=pl.BlockSpec((bm, bn), lambda i, j: (i, j)),
        out_shape=jax.ShapeDtypeStruct((m, n), out_dtype),
        compiler_params=_cparams(("parallel", "parallel")),
        name=name,
    )(*args)


def _head_rms(x, gain):
    ms = jnp.mean(x * x, axis=-1, keepdims=True)
    return x * lax.rsqrt(ms + EPS) * gain


def _rope(x, cosf, sinf):
    return x * cosf + pltpu.roll(x, HEAD_DIM // 2, axis=1) * sinf


def _even_prep_kernel(p_ref, cos_ref, sin_ref, vg_ref, qn_ref, kn_ref,
                      u_ref, v_ref, q_ref, qr_ref, rc_ref, rs_ref, rw_ref, s16_ref, w16_ref, gate_ref):
    cosf = cos_ref[...]
    sinf = sin_ref[...]
    u_ref[...] = jax.nn.gelu(p_ref[:, 0:A_WIDTH]).astype(u_ref.dtype)
    for g in range(A_GROUPS):
        sl = slice(g * LANES, (g + 1) * LANES)
        vg = jax.nn.gelu(p_ref[:, A_WIDTH + g * LANES:A_WIDTH + (g + 1) * LANES])
        v_ref[:, sl] = _head_rms(vg, vg_ref[:, sl])
    qgain = qn_ref[...]
    for h in range(NSA_HEADS):
        sl = slice(h * LANES, (h + 1) * LANES)
        q = _head_rms(p_ref[:, 2 * A_WIDTH + h * LANES:2 * A_WIDTH + (h + 1) * LANES], qgain)
        q_ref[:, sl] = q.astype(q_ref.dtype)
        qr_ref[:, sl] = _rope(q, cosf, sinf).astype(qr_ref.dtype)
    base = 2 * A_WIDTH + B_WIDTH
    for which, (row_ref, row16_ref) in enumerate(((rc_ref, None), (rs_ref, s16_ref), (rw_ref, w16_ref))):
        gain = kn_ref[which:which + 1, :]
        off = base + which * KV_COLS
        for h in range(KV_HEADS):
            sl = slice(h * LANES, (h + 1) * LANES)
            k = _head_rms(p_ref[:, off + h * LANES:off + (h + 1) * LANES], gain)
            if which > 0:
                k = _rope(k, cosf, sinf)
            row_ref[:, sl] = k
            if row16_ref is not None:
                row16_ref[:, sl] = k.astype(BF16)
        vsl = slice(KV_HEADS * LANES, KV_COLS)
        vals = p_ref[:, off + KV_HEADS * LANES:off + KV_COLS]
        row_ref[:, vsl] = vals
        if row16_ref is not None:
            row16_ref[:, vsl] = vals.astype(BF16)
    gate_ref[...] = jax.nn.sigmoid(p_ref[:, GATE_OFF:GATE_OFF + KV_HEADS * LANES])


def even_prep(proj, cosf, sinf, v_gain, q_gain, k_gain):
    m = proj.shape[0]
    tr = min(m, 256)
    row = lambda w: pl.BlockSpec((tr, w), lambda i: (i, 0))
    full = lambda r, w: pl.BlockSpec((r, w), lambda i: (0, 0))
    out_shapes = [
        jax.ShapeDtypeStruct((m, A_WIDTH), BF16),
        jax.ShapeDtypeStruct((m, A_WIDTH), F32),
        jax.ShapeDtypeStruct((m, B_WIDTH), BF16),
        jax.ShapeDtypeStruct((m, B_WIDTH), BF16),
        jax.ShapeDtypeStruct((m, KV_COLS), F32),
        jax.ShapeDtypeStruct((m, KV_COLS), F32),
        jax.ShapeDtypeStruct((m, KV_COLS), F32),
        jax.ShapeDtypeStruct((m, KV_COLS), BF16),
        jax.ShapeDtypeStruct((m, KV_COLS), BF16),
        jax.ShapeDtypeStruct((m, KV_HEADS * LANES), F32),
    ]
    return pl.pallas_call(
        _even_prep_kernel,
        grid=(m // tr,),
        in_specs=[row(E_PAD), row(LANES), row(LANES), full(1, A_WIDTH), full(1, LANES), full(3, LANES)],
        out_specs=[row(s.shape[1]) for s in out_shapes],
        out_shape=out_shapes,
        compiler_params=_cparams(("parallel",)),
        name="even_prep",
    )(proj, cosf, sinf, v_gain.reshape(1, A_WIDTH), q_gain.reshape(1, LANES), k_gain)


def _gmlp_kernel(u_ref, v_ref, ws_ref, bst_ref, o_ref):
    tril = _iota((A_CHUNK, A_CHUNK), 0) >= _iota((A_CHUNK, A_CHUNK), 1)
    for g in range(A_GROUPS):
        sl = slice(g * LANES, (g + 1) * LANES)
        wm = jnp.where(tril, ws_ref[g], 0.0).astype(BF16)
        s = _dot(wm, v_ref[:, sl].astype(BF16)) + bst_ref[:, g:g + 1]
        o_ref[:, sl] = (u_ref[:, sl].astype(F32) * s).astype(o_ref.dtype)


def gmlp(u, v, ws, bs):
    m = u.shape[0]
    row = pl.BlockSpec((A_CHUNK, A_WIDTH), lambda i: (i, 0))
    return pl.pallas_call(
        _gmlp_kernel,
        grid=(m // A_CHUNK,),
        in_specs=[row, row,
                  pl.BlockSpec((A_GROUPS, A_CHUNK, A_CHUNK), lambda i: (0, 0, 0)),
                  pl.BlockSpec((A_CHUNK, A_GROUPS), lambda i: (0, 0))],
        out_specs=row,
        out_shape=jax.ShapeDtypeStruct((m, A_WIDTH), BF16),
        compiler_params=_cparams(("parallel",)),
        name="gmlp",
    )(u, v, ws, bs.T)


def _pool_weights(pool_t):
    m = jnp.max(pool_t, axis=0, keepdims=True)
    e = jnp.exp(pool_t - m)
    return e / jnp.sum(e, axis=0, keepdims=True)


def _compress_kernel(x_ref, pool_ref, o_ref):
    w = _pool_weights(pool_ref[...])
    x = x_ref[...]
    nb = x.shape[0] // CMP_BLOCK
    o_ref[...] = jnp.sum(x.reshape(nb, CMP_BLOCK, KV_COLS) * w[None], axis=1)


def _pool_cols(pool):
    pt = pool.T
    return jnp.concatenate([jnp.repeat(pt, LANES, axis=1)] * 2, axis=1)


def compress(rows, pool):
    m = rows.shape[0]
    tr = min(m, 1024)
    return pl.pallas_call(
        _compress_kernel,
        grid=(m // tr,),
        in_specs=[pl.BlockSpec((tr, KV_COLS), lambda i: (i, 0)),
                  pl.BlockSpec((CMP_BLOCK, KV_COLS), lambda i: (0, 0))],
        out_specs=pl.BlockSpec((tr // CMP_BLOCK, KV_COLS), lambda i: (i, 0)),
        out_shape=jax.ShapeDtypeStruct((m // CMP_BLOCK, KV_COLS), F32),
        compiler_params=_cparams(("parallel",)),
        name="compress",
    )(rows, _pool_cols(pool))


NSA_TQ = 128
SEL_KC = 256


def _online_step(carry, s, mask, v):
    m, l, acc = carry
    s = jnp.where(mask, s, NEG)
    m_new = jnp.maximum(m, jnp.max(s, axis=-1, keepdims=True))
    p = jnp.where(mask, jnp.exp(s - m_new), 0.0)
    alpha = jnp.exp(m - m_new)
    l = alpha * l + jnp.sum(p, axis=-1, keepdims=True)
    acc = alpha * acc + _dot(p.astype(BF16), v)
    return m_new, l, acc


def _nsa_prompt_kernel(qn_ref, qr_ref, kc_ref, vc_ref, ks_ref, vs_ref, kw_ref, vw_ref, gate_ref, o_ref, *, seq):
    t = pl.program_id(2)
    nb = seq // CMP_BLOCK
    rows = NSA_GROUP * NSA_TQ
    qpos = t * NSA_TQ + _iota((NSA_TQ, 1), 0)
    blk = _iota((1, nb), 1)

    kc = kc_ref[0].astype(BF16)
    vc = vc_ref[0].astype(BF16)
    valid_c = ((blk + 1) * CMP_BLOCK - 1) <= qpos
    imp = jnp.zeros((NSA_TQ, nb), F32)
    o_cmp = []
    for g in range(NSA_GROUP):
        q = qn_ref[:, g * LANES:(g + 1) * LANES]
        s = jnp.where(valid_c, _dot_nt(q, kc) * ATTN_SCALE, NEG)
        e = jnp.where(valid_c, jnp.exp(s - jnp.max(s, axis=-1, keepdims=True)), 0.0)
        p = e / jnp.maximum(jnp.sum(e, axis=-1, keepdims=True), TINY)
        imp = imp + p
        o_cmp.append(_dot(p.astype(BF16), vc))

    cur = qpos // CMP_BLOCK
    forced = (blk == 0) | (blk == cur)
    score = jnp.where(forced, FORCE, jnp.where(blk > cur, NEG, imp))
    rank = jnp.zeros((NSA_TQ, nb), jnp.int32)
    for j in range(nb):
        col = score[:, j:j + 1]
        beats = (col > score) | ((col == score) & (blk > j))
        rank = rank + beats.astype(jnp.int32)
    sel = (rank < min(SEL_TOPN, nb)).astype(BF16)

    q4 = jnp.concatenate([qr_ref[:, g * LANES:(g + 1) * LANES] for g in range(NSA_GROUP)], axis=0)
    qpos4 = jnp.concatenate([qpos] * NSA_GROUP, axis=0)
    sel4 = jnp.concatenate([sel] * NSA_GROUP, axis=0)
    init = (jnp.full((rows, 1), NEG, F32), jnp.zeros((rows, 1), F32), jnp.zeros((rows, HEAD_DIM), F32))

    def sel_body(c, carry):
        start = pl.multiple_of(c * SEL_KC, SEL_KC)
        k = ks_ref[pl.ds(start, SEL_KC), :]
        v = vs_ref[pl.ds(start, SEL_KC), :]
        kpos = c * SEL_KC + _iota((1, SEL_KC), 1)
        expand = (_iota((nb, SEL_KC), 0) == (c * SEL_KC + _iota((nb, SEL_KC), 1)) // CMP_BLOCK).astype(BF16)
        chosen = _dot(sel4, expand) > 0.5
        mask = chosen & (kpos <= qpos4)
        return _online_step(carry, _dot_nt(q4, k) * ATTN_SCALE, mask, v)

    n_sel = (t * NSA_TQ + NSA_TQ + SEL_KC - 1) // SEL_KC
    _, l_s, acc_s = lax.fori_loop(0, n_sel, sel_body, init)
    o_sel = acc_s / l_s

    def win_body(c, carry):
        start = pl.multiple_of(c * NSA_TQ, NSA_TQ)
        k = kw_ref[pl.ds(start, NSA_TQ), :]
        v = vw_ref[pl.ds(start, NSA_TQ), :]
        kpos = c * NSA_TQ + _iota((1, NSA_TQ), 1)
        mask = (kpos <= qpos4) & (kpos >= qpos4 - WINDOW)
        return _online_step(carry, _dot_nt(q4, k) * ATTN_SCALE, mask, v)

    lo = jnp.maximum(t - WINDOW // NSA_TQ, 0)
    _, l_w, acc_w = lax.fori_loop(lo, t + 1, win_body, init)
    o_win = acc_w / l_w

    gate = gate_ref[...]
    for g in range(NSA_GROUP):
        rs = slice(g * NSA_TQ, (g + 1) * NSA_TQ)
        out = (gate[:, 3 * g:3 * g + 1] * o_cmp[g] + gate[:, 3 * g + 1:3 * g + 2] * o_sel[rs]
               + gate[:, 3 * g + 2:3 * g + 3] * o_win[rs])
        o_ref[:, g * LANES:(g + 1) * LANES] = out.astype(o_ref.dtype)


def nsa_prompt(qn, qr, kcmp, s16, w16, gate, bsz, seq):
    nt = seq // NSA_TQ
    nb = seq // CMP_BLOCK
    qspec = pl.BlockSpec((NSA_TQ, NSA_GROUP * LANES), lambda b, h, t: (b * nt + t, h))
    cmp_k = pl.BlockSpec((1, nb, LANES), lambda b, h, t: (b, 0, h))
    cmp_v = pl.BlockSpec((1, nb, LANES), lambda b, h, t: (b, 0, KV_HEADS + h))
    key = pl.BlockSpec((seq, LANES), lambda b, h, t: (b, h))
    val = pl.BlockSpec((seq, LANES), lambda b, h, t: (b, KV_HEADS + h))
    kc3 = kcmp.reshape(bsz, nb, KV_COLS)
    return pl.pallas_call(
        functools.partial(_nsa_prompt_kernel, seq=seq),
        grid=(bsz, KV_HEADS, nt),
        in_specs=[qspec, qspec, cmp_k, cmp_v, key, val, key, val,
                  pl.BlockSpec((NSA_TQ, LANES), lambda b, h, t: (b * nt + t, h))],
        out_specs=qspec,
        out_shape=jax.ShapeDtypeStruct((bsz * seq, B_WIDTH), BF16),
        compiler_params=_cparams(("parallel", "parallel", "arbitrary")),
        name="nsa_prompt",
    )(qn, qr, kc3, kc3, s16, s16, w16, w16, gate)


PAGES_PER_STEP = 8


def _pool_past_kernel(pt_ref, *refs):
    page_refs, pool_ref, o_ref = refs[:PAGES_PER_STEP], refs[PAGES_PER_STEP], refs[PAGES_PER_STEP + 1]
    w = _pool_weights(pool_ref[...])
    per_page = PAGE_SIZE // CMP_BLOCK
    for k, ref in enumerate(page_refs):
        x = ref[0].reshape(per_page, CMP_BLOCK, KV_COLS)
        o_ref[0, k * per_page:(k + 1) * per_page, :] = jnp.sum(x * w[None], axis=1)


def pool_past(cache, layer_off, page_table, pool):
    bsz, n_pages = page_table.shape
    steps = n_pages // PAGES_PER_STEP
    per_page = PAGE_SIZE // CMP_BLOCK

    def page_spec(k):
        return pl.BlockSpec((1, PAGE_SIZE, KV_COLS),
                            lambda b, s, pt: (layer_off + pt[b * n_pages + s * PAGES_PER_STEP + k], 0, 0))

    grid_spec = pltpu.PrefetchScalarGridSpec(
        num_scalar_prefetch=1,
        grid=(bsz, steps),
        in_specs=[page_spec(k) for k in range(PAGES_PER_STEP)]
        + [pl.BlockSpec((CMP_BLOCK, KV_COLS), lambda b, s, pt: (0, 0))],
        out_specs=pl.BlockSpec((1, PAGES_PER_STEP * per_page, KV_COLS), lambda b, s, pt: (b, s, 0)),
    )
    return pl.pallas_call(
        _pool_past_kernel,
        grid_spec=grid_spec,
        out_shape=jax.ShapeDtypeStruct((bsz, n_pages * per_page, KV_COLS), F32),
        compiler_params=_cparams(("parallel", "arbitrary")),
        name="pool_past",
    )(page_table.reshape(-1), *([cache] * PAGES_PER_STEP), _pool_cols(pool))


def _sample_select_kernel(q_ref, kc_ref, oc_ref, idx_ref, *, dec):
    nbp = kc_ref.shape[1]
    blk = _iota((dec, nbp), 1)
    lane = _iota((dec, SEL_TOPN), 1)
    for h in range(KV_HEADS):
        kc = kc_ref[0, :, h * LANES:(h + 1) * LANES].astype(BF16)
        vc = kc_ref[0, :, (KV_HEADS + h) * LANES:(KV_HEADS + h + 1) * LANES].astype(BF16)
        imp = jnp.zeros((dec, nbp), F32)
        for g in range(NSA_GROUP):
            hd = h * NSA_GROUP + g
            q = q_ref[0, :, hd * LANES:(hd + 1) * LANES]
            s = _dot_nt(q, kc) * ATTN_SCALE
            e = jnp.exp(s - jnp.max(s, axis=-1, keepdims=True))
            p = e / jnp.maximum(jnp.sum(e, axis=-1, keepdims=True), TINY)
            imp = imp + p
            oc_ref[0, :, hd * LANES:(hd + 1) * LANES] = _dot(p.astype(BF16), vc)
        score = jnp.where(blk == 0, -jnp.inf, imp)
        picked = jnp.zeros((dec, SEL_TOPN), jnp.int32)
        for r in range(SEL_TOPN - 2):
            best = jnp.max(score, axis=-1, keepdims=True)
            arg = jnp.min(jnp.where(score == best, blk, nbp), axis=-1, keepdims=True)
            picked = jnp.where(lane == r + 1, arg, picked)
            score = jnp.where(blk == arg, -jnp.inf, score)
        idx_ref[0, h * dec:(h + 1) * dec, :] = picked


def sample_select(qn, kcmp_past):
    bsz, dec, _ = qn.shape
    nbp = kcmp_past.shape[1]
    return pl.pallas_call(
        functools.partial(_sample_select_kernel, dec=dec),
        grid=(bsz,),
        in_specs=[pl.BlockSpec((1, dec, B_WIDTH), lambda b: (b, 0, 0)),
                  pl.BlockSpec((1, nbp, KV_COLS), lambda b: (b, 0, 0))],
        out_specs=[pl.BlockSpec((1, dec, B_WIDTH), lambda b: (b, 0, 0)),
                   pl.BlockSpec((1, KV_HEADS * dec, SEL_TOPN), lambda b: (b, 0, 0))],
        out_shape=[jax.ShapeDtypeStruct((bsz, dec, B_WIDTH), F32),
                   jax.ShapeDtypeStruct((bsz, KV_HEADS * dec, SEL_TOPN), jnp.int32)],
        compiler_params=_cparams(("parallel",)),
        name="sample_select",
    )(qn, kcmp_past)


def _rows_to_tile(rows):
    r = _iota((SUBLANES, rows[0].shape[1]), 0)
    out = jnp.zeros((SUBLANES, rows[0].shape[1]), rows[0].dtype)
    for i, x in enumerate(rows):
        out = jnp.where(r == i, jnp.broadcast_to(x, out.shape), out)
    return out


def _sample_attn_kernel(idx_ref, pt_ref, *refs, dec):
    n = SEL_TOPN
    k_refs, v_refs = refs[:n], refs[n:2 * n]
    (qr_ref, ksn_ref, vsn_ref, kwb_ref, vwb_ref, kwn_ref, vwn_ref, gate_ref, oc_ref, o_ref) = refs[2 * n:]
    qi = pl.program_id(2)
    q8 = _rows_to_tile([qr_ref[0, 0, :, g * LANES:(g + 1) * LANES].astype(F32)
                        for g in range(NSA_GROUP)]).astype(BF16)
    newpos = _iota((1, dec), 1)

    def attend(k_old, v_old, mask_old, k_new, v_new):
        s_old = jnp.where(mask_old, _dot_nt(q8, k_old) * ATTN_SCALE, NEG)
        mask_new = newpos <= qi
        s_new = jnp.where(mask_new, _dot_nt(q8, k_new) * ATTN_SCALE, NEG)
        m = jnp.maximum(jnp.max(s_old, axis=-1, keepdims=True), jnp.max(s_new, axis=-1, keepdims=True))
        p_old = jnp.where(mask_old, jnp.exp(s_old - m), 0.0)
        p_new = jnp.where(mask_new, jnp.exp(s_new - m), 0.0)
        l = jnp.sum(p_old, axis=-1, keepdims=True) + jnp.sum(p_new, axis=-1, keepdims=True)
        return (_dot(p_old.astype(BF16), v_old) + _dot(p_new.astype(BF16), v_new)) / l

    k_sel = jnp.concatenate([r[0].astype(BF16) for r in k_refs], axis=0)
    v_sel = jnp.concatenate([r[0].astype(BF16) for r in v_refs], axis=0)
    mask_sel = _iota((1, n * CMP_BLOCK), 1) < (n - 1) * CMP_BLOCK
    o_sel = attend(k_sel, v_sel, mask_sel, ksn_ref[0].astype(BF16), vsn_ref[0].astype(BF16))

    wb = kwb_ref.shape[1]
    mask_win = _iota((1, wb), 1) >= qi + (wb - WINDOW)
    o_win = attend(kwb_ref[0].astype(BF16), vwb_ref[0].astype(BF16), mask_win,
                   kwn_ref[0].astype(BF16), vwn_ref[0].astype(BF16))

    gate = gate_ref[0, 0]
    outs = []
    for g in range(NSA_GROUP):
        outs.append(gate[:, 3 * g:3 * g + 1] * oc_ref[0, 0, :, g * LANES:(g + 1) * LANES]
                    + gate[:, 3 * g + 1:3 * g + 2] * o_sel[g:g + 1]
                    + gate[:, 3 * g + 2:3 * g + 3] * o_win[g:g + 1])
    o_ref[0, 0] = jnp.concatenate(outs, axis=1).astype(o_ref.dtype)


def sample_attn(idx, page_table, cache_sel, sel_off, cache_win, win_off, qr, rows_s, rows_w, gate, o_cmp):
    bsz, dec, _ = rows_s.shape
    n_pages = page_table.shape[1]
    per_page = PAGE_SIZE // CMP_BLOCK
    wb = cache_win.shape[1]
    grp = NSA_GROUP * LANES

    def phys(b, h, q, k, idx_r, pt_r):
        blk = idx_r[((b * KV_HEADS + h) * dec + q) * SEL_TOPN + k]
        return (sel_off + pt_r[b * n_pages + blk // per_page]) * per_page + blk % per_page

    def kspec(k, col0):
        return pl.BlockSpec((1, CMP_BLOCK, LANES),
                            lambda b, h, q, idx_r, pt_r: (phys(b, h, q, k, idx_r, pt_r), 0, col0 + h))

    qmap = lambda b, h, q, idx_r, pt_r: (b, q, 0, h)
    newk = pl.BlockSpec((1, dec, LANES), lambda b, h, q, idx_r, pt_r: (b, 0, h))
    newv = pl.BlockSpec((1, dec, LANES), lambda b, h, q, idx_r, pt_r: (b, 0, KV_HEADS + h))
    bufk = pl.BlockSpec((1, wb, LANES), lambda b, h, q, idx_r, pt_r: (win_off + b, 0, h))
    bufv = pl.BlockSpec((1, wb, LANES), lambda b, h, q, idx_r, pt_r: (win_off + b, 0, KV_HEADS + h))
    grid_spec = pltpu.PrefetchScalarGridSpec(
        num_scalar_prefetch=2,
        grid=(bsz, KV_HEADS, dec),
        in_specs=[kspec(k, 0) for k in range(SEL_TOPN)] + [kspec(k, KV_HEADS) for k in range(SEL_TOPN)]
        + [pl.BlockSpec((1, 1, 1, grp), qmap), newk, newv, bufk, bufv, newk, newv,
           pl.BlockSpec((1, 1, 1, LANES), qmap), pl.BlockSpec((1, 1, 1, grp), qmap)],
        out_specs=pl.BlockSpec((1, 1, 1, grp), qmap),
    )
    q4 = lambda a: a.reshape(bsz, dec, 1, a.shape[-1])
    return pl.pallas_call(
        functools.partial(_sample_attn_kernel, dec=dec),
        grid_spec=grid_spec,
        out_shape=jax.ShapeDtypeStruct((bsz, dec, 1, B_WIDTH), BF16),
        compiler_params=_cparams(("parallel", "parallel", "arbitrary")),
        name="sample_attn",
    )(idx.reshape(-1), page_table.reshape(-1), *([cache_sel] * (2 * SEL_TOPN)),
      q4(qr), rows_s, rows_s, cache_win, cache_win, rows_w, rows_w, q4(gate), q4(o_cmp))


def _shift_rows(x, prev, k):
    xr = pltpu.roll(x, k, axis=0)
    pr = pltpu.roll(prev, k, axis=0)
    top = jnp.where(_iota((SUBLANES, 1), 0) < k, pr, xr[:SUBLANES])
    return jnp.concatenate([top, xr[SUBLANES:]], axis=0)


SSM_PREP_ROWS = 256
SSM_PREP_COLS = 1024


def _ssm_prep_kernel(x_ref, w_ref, b_ref, o_ref, tail_ref, carry_ref):
    @pl.when(pl.program_id(1) == 0)
    def _():
        carry_ref[...] = jnp.zeros_like(carry_ref)

    for c0 in range(0, SSM_CONV_DIM, SSM_PREP_COLS):
        cs = slice(c0, c0 + SSM_PREP_COLS)
        x = x_ref[:, cs]
        prev = carry_ref[:, cs]
        y = x * w_ref[SSM_CONV - 1:SSM_CONV, cs] + b_ref[:, cs]
        for k in range(1, SSM_CONV):
            y = y + _shift_rows(x, prev, k) * w_ref[SSM_CONV - 1 - k:SSM_CONV - k, cs]
        o_ref[:, cs] = y * jax.nn.sigmoid(y)
        carry_ref[:, cs] = x[x.shape[0] - SUBLANES:, :]
    tail_ref[0] = carry_ref[...]


def ssm_prep(proj, conv_w, conv_b, bsz, seq):
    tr = SSM_PREP_ROWS
    nt = seq // tr
    return pl.pallas_call(
        _ssm_prep_kernel,
        grid=(bsz, nt),
        in_specs=[pl.BlockSpec((tr, SSM_CONV_DIM), lambda b, i: (b * nt + i, 0)),
                  pl.BlockSpec((SSM_CONV, SSM_CONV_DIM), lambda b, i: (0, 0)),
                  pl.BlockSpec((1, SSM_CONV_DIM), lambda b, i: (0, 0))],
        out_specs=[pl.BlockSpec((tr, SSM_CONV_DIM), lambda b, i: (b * nt + i, 0)),
                   pl.BlockSpec((1, SUBLANES, SSM_CONV_DIM), lambda b, i: (b, 0, 0))],
        out_shape=[jax.ShapeDtypeStruct((bsz * seq, SSM_CONV_DIM), F32),
                   jax.ShapeDtypeStruct((bsz, SUBLANES, SSM_CONV_DIM), F32)],
        scratch_shapes=[pltpu.VMEM((SUBLANES, SSM_CONV_DIM), F32)],
        compiler_params=_cparams(("parallel", "arbitrary")),
        name="ssm_prep",
    )(proj, conv_w, conv_b.reshape(1, -1))


def _taps_kernel(*refs, width, act):
    x_refs, w_ref, b_ref, o_ref = refs[:width], refs[width], refs[width + 1], refs[width + 2]
    y = b_ref[...] + x_refs[0][...] * w_ref[0:1, :]
    for k in range(1, width):
        y = y + x_refs[k][...] * w_ref[k:k + 1, :]
    if act == "silu":
        o_ref[...] = y * jax.nn.sigmoid(y)
    else:
        half = y.shape[1] // 2
        g = y[:, :half]
        o_ref[...] = (g * jax.nn.sigmoid(g) * y[:, half:]).astype(o_ref.dtype)


def conv_taps(views, w, b, *, act, tile, out_dtype):
    width = len(views)
    m, n = views[0].shape
    out_tile = tile if act == "silu" else tile // 2
    col = pl.BlockSpec((m, tile), lambda j: (0, j))
    return pl.pallas_call(
        functools.partial(_taps_kernel, width=width, act=act),
        grid=(n // tile,),
        in_specs=[col] * width + [pl.BlockSpec((width, tile), lambda j: (0, j)),
                                  pl.BlockSpec((1, tile), lambda j: (0, j))],
        out_specs=pl.BlockSpec((m, out_tile), lambda j: (0, j)),
        out_shape=jax.ShapeDtypeStruct((m, n // tile * out_tile), out_dtype),
        compiler_params=_cparams(("parallel",)),
        name="conv_taps_" + act,
    )(*views, w, b.reshape(1, -1))


def _softplus_kernel(x_ref, b_ref, o_ref):
    o_ref[...] = jax.nn.softplus(x_ref[...] + b_ref[...])


def softplus_bias(x, b):
    m, n = x.shape
    return pl.pallas_call(
        _softplus_kernel,
        out_shape=jax.ShapeDtypeStruct((m, n), F32),
        name="dt_softplus",
    )(x, b.reshape(1, n))


def _ssd_kernel(xs_ref, bm_ref, cm_ref, z_ref, dt_ref, dtt_ref, al_ref, alt_ref, dsk_ref, ng_ref, h0_ref,
                y_ref, ht_ref, st_ref, *, n_chunks):
    c = pl.program_id(2)
    q = SSM_CHUNK
    e_heads = SSM_GROUP_HEADS

    @pl.when(c == 0)
    def _():
        st_ref[...] = h0_ref[0]

    ri = _iota((q, q), 0)
    ci = _iota((q, q), 1)
    tril = ri >= ci
    tril_b = tril.astype(BF16)
    triu_b = (ri <= ci).astype(BF16)
    expand = (_iota((e_heads, SSM_GROUP_WIDTH), 0) == _iota((e_heads, SSM_GROUP_WIDTH), 1) // SSM_HEAD_DIM).astype(BF16)

    dt = dt_ref[0]
    da = dt * (-jnp.exp(al_ref[0]))
    da_t = dtt_ref[0] * (-jnp.exp(alt_ref[0]))
    acs = _dot3_r(tril_b, da)
    acs_t = _dot3_l(da_t, triu_b)
    tot = acs[q - 1:q, :]

    xs = xs_ref[...]
    xdt = xs * _dot3_l(dt, expand)
    xdt_b = xdt.astype(BF16)
    bm = bm_ref[...]
    cm_b = cm_ref[...].astype(BF16)
    cb = _dot_nt(cm_b, bm.astype(BF16))

    lane = _iota((q, 2 * SSM_HEAD_DIM), 1)
    y_parts = []
    for pair in range(e_heads // 2):
        cols = slice(pair * 2 * SSM_HEAD_DIM, (pair + 1) * 2 * SSM_HEAD_DIM)
        res = []
        for e in (2 * pair, 2 * pair + 1):
            seg = jnp.where(tril, acs[:, e:e + 1] - acs_t[e:e + 1, :], NEG)
            mix = (cb * jnp.exp(seg)).astype(BF16)
            res.append(_dot(mix, xdt_b[:, cols]))
        y_parts.append(jnp.where(lane < SSM_HEAD_DIM, res[0], res[1]))
    y = jnp.concatenate(y_parts, axis=1)

    state = st_ref[...]
    y = y + _dot(cm_b, state.astype(BF16)) * _dot3_l(jnp.exp(acs), expand)
    decay = _dot3_l(jnp.exp(tot - acs), expand)
    contrib = _dot(bm.T.astype(BF16), (xdt * decay).astype(BF16))
    st_ref[...] = state * _dot3_l(jnp.exp(tot), expand) + contrib

    y = y + xs * dsk_ref[...]
    z = z_ref[...]
    y = y * (z * jax.nn.sigmoid(z))
    ms = jnp.mean(y * y, axis=-1, keepdims=True)
    y_ref[...] = (y * lax.rsqrt(ms + EPS) * ng_ref[...]).astype(y_ref.dtype)

    @pl.when(c == n_chunks - 1)
    def _():
        ht_ref[0] = st_ref[...]


def ssd(xbc, zsrc, z_col0, dt, a_log, d_skip, norm_g, h0, bsz, seq):
    nc = seq // SSM_CHUNK
    gw = SSM_GROUP_WIDTH
    e = SSM_GROUP_HEADS
    m = bsz * seq
    dt_g = dt.reshape(m, SSM_GROUPS, e).transpose(1, 0, 2)
    dt_t = dt_g.transpose(0, 2, 1)
    al = a_log.astype(F32).reshape(SSM_GROUPS, 1, e)
    al_t = a_log.astype(F32).reshape(SSM_GROUPS, e, 1)
    dsk = jnp.repeat(d_skip.astype(F32), SSM_HEAD_DIM).reshape(1, SSM_INNER)
    b_off = SSM_INNER // SSM_STATE
    c_off = b_off + SSM_GROUPS
    rowmap = lambda col: (lambda b, g, c: (b * nc + c, col(g)))
    return pl.pallas_call(
        functools.partial(_ssd_kernel, n_chunks=nc),
        grid=(bsz, SSM_GROUPS, nc),
        in_specs=[
            pl.BlockSpec((SSM_CHUNK, gw), rowmap(lambda g: g)),
            pl.BlockSpec((SSM_CHUNK, SSM_STATE), rowmap(lambda g: b_off + g)),
            pl.BlockSpec((SSM_CHUNK, SSM_STATE), rowmap(lambda g: c_off + g)),
            pl.BlockSpec((SSM_CHUNK, gw), rowmap(lambda g: z_col0 + g)),
            pl.BlockSpec((1, SSM_CHUNK, e), lambda b, g, c: (g, b * nc + c, 0)),
            pl.BlockSpec((1, e, SSM_CHUNK), lambda b, g, c: (g, 0, b * nc + c)),
            pl.BlockSpec((1, 1, e), lambda b, g, c: (g, 0, 0)),
            pl.BlockSpec((1, e, 1), lambda b, g, c: (g, 0, 0)),
            pl.BlockSpec((1, gw), lambda b, g, c: (0, g)),
            pl.BlockSpec((1, gw), lambda b, g, c: (0, g)),
            pl.BlockSpec((1, SSM_STATE, gw), lambda b, g, c: (b, 0, g)),
        ],
        out_specs=[pl.BlockSpec((SSM_CHUNK, gw), rowmap(lambda g: g)),
                   pl.BlockSpec((1, SSM_STATE, gw), lambda b, g, c: (b, 0, g))],
        out_shape=[jax.ShapeDtypeStruct((m, SSM_INNER), BF16),
                   jax.ShapeDtypeStruct((bsz, SSM_STATE, SSM_INNER), F32)],
        scratch_shapes=[pltpu.VMEM((SSM_STATE, gw), F32)],
        compiler_params=_cparams(("parallel", "parallel", "arbitrary")),
        name="ssd",
    )(xbc, xbc, xbc, zsrc, dt_g, dt_t, al, al_t, dsk, norm_g.astype(F32).reshape(1, SSM_INNER), h0)


def _state_to_cols(h):
    b = h.shape[0]
    return h.transpose(0, 3, 1, 2).reshape(b, SSM_STATE, SSM_INNER)


def _state_from_cols(s):
    b = s.shape[0]
    return s.reshape(b, SSM_STATE, SSM_HEADS, SSM_HEAD_DIM).transpose(0, 2, 3, 1)


FFN_BM = 1024
FFN_SUB = 256


def _ffn_up_kernel(x_ref, w_ref, cw_ref, cb_ref, act_ref, tail_ref, carry_ref, *, tiles_per_seq):
    @pl.when(pl.program_id(1) % tiles_per_seq == 0)
    def _():
        carry_ref[...] = jnp.zeros_like(carry_ref)

    w = w_ref[...]
    w0 = cw_ref[0:1, :]
    w1 = cw_ref[1:2, :]
    w2 = cw_ref[2:3, :]
    bias = cb_ref[...]
    for r in range(FFN_BM // FFN_SUB):
        rs = slice(r * FFN_SUB, (r + 1) * FFN_SUB)
        hu = _dot(x_ref[rs, :], w)
        prev = carry_ref[...]
        hc = _shift_rows(hu, prev, 2) * w0 + _shift_rows(hu, prev, 1) * w1 + hu * w2 + bias
        g = hc[:, :FFN_HALF]
        act_ref[rs, :] = (g * jax.nn.sigmoid(g) * hc[:, FFN_HALF:]).astype(act_ref.dtype)
        carry_ref[...] = hu[FFN_SUB - SUBLANES:, :]
    tail_ref[0] = carry_ref[...]


def ffn_up(xn, w_up, conv_w, conv_b, bsz, seq):
    m, k = xn.shape
    tiles_per_seq = seq // FFN_BM
    return pl.pallas_call(
        functools.partial(_ffn_up_kernel, tiles_per_seq=tiles_per_seq),
        grid=(FFN_TILES, m // FFN_BM),
        in_specs=[pl.BlockSpec((FFN_BM, k), lambda j, i: (i, 0)),
                  pl.BlockSpec((k, FFN_TILE), lambda j, i: (0, j)),
                  pl.BlockSpec((FFN_CONV, FFN_TILE), lambda j, i: (0, j)),
                  pl.BlockSpec((1, FFN_TILE), lambda j, i: (0, j))],
        out_specs=[pl.BlockSpec((FFN_BM, FFN_HALF), lambda j, i: (i, j)),
                   pl.BlockSpec((1, SUBLANES, FFN_TILE), lambda j, i: (i // tiles_per_seq, 0, j))],
        out_shape=[jax.ShapeDtypeStruct((m, D_FF), BF16),
                   jax.ShapeDtypeStruct((bsz, SUBLANES, 2 * D_FF), F32)],
        scratch_shapes=[pltpu.VMEM((SUBLANES, FFN_TILE), F32)],
        compiler_params=_cparams(("parallel", "arbitrary")),
        name="ffn_up",
    )(xn, w_up, conv_w, conv_b.reshape(1, -1))


def _ffn_perm(a):
    lead = a.shape[:-1]
    return a.reshape(*lead, 2, FFN_TILES, FFN_HALF).swapaxes(-3, -2).reshape(*lead, 2 * D_FF)


def _ffn_unperm(a):
    lead = a.shape[:-1]
    return a.reshape(*lead, FFN_TILES, 2, FFN_HALF).swapaxes(-3, -2).reshape(*lead, 2 * D_FF)


def _rope_tables(pos):
    half = HEAD_DIM // 2
    inv = ROPE_THETA ** (-jnp.arange(half, dtype=F32) / half)
    ang = pos.astype(F32)[:, None] * inv[None, :]
    cos, sin = jnp.cos(ang), jnp.sin(ang)
    return jnp.concatenate([cos, cos], axis=1), jnp.concatenate([-sin, sin], axis=1)


def _even_weights(w_in):
    pad = lambda a: jnp.pad(a, ((0, 0), (0, LANES - a.shape[1])))
    per = NSA_GROUP * 3
    parts = [w_in[:, :GATE_OFF]]
    for h in range(KV_HEADS):
        parts.append(pad(w_in[:, GATE_OFF + h * per:GATE_OFF + (h + 1) * per]))
    parts.append(jnp.zeros((D_MODEL, E_PAD - GATE_OFF - KV_HEADS * LANES), w_in.dtype))
    return jnp.concatenate(parts, axis=1).astype(BF16)


def _kv_out(rows, bsz):
    return rows.reshape(bsz, -1, 2, KV_HEADS, HEAD_DIM)


def kernel(x_prompt, x_sample, cache_kv_cmp, cache_kv_sel, cache_kv_win, state_ssm_conv, state_ssm, state_ffn_conv, page_table, norm_mix, norm_ffn, w_in_even, w_out_even, gmlp_v_norm, gmlp_ws, gmlp_bs, q_norm, k_norm, cmp_pool, w_in_odd, ssm_conv_w, ssm_conv_b, ssm_dt_bias, ssm_a_log, ssm_d, ssm_norm, w_out_odd, ffn_w_up, ffn_conv_w, ffn_conv_b, ffn_w_down):
    bp, sp, _ = x_prompt.shape
    bs, ss, _ = x_sample.shape
    depth = norm_mix.shape[0]
    n_pool = cache_kv_cmp.shape[1]
    n_pages = page_table.shape[1]
    past_len = n_pages * PAGE_SIZE
    wb = cache_kv_win.shape[2]
    mp, ms = bp * sp, bs * ss

    cos_p, sin_p = _rope_tables(jnp.arange(sp, dtype=jnp.int32))
    cos_p, sin_p = jnp.tile(cos_p, (bp, 1)), jnp.tile(sin_p, (bp, 1))
    cos_s, sin_s = _rope_tables(past_len + jnp.arange(ss, dtype=jnp.int32))
    cos_s, sin_s = jnp.tile(cos_s, (bs, 1)), jnp.tile(sin_s, (bs, 1))

    cache_c = cache_kv_cmp.reshape(-1, PAGE_SIZE, KV_COLS)
    cache_s = cache_kv_sel.reshape(-1, CMP_BLOCK, KV_COLS)
    cache_w = cache_kv_win.reshape(-1, wb, KV_COLS)

    hp = x_prompt.reshape(mp, D_MODEL)
    hs = x_sample.reshape(ms, D_MODEL)
    outs = {k: [] for k in ("p_cmp", "p_sel", "p_win", "s_cmp", "s_sel", "s_win", "s_v",
                            "p_sconv", "p_sst", "s_sconv", "s_sst", "p_fconv", "s_fconv")}

    for layer in range(depth):
        i = layer // 2
        xp = rms_cast(hp, norm_mix[layer])
        xs = rms_cast(hs, norm_mix[layer])
        if layer % 2 == 0:
            w_in = _even_weights(w_in_even[i])
            w_out = w_out_even[i].astype(BF16)
            proj = matmul(xp, w_in, bn=1024, name="even_in")
            u, v, qn, qr, rc, rs, rw, s16, w16, gate = even_prep(proj, cos_p, sin_p, gmlp_v_norm[i], q_norm[i], k_norm[i])
            a_out = gmlp(u, v, gmlp_ws[i], gmlp_bs[i])
            kcmp = compress(rc, cmp_pool[i])
            b_out = nsa_prompt(qn, qr, kcmp, s16, w16, gate, bp, sp)
            hp = matmul(jnp.concatenate([a_out, b_out], axis=1), w_out, bn=1024, res=hp, name="even_out")
            outs["p_cmp"].append(_kv_out(rc, bp))
            outs["p_sel"].append(_kv_out(rs, bp))
            outs["p_win"].append(_kv_out(rw, bp)[:, sp - min(WINDOW, sp):])
            proj = matmul(xs, w_in, bn=1024, name="even_in_s")
            u, v, qn, qr, rc, rs, rw, _, _, gate = even_prep(proj, cos_s, sin_s, gmlp_v_norm[i], q_norm[i], k_norm[i])
            lpad = ((0, 0), (0, A_CHUNK - ss), (0, 0))
            a_out = gmlp(jnp.pad(u.reshape(bs, ss, -1), lpad).reshape(bs * A_CHUNK, -1),
                         jnp.pad(v.reshape(bs, ss, -1), lpad).reshape(bs * A_CHUNK, -1),
                         gmlp_ws[i], gmlp_bs[i]).reshape(bs, A_CHUNK, -1)[:, :ss].reshape(ms, -1)
            kc_past = pool_past(cache_c, i * n_pool, page_table, cmp_pool[i])
            o_cmp, idx = sample_select(qn.reshape(bs, ss, -1), kc_past)
            b_out = sample_attn(idx, page_table, cache_s, i * n_pool, cache_w, i * bs,
                                qr.reshape(bs, ss, -1), rs.reshape(bs, ss, -1), rw.reshape(bs, ss, -1),
                                gate.reshape(bs, ss, -1), o_cmp).reshape(ms, B_WIDTH)
            hs = matmul(jnp.concatenate([a_out, b_out], axis=1), w_out, bn=1024, res=hs, name="even_out_s")
            outs["s_cmp"].append(_kv_out(rc, bs))
            outs["s_sel"].append(_kv_out(rs, bs))
            outs["s_win"].append(_kv_out(rw, bs))
            outs["s_v"].append(v.reshape(bs, ss, A_WIDTH))
        else:
            w_in = w_in_odd[i]
            w_zx = jnp.concatenate([w_in[:, SSM_INNER:SSM_INNER + SSM_CONV_DIM], w_in[:, :SSM_INNER]], axis=1).astype(BF16)
            w_dt = jnp.pad(w_in[:, SSM_INNER + SSM_CONV_DIM:], ((0, 0), (0, LANES - SSM_HEADS))).astype(BF16)
            dt_b = jnp.pad(ssm_dt_bias[i].astype(F32), (0, LANES - SSM_HEADS))
            w_out = w_out_odd[i].astype(BF16)
            z_col0 = SSM_CONV_DIM // SSM_GROUP_WIDTH
            proj = matmul(xp, w_zx, bn=1024, name="odd_in")
            dt = softplus_bias(matmul(xp, w_dt, bn=LANES, name="odd_dt"), dt_b)[:, :SSM_HEADS]
            xbc, tail = ssm_prep(proj, ssm_conv_w[i], ssm_conv_b[i], bp, sp)
            h0 = jnp.zeros((bp, SSM_STATE, SSM_INNER), F32)
            y, ht = ssd(xbc, proj, z_col0, dt, ssm_a_log[i], ssm_d[i], ssm_norm[i], h0, bp, sp)
            hp = matmul(y, w_out, bn=512, res=hp, name="odd_out")
            outs["p_sconv"].append(tail[:, SUBLANES - (SSM_CONV - 1):])
            outs["p_sst"].append(_state_from_cols(ht))
            proj = matmul(xs, w_zx, bn=1024, name="odd_in_s")
            dt = softplus_bias(matmul(xs, w_dt, bn=LANES, name="odd_dt_s"), dt_b)[:, :SSM_HEADS]
            xin = jnp.concatenate([state_ssm_conv[i], proj[:, :SSM_CONV_DIM].reshape(bs, ss, -1)], axis=1)
            views = [xin[:, k:k + ss].reshape(ms, -1) for k in range(SSM_CONV)]
            xbc = conv_taps(views, ssm_conv_w[i], ssm_conv_b[i], act="silu", tile=1024, out_dtype=F32)
            cpad = ((0, 0), (0, SSM_CHUNK - ss), (0, 0))
            padrows = lambda a: jnp.pad(a.reshape(bs, ss, -1), cpad).reshape(bs * SSM_CHUNK, -1)
            y, ht = ssd(padrows(xbc), padrows(proj[:, SSM_CONV_DIM:]), 0, padrows(dt), ssm_a_log[i], ssm_d[i], ssm_norm[i],
                        _state_to_cols(state_ssm[i].astype(F32)), bs, SSM_CHUNK)
            y = y.reshape(bs, SSM_CHUNK, -1)[:, :ss].reshape(ms, -1)
            hs = matmul(y, w_out, bn=1024, res=hs, name="odd_out_s")
            outs["s_sconv"].append(xin[:, ss:])
            outs["s_sst"].append(_state_from_cols(ht))
        w_up = _ffn_perm(ffn_w_up[layer]).astype(BF16)
        w_down = ffn_w_down[layer].astype(BF16)
        cw = _ffn_perm(ffn_conv_w[layer])
        cb = _ffn_perm(ffn_conv_b[layer])
        xp = rms_cast(hp, norm_ffn[layer])
        act, tail = ffn_up(xp, w_up, cw, cb, bp, sp)
        hp = matmul(act, w_down, bn=512, res=hp, name="ffn_down")
        outs["p_fconv"].append(_ffn_unperm(tail[:, SUBLANES - (FFN_CONV - 1):]))
        xs = rms_cast(hs, norm_ffn[layer])
        hu = matmul(xs, w_up, bn=1024, name="ffn_up_s")
        xin = jnp.concatenate([_ffn_perm(state_ffn_conv[layer]), hu.reshape(bs, ss, -1)], axis=1)
        views = [xin[:, k:k + ss].reshape(ms, -1) for k in range(FFN_CONV)]
        act = conv_taps(views, cw, cb, act="glu", tile=FFN_TILE, out_dtype=BF16)
        hs = matmul(act, w_down, bn=512, res=hs, name="ffn_down_s")
        outs["s_fconv"].append(_ffn_unperm(xin[:, ss:]))

    st = lambda k: jnp.stack(outs[k])
    return (hp.reshape(bp, sp, D_MODEL), hs.reshape(bs, ss, D_MODEL), st("p_cmp"), st("p_sel"), st("p_win"),
            st("p_sconv"), st("p_sst"), st("p_fconv"), st("s_cmp"), st("s_sel"), st("s_win"), st("s_v"),
            st("s_sconv"), st("s_sst"), st("s_fconv"))
```

```python
import functools

import jax
import jax.numpy as jnp
from jax import lax
from jax.experimental import pallas as pl
from jax.experimental.pallas import tpu as pltpu

F32 = jnp.float32
BF16 = jnp.bfloat16

D_MODEL = 2048
A_WIDTH = D_MODEL // 2
A_GROUPS = 8
A_CHUNK = 128
NSA_HEADS = 8
HEAD_DIM = 128
KV_HEADS = 2
NSA_GROUP = NSA_HEADS // KV_HEADS
B_WIDTH = NSA_HEADS * HEAD_DIM
KV_COLS = 2 * KV_HEADS * HEAD_DIM
KV_SLOTS = 2 * KV_HEADS
CMP_BLOCK = 64
SEL_TOPN = 16
WINDOW = 512
PAGE_SIZE = 128
ROPE_THETA = 10000.0
ATTN_SCALE = HEAD_DIM ** -0.5
LOG2E = 1.4426950408889634
SSM_INNER = 2 * D_MODEL
SSM_HEAD_DIM = 64
SSM_HEADS = SSM_INNER // SSM_HEAD_DIM
SSM_GROUPS = 8
SSM_GROUP_HEADS = SSM_HEADS // SSM_GROUPS
SSM_GROUP_WIDTH = SSM_INNER // SSM_GROUPS
SSM_STATE = 128
SSM_CONV = 4
SSM_CHUNK = 128
SSM_CONV_DIM = SSM_INNER + 2 * SSM_GROUPS * SSM_STATE
D_FF = 5632
FFN_CONV = 3
EPS = 1e-6
NEG = -1e30
FORCE = 1e4
TINY = 1e-30

VMEM_LIMIT_BYTES = 56 * 1024 * 1024
LANES = 128
SUBLANES = 8

E_MAIN = 2 * A_WIDTH + B_WIDTH + 3 * KV_COLS
N_GATES = 3 * NSA_HEADS
GATES_PER_KV = 3 * NSA_GROUP


def _cparams(sem):
    return pltpu.CompilerParams(dimension_semantics=sem, vmem_limit_bytes=VMEM_LIMIT_BYTES)


def _dot(a, b):
    return jnp.dot(a, b, preferred_element_type=F32)


def _dot_nt(a, b):
    return lax.dot_general(a, b, (((1,), (1,)), ((), ())), preferred_element_type=F32)


def _iota(shape, dim):
    return lax.broadcasted_iota(jnp.int32, shape, dim)


def _split3(x):
    hi = x.astype(BF16)
    r1 = x - hi.astype(F32)
    mid = r1.astype(BF16)
    lo = (r1 - mid.astype(F32)).astype(BF16)
    return hi, mid, lo


def _dot3_l(x, m):
    hi, mid, lo = _split3(x)
    return _dot(hi, m) + _dot(mid, m) + _dot(lo, m)


def _dot3_r(m, x):
    hi, mid, lo = _split3(x)
    return _dot(m, hi) + _dot(m, mid) + _dot(m, lo)


def _dot2_l(x, m):
    hi = x.astype(BF16)
    return _dot(hi, m) + _dot((x - hi.astype(F32)).astype(BF16), m)


def _rms_kernel(x_ref, g_ref, o_ref):
    x = x_ref[...]
    ms = jnp.mean(x * x, axis=-1, keepdims=True)
    o_ref[...] = (x * lax.rsqrt(ms + EPS) * g_ref[...]).astype(o_ref.dtype)


def rms_cast(x, g):
    m, d = x.shape
    tr = min(m, 512)
    return pl.pallas_call(
        _rms_kernel,
        grid=(m // tr,),
        in_specs=[pl.BlockSpec((tr, d), lambda i: (i, 0)), pl.BlockSpec((1, d), lambda i: (0, 0))],
        out_specs=pl.BlockSpec((tr, d), lambda i: (i, 0)),
        out_shape=jax.ShapeDtypeStruct((m, d), BF16),
        compiler_params=_cparams(("parallel",)),
        name="rms_cast",
    )(x, g.reshape(1, d))


def _mm_kernel(x_ref, w_ref, o_ref):
    o_ref[...] = _dot(x_ref[...], w_ref[0]).astype(o_ref.dtype)


def _mm_res_kernel(x_ref, w_ref, r_ref, o_ref):
    o_ref[...] = (r_ref[...] + _dot(x_ref[...], w_ref[0])).astype(o_ref.dtype)


def matmul(x, w, *, bn, layer=0, res=None, out_dtype=F32, name="mm"):
    m, k = x.shape
    if w.ndim == 2:
        w = w[None]
    n = w.shape[2]
    bm = min(m, 1024)
    in_specs = [pl.BlockSpec((bm, k), lambda i, j: (i, 0)), pl.BlockSpec((1, k, bn), lambda i, j: (layer, 0, j))]
    args = [x, w]
    kern = _mm_kernel
    if res is not None:
        in_specs.append(pl.BlockSpec((bm, bn), lambda i, j: (i, j)))
        args.append(res)
        kern = _mm_res_kernel
    return pl.pallas_call(
        kern,
        grid=(m // bm, n // bn),
        in_specs=in_specs,
        out_specs=pl.BlockSpec((bm, bn), lambda i, j: (i, j)),
        out_shape=jax.ShapeDtypeStruct((m, n), out_dtype),
        compiler_params=_cparams(("parallel", "parallel")),
        name=name,
    )(*args)


WS_BM = 1024


def _mm_ws_kernel(*refs, n_parts, has_res, w_t):
    xp, xs, w_ref = refs[:n_parts], refs[n_parts:2 * n_parts], refs[2 * n_parts]
    pos = 2 * n_parts + 1
    if has_res:
        r_ref, rs_ref = refs[pos], refs[pos + 1]
        pos += 2
    o_ref, os_ref, wb_ref = refs[pos], refs[pos + 1], refs[pos + 2]

    def mm(parts):
        acc, k0 = None, 0
        for p in parts:
            kp = p.shape[1]
            t = _dot(p[...], wb_ref[k0:k0 + kp, :])
            acc = t if acc is None else acc + t
            k0 += kp
        return acc

    @pl.when(pl.program_id(1) == 0)
    def _():
        wb_ref[...] = (w_ref[0].T if w_t else w_ref[0]).astype(BF16)
        ys = mm(xs)
        if has_res:
            ys = rs_ref[...] + ys
        os_ref[...] = ys.astype(os_ref.dtype)

    y = mm(xp)
    if has_res:
        y = r_ref[...] + y
    o_ref[...] = y.astype(o_ref.dtype)


def matmul_ws(xp_parts, xs_parts, w, layer, n_cols, *, bn, w_t=False, res=None, name="mm_ws"):
    mp = xp_parts[0].shape[0]
    ms = xs_parts[0].shape[0]
    k = w.shape[2] if w_t else w.shape[1]
    n_parts = len(xp_parts)
    in_specs = [pl.BlockSpec((WS_BM, p.shape[1]), lambda j, i: (i, 0)) for p in xp_parts]
    in_specs += [pl.BlockSpec((ms, p.shape[1]), lambda j, i: (0, 0)) for p in xs_parts]
    in_specs.append(pl.BlockSpec((1, bn, k), lambda j, i: (layer, j, 0)) if w_t
                    else pl.BlockSpec((1, k, bn), lambda j, i: (layer, 0, j)))
    args = list(xp_parts) + list(xs_parts) + [w]
    if res is not None:
        in_specs += [pl.BlockSpec((WS_BM, bn), lambda j, i: (i, j)), pl.BlockSpec((ms, bn), lambda j, i: (0, j))]
        args += list(res)
    return pl.pallas_call(
        functools.partial(_mm_ws_kernel, n_parts=n_parts, has_res=res is not None, w_t=w_t),
        grid=(n_cols // bn, mp // WS_BM),
        in_specs=in_specs,
        out_specs=[pl.BlockSpec((WS_BM, bn), lambda j, i: (i, j)), pl.BlockSpec((ms, bn), lambda j, i: (0, j))],
        out_shape=[jax.ShapeDtypeStruct((mp, n_cols), F32), jax.ShapeDtypeStruct((ms, n_cols), F32)],
        scratch_shapes=[pltpu.VMEM((k, bn), BF16)],
        compiler_params=_cparams(("parallel", "arbitrary")),
        name=name,
    )(*args)


def _mm_narrow_kernel(x_ref, wt_ref, b_ref, o_ref, *, softplus):
    y = _dot_nt(x_ref[...], wt_ref[...].astype(BF16))
    if softplus:
        y = jax.nn.softplus(y + b_ref[...])
    o_ref[...] = y


def matmul_narrow(x, wt, bias=None):
    m, k = x.shape
    n = wt.shape[0]
    bm = min(m, 1024)
    b = jnp.zeros((LANES,), F32) if bias is None else jnp.pad(bias.astype(F32), (0, LANES - n))
    return pl.pallas_call(
        functools.partial(_mm_narrow_kernel, softplus=bias is not None),
        grid=(m // bm,),
        in_specs=[pl.BlockSpec((bm, k), lambda i: (i, 0)), pl.BlockSpec((LANES, k), lambda i: (0, 0)),
                  pl.BlockSpec((1, LANES), lambda i: (0, 0))],
        out_specs=pl.BlockSpec((bm, LANES), lambda i: (i, 0)),
        out_shape=jax.ShapeDtypeStruct((m, LANES), F32),
        compiler_params=_cparams(("parallel",)),
        name="mm_narrow",
    )(x, jnp.pad(wt, ((0, LANES - n), (0, 0))), b.reshape(1, LANES))


def _head_rms(x, gain):
    ms = jnp.mean(x * x, axis=-1, keepdims=True)
    return x * lax.rsqrt(ms + EPS) * gain


def _rope(x, cosf, sinf):
    return x * cosf + pltpu.roll(x, HEAD_DIM // 2, axis=1) * sinf


def _even_prep_kernel(p_ref, gl_ref, cos_ref, sin_ref, vg_ref, qn_ref, kn_ref,
                      u_ref, v_ref, q_ref, qr_ref, rc_ref, rs_ref, rw_ref, gate_ref, *mxu_refs):
    cosf = cos_ref[...]
    sinf = sin_ref[...]
    u_ref[...] = jax.nn.gelu(p_ref[:, 0:A_WIDTH]).astype(u_ref.dtype)
    for g in range(A_GROUPS):
        sl = slice(g * LANES, (g + 1) * LANES)
        vg = jax.nn.gelu(p_ref[:, A_WIDTH + g * LANES:A_WIDTH + (g + 1) * LANES])
        v_ref[:, sl] = _head_rms(vg, vg_ref[:, sl])
    qgain = qn_ref[...]
    for h in range(NSA_HEADS):
        sl = slice(h * LANES, (h + 1) * LANES)
        q = _head_rms(p_ref[:, 2 * A_WIDTH + h * LANES:2 * A_WIDTH + (h + 1) * LANES], qgain)
        q_ref[:, sl] = q.astype(q_ref.dtype)
        qr_ref[:, sl] = _rope(q, cosf, sinf).astype(qr_ref.dtype)
    base = 2 * A_WIDTH + B_WIDTH
    for which, row_ref in enumerate((rc_ref, rs_ref, rw_ref)):
        gain = kn_ref[which:which + 1, :]
        off = base + which * KV_COLS
        k16_ref, vt_ref = (mxu_refs[which - 1], mxu_refs[which + 1]) if (mxu_refs and which > 0) else (None, None)
        for h in range(KV_HEADS):
            sl = slice(h * LANES, (h + 1) * LANES)
            k = _head_rms(p_ref[:, off + h * LANES:off + (h + 1) * LANES], gain)
            if which > 0:
                k = _rope(k, cosf, sinf)
            row_ref[:, sl] = k
            vals = p_ref[:, off + (KV_HEADS + h) * LANES:off + (KV_HEADS + h + 1) * LANES]
            row_ref[:, (KV_HEADS + h) * LANES:(KV_HEADS + h + 1) * LANES] = vals
            if k16_ref is not None:
                k16_ref[:, sl] = k.astype(BF16)
                vt_ref[0, h] = vals.T.astype(BF16)
    gate_ref[...] = jax.nn.sigmoid(gl_ref[...])


def even_prep(proj, gl, cosf, sinf, v_gain, q_gain, k_gain, *, for_mxu):
    m = proj.shape[0]
    tr = min(m, SEL_KC)
    row = lambda w: pl.BlockSpec((tr, w), lambda i: (i, 0))
    full = lambda r, w: pl.BlockSpec((r, w), lambda i: (0, 0))
    out_shapes = [
        jax.ShapeDtypeStruct((m, A_WIDTH), BF16),
        jax.ShapeDtypeStruct((m, A_WIDTH), F32),
        jax.ShapeDtypeStruct((m, B_WIDTH), BF16),
        jax.ShapeDtypeStruct((m, B_WIDTH), BF16),
        jax.ShapeDtypeStruct((m, KV_COLS), F32),
        jax.ShapeDtypeStruct((m, KV_COLS), F32),
        jax.ShapeDtypeStruct((m, KV_COLS), F32),
        jax.ShapeDtypeStruct((m, LANES), F32),
    ]
    out_specs = [row(s.shape[1]) for s in out_shapes]
    if for_mxu:
        assert tr == SEL_KC
        out_shapes += [jax.ShapeDtypeStruct((m, KV_HEADS * LANES), BF16)] * 2
        out_specs += [row(KV_HEADS * LANES)] * 2
        out_shapes += [jax.ShapeDtypeStruct((m // tr, KV_HEADS, HEAD_DIM, tr), BF16)] * 2
        out_specs += [pl.BlockSpec((1, KV_HEADS, HEAD_DIM, tr), lambda i: (i, 0, 0, 0))] * 2
    return pl.pallas_call(
        _even_prep_kernel,
        grid=(m // tr,),
        in_specs=[row(E_MAIN), row(LANES), row(LANES), row(LANES),
                  full(1, A_WIDTH), full(1, LANES), full(3, LANES)],
        out_specs=out_specs,
        out_shape=out_shapes,
        compiler_params=_cparams(("parallel",)),
        name="even_prep",
    )(proj, gl, cosf, sinf, v_gain.reshape(1, A_WIDTH), q_gain.reshape(1, LANES), k_gain)


def _gmlp_kernel(u_ref, v_ref, ws_ref, bst_ref, o_ref):
    tril = _iota((A_CHUNK, A_CHUNK), 0) >= _iota((A_CHUNK, A_CHUNK), 1)
    for g in range(A_GROUPS):
        sl = slice(g * LANES, (g + 1) * LANES)
        wm = jnp.where(tril, ws_ref[g], 0.0).astype(BF16)
        s = _dot(wm, v_ref[:, sl].astype(BF16)) + bst_ref[:, g:g + 1]
        o_ref[:, sl] = (u_ref[:, sl].astype(F32) * s).astype(o_ref.dtype)


def gmlp(u, v, ws, bs):
    m = u.shape[0]
    row = pl.BlockSpec((A_CHUNK, A_WIDTH), lambda i: (i, 0))
    return pl.pallas_call(
        _gmlp_kernel,
        grid=(m // A_CHUNK,),
        in_specs=[row, row,
                  pl.BlockSpec((A_GROUPS, A_CHUNK, A_CHUNK), lambda i: (0, 0, 0)),
                  pl.BlockSpec((A_CHUNK, A_GROUPS), lambda i: (0, 0))],
        out_specs=row,
        out_shape=jax.ShapeDtypeStruct((m, A_WIDTH), BF16),
        compiler_params=_cparams(("parallel",)),
        name="gmlp",
    )(u, v, ws, bs.T)


def _compress_kernel(x_ref, pool_ref, o_ref):
    pool = pool_ref[...]
    e = jnp.exp(pool - jnp.max(pool, axis=0, keepdims=True))
    w = e / jnp.sum(e, axis=0, keepdims=True)
    x = x_ref[...]
    nb = x.shape[0] // CMP_BLOCK
    o_ref[...] = jnp.sum(x.reshape(nb, CMP_BLOCK, KV_COLS) * w[None], axis=1)


def compress(rows, pool):
    m = rows.shape[0]
    tr = min(m, 1024)
    pool_cols = jnp.concatenate([jnp.repeat(pool.T, LANES, axis=1)] * 2, axis=1)
    return pl.pallas_call(
        _compress_kernel,
        grid=(m // tr,),
        in_specs=[pl.BlockSpec((tr, KV_COLS), lambda i: (i, 0)),
                  pl.BlockSpec((CMP_BLOCK, KV_COLS), lambda i: (0, 0))],
        out_specs=pl.BlockSpec((tr // CMP_BLOCK, KV_COLS), lambda i: (i, 0)),
        out_shape=jax.ShapeDtypeStruct((m // CMP_BLOCK, KV_COLS), F32),
        compiler_params=_cparams(("parallel",)),
        name="compress",
    )(rows, pool_cols)


NSA_TQ = 128
SEL_KC = 512
EXP2_SCALE = ATTN_SCALE * LOG2E


def _kv_gates(gate_ref, h):
    g = gate_ref[...]
    return jnp.where(h == 0, g[:, 0:GATES_PER_KV], g[:, GATES_PER_KV:2 * GATES_PER_KV])


def _pad_rows(x, rows):
    return jnp.concatenate([x, jnp.zeros((rows - x.shape[0], x.shape[1]), x.dtype)], axis=0)


def _nsa_prompt_kernel(qn_ref, qr_ref, kc_ref, vc_ref, ks_ref, vts_ref, kw_ref, vtw_ref, gate_ref, o_ref,
                       acc_ref, osel_ref, *, seq):
    h = pl.program_id(1)
    t = pl.program_id(2)
    nb = seq // CMP_BLOCK
    tq = NSA_TQ
    qpos_t = t * tq + _iota((1, tq), 1)
    blk_t = _iota((nb, 1), 0)

    heads = lambda ref: jnp.concatenate([ref[:, g * LANES:(g + 1) * LANES] for g in range(NSA_GROUP)], axis=0)
    wide = lambda x: jnp.concatenate([x] * NSA_GROUP, axis=1)
    qn4 = heads(qn_ref)
    qr4 = heads(qr_ref)

    kc = kc_ref[0].astype(BF16)
    vc_t = _pad_rows(vc_ref[0], LANES).T.astype(BF16)
    valid_ct = ((blk_t + 1) * CMP_BLOCK - 1) <= t * tq + _iota((1, NSA_GROUP * tq), 1) % tq
    st = jnp.where(valid_ct, _dot_nt(kc, qn4) * ATTN_SCALE, NEG)
    et = jnp.where(valid_ct, jnp.exp(st - jnp.max(st, axis=0, keepdims=True)), 0.0)
    pt = et / jnp.maximum(jnp.sum(et, axis=0, keepdims=True), TINY)
    o_cmp = _dot(vc_t, _pad_rows(pt, LANES).astype(BF16))
    imp_t = pt[:, 0:tq]
    for g in range(1, NSA_GROUP):
        imp_t = imp_t + pt[:, g * tq:(g + 1) * tq]

    cur_t = qpos_t // CMP_BLOCK
    forced = (blk_t == 0) | (blk_t == cur_t)
    score = jnp.where(forced, FORCE, jnp.where(blk_t > cur_t, NEG, imp_t))
    rank = jnp.zeros((nb, tq), jnp.int32)
    for j in range(nb):
        row = score[j:j + 1, :]
        beats = (row > score) | ((row == score) & (blk_t > j))
        rank = rank + beats.astype(jnp.int32)
    sel = _pad_rows((rank < min(SEL_TOPN, nb)).astype(F32), LANES).astype(BF16)

    def attend(lo, hi, k_ref, vt_ref, bias_fn):
        acc_ref[...] = jnp.zeros(acc_ref.shape, F32)

        def body(c, carry):
            m, l = carry
            start = pl.multiple_of(c * SEL_KC, SEL_KC)
            s = _dot_nt(k_ref[pl.ds(start, SEL_KC), :], qr4) + wide(bias_fn(c))
            m_new = jnp.maximum(m, jnp.max(s, axis=0, keepdims=True))
            p = jnp.exp2((s - m_new) * EXP2_SCALE)
            alpha = jnp.exp2((m - m_new) * EXP2_SCALE)
            acc_ref[...] = alpha * acc_ref[...] + _dot(vt_ref[c, 0], p.astype(BF16))
            return m_new, alpha * l + jnp.sum(p, axis=0, keepdims=True)

        init = (jnp.full((1, NSA_GROUP * tq), NEG, F32), jnp.zeros((1, NSA_GROUP * tq), F32))
        return lax.fori_loop(lo, hi, body, init)[1]

    def kpos_col(c):
        return c * SEL_KC + _iota((SEL_KC, 1), 0)

    def sel_bias(c):
        expand = ((c * SEL_KC + _iota((SEL_KC, LANES), 0)) // CMP_BLOCK == _iota((SEL_KC, LANES), 1)).astype(BF16)
        chosen = _dot(expand, sel) > 0.5
        return jnp.where(chosen & (kpos_col(c) <= qpos_t), 0.0, NEG)

    def win_bias(c):
        kpos = kpos_col(c)
        return jnp.where((kpos <= qpos_t) & (kpos >= qpos_t - WINDOW), 0.0, NEG)

    hi = (t * tq + tq - 1) // SEL_KC + 1
    l_sel = attend(0, hi, ks_ref, vts_ref, sel_bias)
    osel_ref[...] = acc_ref[...] / l_sel
    l_win = attend(jnp.maximum(t * tq - WINDOW, 0) // SEL_KC, hi, kw_ref, vtw_ref, win_bias)
    o_win = acc_ref[...] / l_win

    g_all = gate_ref[...]
    gate_t = jnp.where(h == 0, g_all, pltpu.roll(g_all, LANES - GATES_PER_KV, axis=1)).T
    for g in range(NSA_GROUP):
        cs = slice(g * tq, (g + 1) * tq)
        out_t = (gate_t[3 * g:3 * g + 1] * o_cmp[:, cs] + gate_t[3 * g + 1:3 * g + 2] * osel_ref[:, cs]
                 + gate_t[3 * g + 2:3 * g + 3] * o_win[:, cs])
        o_ref[:, g * LANES:(g + 1) * LANES] = out_t.T.astype(o_ref.dtype)


def nsa_prompt(qn, qr, kcmp, ks16, vts16, kw16, vtw16, gate, bsz, seq):
    nt = seq // NSA_TQ
    nb = seq // CMP_BLOCK
    nc = seq // SEL_KC
    assert nb <= LANES and seq % SEL_KC == 0
    qspec = pl.BlockSpec((NSA_TQ, NSA_GROUP * LANES), lambda b, h, t: (b * nt + t, h))
    cmp_k = pl.BlockSpec((1, nb, LANES), lambda b, h, t: (b, 0, h))
    cmp_v = pl.BlockSpec((1, nb, LANES), lambda b, h, t: (b, 0, KV_HEADS + h))
    key = pl.BlockSpec((seq, LANES), lambda b, h, t: (b, h))
    val_t = pl.BlockSpec((nc, 1, HEAD_DIM, SEL_KC), lambda b, h, t: (b, h, 0, 0))
    kc3 = kcmp.reshape(bsz, nb, KV_COLS)
    return pl.pallas_call(
        functools.partial(_nsa_prompt_kernel, seq=seq),
        grid=(bsz, KV_HEADS, nt),
        in_specs=[qspec, qspec, cmp_k, cmp_v, key, val_t, key, val_t,
                  pl.BlockSpec((NSA_TQ, LANES), lambda b, h, t: (b * nt + t, 0))],
        out_specs=qspec,
        out_shape=jax.ShapeDtypeStruct((bsz * seq, B_WIDTH), BF16),
        scratch_shapes=[pltpu.VMEM((HEAD_DIM, NSA_GROUP * NSA_TQ), F32),
                        pltpu.VMEM((HEAD_DIM, NSA_GROUP * NSA_TQ), F32)],
        compiler_params=_cparams(("parallel", "parallel", "arbitrary")),
        name="nsa_prompt",
    )(qn, qr, kc3, kc3, ks16, vts16, kw16, vtw16, gate)


PAGES_PER_STEP = 8
PAGE_ROWS = PAGE_SIZE * KV_SLOTS
BLOCK_ROWS = CMP_BLOCK * KV_SLOTS
BLOCKS_PER_PAGE = PAGE_SIZE // CMP_BLOCK


def _pool_past_kernel(pt_ref, *refs):
    page_refs, pool_ref, o_ref = refs[:PAGES_PER_STEP], refs[PAGES_PER_STEP], refs[PAGES_PER_STEP + 1]
    tiles = BLOCK_ROWS // SUBLANES
    pool = pool_ref[...].reshape(tiles, SUBLANES, LANES)
    fold = lambda a, op: op(a, pltpu.roll(a, KV_SLOTS, axis=0))
    mx = fold(jnp.max(pool, axis=0), jnp.maximum)
    e = jnp.exp(pool - mx[None])
    w = e / fold(jnp.sum(e, axis=0), jnp.add)[None]
    first = _iota((SUBLANES, LANES), 0) < KV_SLOTS
    for k, ref in enumerate(page_refs):
        x = ref[0].reshape(BLOCKS_PER_PAGE, tiles, SUBLANES, LANES)
        sums = [fold(jnp.sum(x[b] * w, axis=0), jnp.add) for b in range(BLOCKS_PER_PAGE)]
        o_ref[0, k * SUBLANES:(k + 1) * SUBLANES, :] = jnp.where(first, sums[0], sums[1])


def pool_past(cache, layer_off, page_table, pool):
    bsz, n_pages = page_table.shape
    steps = n_pages // PAGES_PER_STEP
    assert BLOCKS_PER_PAGE * KV_SLOTS == SUBLANES
    pool_rows = jnp.broadcast_to(jnp.tile(pool.T, (1, 2)).reshape(BLOCK_ROWS, 1), (BLOCK_ROWS, LANES))

    def page_spec(k):
        return pl.BlockSpec((1, PAGE_ROWS, LANES),
                            lambda b, s, pt: (layer_off + pt[b * n_pages + s * PAGES_PER_STEP + k], 0, 0))

    grid_spec = pltpu.PrefetchScalarGridSpec(
        num_scalar_prefetch=1,
        grid=(bsz, steps),
        in_specs=[page_spec(k) for k in range(PAGES_PER_STEP)]
        + [pl.BlockSpec((BLOCK_ROWS, LANES), lambda b, s, pt: (0, 0))],
        out_specs=pl.BlockSpec((1, PAGES_PER_STEP * SUBLANES, LANES), lambda b, s, pt: (b, s, 0)),
    )
    return pl.pallas_call(
        _pool_past_kernel,
        grid_spec=grid_spec,
        out_shape=jax.ShapeDtypeStruct((bsz, n_pages * SUBLANES, LANES), F32),
        compiler_params=_cparams(("parallel", "arbitrary")),
        name="pool_past",
    )(page_table.reshape(-1), *([cache] * PAGES_PER_STEP), pool_rows)


def _sample_select_kernel(q_ref, kc_ref, oc_ref, idx_ref, *, dec):
    rows = kc_ref.shape[1]
    x = kc_ref[0].astype(BF16)
    lane = _iota((dec, rows), 1)
    slot = _iota((dec, SEL_TOPN), 1)
    for h in range(KV_HEADS):
        is_key = lane % KV_SLOTS == h
        imp = jnp.zeros((dec, rows), F32)
        for g in range(NSA_GROUP):
            hd = h * NSA_GROUP + g
            q = q_ref[0, :, hd * LANES:(hd + 1) * LANES]
            s = jnp.where(is_key, _dot_nt(q, x) * ATTN_SCALE, NEG)
            e = jnp.where(is_key, jnp.exp(s - jnp.max(s, axis=-1, keepdims=True)), 0.0)
            p = e / jnp.maximum(jnp.sum(e, axis=-1, keepdims=True), TINY)
            imp = imp + p
            oc_ref[0, :, hd * LANES:(hd + 1) * LANES] = _dot(pltpu.roll(p, KV_HEADS, axis=1).astype(BF16), x)
        score = jnp.where(is_key & (lane >= KV_SLOTS), imp, -jnp.inf)
        picked = jnp.zeros((dec, SEL_TOPN), jnp.int32)
        for r in range(SEL_TOPN - 2):
            best = jnp.max(score, axis=-1, keepdims=True)
            arg = jnp.min(jnp.where(score == best, lane, rows), axis=-1, keepdims=True)
            picked = jnp.where(slot == r + 1, arg // KV_SLOTS, picked)
            score = jnp.where(lane == arg, -jnp.inf, score)
        idx_ref[0, h * dec:(h + 1) * dec, :] = picked


def sample_select(qn, kcmp_past):
    bsz, dec, _ = qn.shape
    rows = kcmp_past.shape[1]
    return pl.pallas_call(
        functools.partial(_sample_select_kernel, dec=dec),
        grid=(bsz,),
        in_specs=[pl.BlockSpec((1, dec, B_WIDTH), lambda b: (b, 0, 0)),
                  pl.BlockSpec((1, rows, LANES), lambda b: (b, 0, 0))],
        out_specs=[pl.BlockSpec((1, dec, B_WIDTH), lambda b: (b, 0, 0)),
                   pl.BlockSpec((1, KV_HEADS * dec, SEL_TOPN), lambda b: (b, 0, 0))],
        out_shape=[jax.ShapeDtypeStruct((bsz, dec, B_WIDTH), F32),
                   jax.ShapeDtypeStruct((bsz, KV_HEADS * dec, SEL_TOPN), jnp.int32)],
        compiler_params=_cparams(("parallel",)),
        name="sample_select",
    )(qn, kcmp_past)


def _rows_to_tile(rows):
    r = _iota((SUBLANES, rows[0].shape[1]), 0)
    out = jnp.zeros((SUBLANES, rows[0].shape[1]), rows[0].dtype)
    for i, x in enumerate(rows):
        out = jnp.where(r == i, jnp.broadcast_to(x, out.shape), out)
    return out


def _sample_attn_kernel(idx_ref, pt_ref, *refs, dec):
    n = SEL_TOPN
    blk_refs = refs[:n]
    (qr_ref, ksn_ref, vsn_ref, wbuf_ref, kwn_ref, vwn_ref, gate_ref, oc_ref, o_ref) = refs[n:]
    h = pl.program_id(1)
    qi = pl.program_id(2)
    q8 = _rows_to_tile([qr_ref[0, 0, :, g * LANES:(g + 1) * LANES].astype(F32)
                        for g in range(NSA_GROUP)]).astype(BF16)
    newpos = _iota((1, dec), 1)

    def attend(x_old, mask_old, k_new, v_new):
        s_old = jnp.where(mask_old, _dot_nt(q8, x_old) * ATTN_SCALE, NEG)
        mask_new = newpos <= qi
        s_new = jnp.where(mask_new, _dot_nt(q8, k_new) * ATTN_SCALE, NEG)
        m = jnp.maximum(jnp.max(s_old, axis=-1, keepdims=True), jnp.max(s_new, axis=-1, keepdims=True))
        p_old = jnp.where(mask_old, jnp.exp(s_old - m), 0.0)
        p_new = jnp.where(mask_new, jnp.exp(s_new - m), 0.0)
        l = jnp.sum(p_old, axis=-1, keepdims=True) + jnp.sum(p_new, axis=-1, keepdims=True)
        pv = _dot(pltpu.roll(p_old, KV_HEADS, axis=1).astype(BF16), x_old)
        return (pv + _dot(p_new.astype(BF16), v_new)) / l

    x_sel = jnp.concatenate([r[0].astype(BF16) for r in blk_refs], axis=0)
    lane = _iota((1, n * BLOCK_ROWS), 1)
    mask_sel = (lane % KV_SLOTS == h) & (lane < (n - 1) * BLOCK_ROWS)
    o_sel = attend(x_sel, mask_sel, ksn_ref[0].astype(BF16), vsn_ref[0].astype(BF16))

    wrows = wbuf_ref.shape[1]
    wb = wrows // KV_SLOTS
    lane_w = _iota((1, wrows), 1)
    mask_win = (lane_w % KV_SLOTS == h) & (lane_w // KV_SLOTS >= qi + (wb - WINDOW))
    o_win = attend(wbuf_ref[0].astype(BF16), mask_win, kwn_ref[0].astype(BF16), vwn_ref[0].astype(BF16))

    gate = _kv_gates(gate_ref.at[0, 0], h)
    outs = []
    for g in range(NSA_GROUP):
        outs.append(gate[:, 3 * g:3 * g + 1] * oc_ref[0, 0, :, g * LANES:(g + 1) * LANES]
                    + gate[:, 3 * g + 1:3 * g + 2] * o_sel[g:g + 1]
                    + gate[:, 3 * g + 2:3 * g + 3] * o_win[g:g + 1])
    o_ref[0, 0] = jnp.concatenate(outs, axis=1).astype(o_ref.dtype)


def sample_attn(idx, page_table, cache_sel, sel_off, cache_win, win_off, qr, rows_s, rows_w, gate, o_cmp):
    bsz, dec, _ = rows_s.shape
    n_pages = page_table.shape[1]
    grp = NSA_GROUP * LANES

    def phys(b, h, q, k, idx_r, pt_r):
        blk = idx_r[((b * KV_HEADS + h) * dec + q) * SEL_TOPN + k]
        return (sel_off + pt_r[b * n_pages + blk // BLOCKS_PER_PAGE]) * BLOCKS_PER_PAGE + blk % BLOCKS_PER_PAGE

    def bspec(k):
        return pl.BlockSpec((1, BLOCK_ROWS, LANES),
                            lambda b, h, q, idx_r, pt_r: (phys(b, h, q, k, idx_r, pt_r), 0, 0))

    qmap = lambda b, h, q, idx_r, pt_r: (b, q, 0, h)
    newk = pl.BlockSpec((1, dec, LANES), lambda b, h, q, idx_r, pt_r: (b, 0, h))
    newv = pl.BlockSpec((1, dec, LANES), lambda b, h, q, idx_r, pt_r: (b, 0, KV_HEADS + h))
    wbuf = pl.BlockSpec((1, cache_win.shape[1], LANES), lambda b, h, q, idx_r, pt_r: (win_off + b, 0, 0))
    grid_spec = pltpu.PrefetchScalarGridSpec(
        num_scalar_prefetch=2,
        grid=(bsz, KV_HEADS, dec),
        in_specs=[bspec(k) for k in range(SEL_TOPN)]
        + [pl.BlockSpec((1, 1, 1, grp), qmap), newk, newv, wbuf, newk, newv,
           pl.BlockSpec((1, 1, 1, LANES), lambda b, h, q, idx_r, pt_r: (b, q, 0, 0)),
           pl.BlockSpec((1, 1, 1, grp), qmap)],
        out_specs=pl.BlockSpec((1, 1, 1, grp), qmap),
    )
    q4 = lambda a: a.reshape(bsz, dec, 1, a.shape[-1])
    return pl.pallas_call(
        functools.partial(_sample_attn_kernel, dec=dec),
        grid_spec=grid_spec,
        out_shape=jax.ShapeDtypeStruct((bsz, dec, 1, B_WIDTH), BF16),
        compiler_params=_cparams(("parallel", "parallel", "arbitrary")),
        name="sample_attn",
    )(idx.reshape(-1), page_table.reshape(-1), *([cache_sel] * SEL_TOPN),
      q4(qr), rows_s, rows_s, cache_win, rows_w, rows_w, q4(gate), q4(o_cmp))


def _shift_rows(x, prev, k):
    xr = pltpu.roll(x, k, axis=0)
    pr = pltpu.roll(prev, k, axis=0)
    top = jnp.where(_iota((SUBLANES, 1), 0) < k, pr, xr[:SUBLANES])
    return jnp.concatenate([top, xr[SUBLANES:]], axis=0)


SSM_PREP_ROWS = 512
SSM_PREP_COLS = 1024


def _ssm_prep_kernel(x_ref, w_ref, b_ref, o_ref, tail_ref, carry_ref):
    @pl.when(pl.program_id(2) == 0)
    def _():
        carry_ref[...] = jnp.zeros_like(carry_ref)

    x = x_ref[...]
    prev = carry_ref[...]
    y = x * w_ref[0, SSM_CONV - 1:SSM_CONV, :] + b_ref[0]
    for k in range(1, SSM_CONV):
        y = y + _shift_rows(x, prev, k) * w_ref[0, SSM_CONV - 1 - k:SSM_CONV - k, :]
    o_ref[...] = y * jax.nn.sigmoid(y)
    carry_ref[...] = x[x.shape[0] - SUBLANES:, :]
    tail_ref[0] = carry_ref[...]


def ssm_prep(proj, col0, conv_w, conv_b, layer, bsz, seq):
    tr, tc = SSM_PREP_ROWS, SSM_PREP_COLS
    nt = seq // tr
    c0 = col0 // tc
    return pl.pallas_call(
        _ssm_prep_kernel,
        grid=(SSM_CONV_DIM // tc, bsz, nt),
        in_specs=[pl.BlockSpec((tr, tc), lambda c, b, i: (b * nt + i, c0 + c)),
                  pl.BlockSpec((1, SSM_CONV, tc), lambda c, b, i: (layer, 0, c)),
                  pl.BlockSpec((1, 1, tc), lambda c, b, i: (layer, 0, c))],
        out_specs=[pl.BlockSpec((tr, tc), lambda c, b, i: (b * nt + i, c)),
                   pl.BlockSpec((1, SUBLANES, tc), lambda c, b, i: (b, 0, c))],
        out_shape=[jax.ShapeDtypeStruct((bsz * seq, SSM_CONV_DIM), F32),
                   jax.ShapeDtypeStruct((bsz, SUBLANES, SSM_CONV_DIM), F32)],
        scratch_shapes=[pltpu.VMEM((SUBLANES, tc), F32)],
        compiler_params=_cparams(("parallel", "parallel", "arbitrary")),
        name="ssm_prep",
    )(proj, conv_w, conv_b.reshape(conv_b.shape[0], 1, -1))


def _taps_kernel(*refs, width, act):
    x_refs, w_ref, b_ref, o_ref = refs[:width], refs[width], refs[width + 1], refs[width + 2]
    y = b_ref[...] + x_refs[0][...] * w_ref[0:1, :]
    for k in range(1, width):
        y = y + x_refs[k][...] * w_ref[k:k + 1, :]
    if act == "silu":
        o_ref[...] = y * jax.nn.sigmoid(y)
    else:
        o_ref[...] = y


def conv_taps(views, w, b, *, act, tile):
    width = len(views)
    m, n = views[0].shape
    col = pl.BlockSpec((m, tile), lambda j: (0, j))
    return pl.pallas_call(
        functools.partial(_taps_kernel, width=width, act=act),
        grid=(n // tile,),
        in_specs=[col] * width + [pl.BlockSpec((width, tile), lambda j: (0, j)),
                                  pl.BlockSpec((1, tile), lambda j: (0, j))],
        out_specs=col,
        out_shape=jax.ShapeDtypeStruct((m, n), F32),
        compiler_params=_cparams(("parallel",)),
        name="conv_taps_" + act,
    )(*views, w, b.reshape(1, -1))


def _glu_kernel(g_ref, u_ref, o_ref):
    g = g_ref[...]
    o_ref[...] = (g * jax.nn.sigmoid(g) * u_ref[...]).astype(o_ref.dtype)


def glu(hc, half, tile):
    m = hc.shape[0]
    nt = half // tile
    return pl.pallas_call(
        _glu_kernel,
        grid=(nt,),
        in_specs=[pl.BlockSpec((m, tile), lambda j: (0, j)), pl.BlockSpec((m, tile), lambda j: (0, nt + j))],
        out_specs=pl.BlockSpec((m, tile), lambda j: (0, j)),
        out_shape=jax.ShapeDtypeStruct((m, half), BF16),
        compiler_params=_cparams(("parallel",)),
        name="glu",
    )(hc, hc)


def _ssd_kernel(xs_ref, bm_ref, cm_ref, z_ref, dt_ref, dtt_ref, al_ref, alt_ref, dsk_ref, ng_ref, h0_ref,
                y_ref, ht_ref, st_ref, *, n_chunks):
    c = pl.program_id(2)
    q = SSM_CHUNK
    e_heads = SSM_GROUP_HEADS

    @pl.when(c == 0)
    def _():
        st_ref[...] = h0_ref[0]

    ri = _iota((q, q), 0)
    ci = _iota((q, q), 1)
    tril = ri >= ci
    tril_b = tril.astype(BF16)
    triu_b = (ri <= ci).astype(BF16)
    expand = (_iota((e_heads, SSM_GROUP_WIDTH), 0) == _iota((e_heads, SSM_GROUP_WIDTH), 1) // SSM_HEAD_DIM).astype(BF16)

    dt = dt_ref[0]
    da = dt * (-jnp.exp(al_ref[0]))
    da_t = dtt_ref[0] * (-jnp.exp(alt_ref[0]))
    acs = _dot3_r(tril_b, da)
    acs_t = _dot3_l(da_t, triu_b)
    tot = acs[q - 1:q, :]

    xs = xs_ref[...]
    xdt = xs * _dot2_l(dt, expand)
    xdt_b = xdt.astype(BF16)
    bm = bm_ref[...]
    cm_b = cm_ref[...].astype(BF16)
    cb = _dot_nt(cm_b, bm.astype(BF16))

    lane = _iota((q, 2 * SSM_HEAD_DIM), 1)
    y_parts = []
    for pair in range(e_heads // 2):
        cols = slice(pair * 2 * SSM_HEAD_DIM, (pair + 1) * 2 * SSM_HEAD_DIM)
        res = []
        for e in (2 * pair, 2 * pair + 1):
            seg = jnp.where(tril, acs[:, e:e + 1] - acs_t[e:e + 1, :], NEG)
            mix = (cb * jnp.exp(seg)).astype(BF16)
            res.append(_dot(mix, xdt_b[:, cols]))
        y_parts.append(jnp.where(lane < SSM_HEAD_DIM, res[0], res[1]))
    y = jnp.concatenate(y_parts, axis=1)

    state = st_ref[...]
    ex = _dot2_l(jnp.concatenate([jnp.exp(acs), jnp.exp(tot - acs)], axis=0), expand)
    y = y + _dot(cm_b, state.astype(BF16)) * ex[0:q]
    contrib = _dot(bm.T.astype(BF16), (xdt * ex[q:2 * q]).astype(BF16))
    st_ref[...] = state * ex[q - 1:q] + contrib

    y = y + xs * dsk_ref[...]
    z = z_ref[...]
    y = y * (z * jax.nn.sigmoid(z))
    ms = jnp.mean(y * y, axis=-1, keepdims=True)
    y_ref[...] = (y * lax.rsqrt(ms + EPS) * ng_ref[...]).astype(y_ref.dtype)

    @pl.when(c == n_chunks - 1)
    def _():
        ht_ref[0] = st_ref[...]


def ssd(xbc, zsrc, dt, a_log, d_skip, norm_g, h0, bsz, seq):
    nc = seq // SSM_CHUNK
    gw = SSM_GROUP_WIDTH
    e = SSM_GROUP_HEADS
    m = bsz * seq
    dt_g = dt.reshape(m, SSM_GROUPS, e).transpose(1, 0, 2)
    dt_t = dt_g.transpose(0, 2, 1)
    al = a_log.astype(F32).reshape(SSM_GROUPS, 1, e)
    al_t = a_log.astype(F32).reshape(SSM_GROUPS, e, 1)
    dsk = jnp.repeat(d_skip.astype(F32), SSM_HEAD_DIM).reshape(1, SSM_INNER)
    b_off = SSM_INNER // SSM_STATE
    c_off = b_off + SSM_GROUPS
    rowmap = lambda col: (lambda b, g, c: (b * nc + c, col(g)))
    return pl.pallas_call(
        functools.partial(_ssd_kernel, n_chunks=nc),
        grid=(bsz, SSM_GROUPS, nc),
        in_specs=[
            pl.BlockSpec((SSM_CHUNK, gw), rowmap(lambda g: g)),
            pl.BlockSpec((SSM_CHUNK, SSM_STATE), rowmap(lambda g: b_off + g)),
            pl.BlockSpec((SSM_CHUNK, SSM_STATE), rowmap(lambda g: c_off + g)),
            pl.BlockSpec((SSM_CHUNK, gw), rowmap(lambda g: g)),
            pl.BlockSpec((1, SSM_CHUNK, e), lambda b, g, c: (g, b * nc + c, 0)),
            pl.BlockSpec((1, e, SSM_CHUNK), lambda b, g, c: (g, 0, b * nc + c)),
            pl.BlockSpec((1, 1, e), lambda b, g, c: (g, 0, 0)),
            pl.BlockSpec((1, e, 1), lambda b, g, c: (g, 0, 0)),
            pl.BlockSpec((1, gw), lambda b, g, c: (0, g)),
            pl.BlockSpec((1, gw), lambda b, g, c: (0, g)),
            pl.BlockSpec((1, SSM_STATE, gw), lambda b, g, c: (b, 0, g)),
        ],
        out_specs=[pl.BlockSpec((SSM_CHUNK, gw), rowmap(lambda g: g)),
                   pl.BlockSpec((1, SSM_STATE, gw), lambda b, g, c: (b, 0, g))],
        out_shape=[jax.ShapeDtypeStruct((m, SSM_INNER), BF16),
                   jax.ShapeDtypeStruct((bsz, SSM_STATE, SSM_INNER), F32)],
        scratch_shapes=[pltpu.VMEM((SSM_STATE, gw), F32)],
        compiler_params=_cparams(("parallel", "parallel", "arbitrary")),
        name="ssd",
    )(xbc, xbc, xbc, zsrc, dt_g, dt_t, al, al_t, dsk, norm_g.astype(F32).reshape(1, SSM_INNER), h0)


def _state_to_cols(h):
    b = h.shape[0]
    return h.transpose(0, 3, 1, 2).reshape(b, SSM_STATE, SSM_INNER)


def _state_from_cols(s):
    b = s.shape[0]
    return s.reshape(b, SSM_STATE, SSM_HEADS, SSM_HEAD_DIM).transpose(0, 2, 3, 1)


FFN_BM = 1024
FFN_SUB = 256
FFN_TILE = 512
FFN_TILES = D_FF // FFN_TILE


def _conv3(h, prev, cw_ref, cb_ref):
    return (_shift_rows(h, prev, 2) * cw_ref[0, 0:1, :] + _shift_rows(h, prev, 1) * cw_ref[0, 1:2, :]
            + h * cw_ref[0, 2:3, :] + cb_ref[0])


def _ffn_up_kernel(x_ref, xs_ref, wg_ref, wu_ref, cwg_ref, cwu_ref, cbg_ref, cbu_ref,
                   act_ref, tg_ref, tu_ref, sg_ref, su_ref, wgb_ref, wub_ref, cg_ref, cu_ref, *, tiles_per_seq):
    i = pl.program_id(1)

    @pl.when(i == 0)
    def _():
        wgb_ref[...] = wg_ref[0].astype(BF16)
        wub_ref[...] = wu_ref[0].astype(BF16)
        sg_ref[...] = _dot(xs_ref[...], wgb_ref[...])
        su_ref[...] = _dot(xs_ref[...], wub_ref[...])

    @pl.when(i % tiles_per_seq == 0)
    def _():
        cg_ref[...] = jnp.zeros_like(cg_ref)
        cu_ref[...] = jnp.zeros_like(cu_ref)

    for r in range(FFN_BM // FFN_SUB):
        rs = slice(r * FFN_SUB, (r + 1) * FFN_SUB)
        x = x_ref[rs, :]
        hg = _dot(x, wgb_ref[...])
        hu = _dot(x, wub_ref[...])
        g = _conv3(hg, cg_ref[...], cwg_ref, cbg_ref)
        u = _conv3(hu, cu_ref[...], cwu_ref, cbu_ref)
        act_ref[rs, :] = (g * jax.nn.sigmoid(g) * u).astype(act_ref.dtype)
        cg_ref[...] = hg[FFN_SUB - SUBLANES:, :]
        cu_ref[...] = hu[FFN_SUB - SUBLANES:, :]
    tg_ref[0] = cg_ref[...]
    tu_ref[0] = cu_ref[...]


def ffn_up(xp, xs, w_up, conv_w, conv_b, layer, bsz, seq):
    m, k = xp.shape
    ms = xs.shape[0]
    tiles_per_seq = seq // FFN_BM
    nt = FFN_TILES
    wspec = lambda off: pl.BlockSpec((1, k, FFN_TILE), lambda j, i: (layer, 0, off + j))
    cwspec = lambda off: pl.BlockSpec((1, FFN_CONV, FFN_TILE), lambda j, i: (layer, 0, off + j))
    cbspec = lambda off: pl.BlockSpec((1, 1, FFN_TILE), lambda j, i: (layer, 0, off + j))
    tail = pl.BlockSpec((1, SUBLANES, FFN_TILE), lambda j, i: (i // tiles_per_seq, 0, j))
    samp = pl.BlockSpec((ms, FFN_TILE), lambda j, i: (0, j))
    cb3 = conv_b.reshape(conv_b.shape[0], 1, -1)
    return pl.pallas_call(
        functools.partial(_ffn_up_kernel, tiles_per_seq=tiles_per_seq),
        grid=(nt, m // FFN_BM),
        in_specs=[pl.BlockSpec((FFN_BM, k), lambda j, i: (i, 0)), pl.BlockSpec((ms, k), lambda j, i: (0, 0)),
                  wspec(0), wspec(nt), cwspec(0), cwspec(nt), cbspec(0), cbspec(nt)],
        out_specs=[pl.BlockSpec((FFN_BM, FFN_TILE), lambda j, i: (i, j)), tail, tail, samp, samp],
        out_shape=[jax.ShapeDtypeStruct((m, D_FF), BF16),
                   jax.ShapeDtypeStruct((bsz, SUBLANES, D_FF), F32), jax.ShapeDtypeStruct((bsz, SUBLANES, D_FF), F32),
                   jax.ShapeDtypeStruct((ms, D_FF), F32), jax.ShapeDtypeStruct((ms, D_FF), F32)],
        scratch_shapes=[pltpu.VMEM((k, FFN_TILE), BF16), pltpu.VMEM((k, FFN_TILE), BF16),
                        pltpu.VMEM((SUBLANES, FFN_TILE), F32), pltpu.VMEM((SUBLANES, FFN_TILE), F32)],
        compiler_params=_cparams(("parallel", "arbitrary")),
        name="ffn_up",
    )(xp, xs, w_up, w_up, conv_w, conv_w, cb3, cb3)


def _rope_tables(pos):
    half = HEAD_DIM // 2
    inv = ROPE_THETA ** (-jnp.arange(half, dtype=F32) / half)
    ang = pos.astype(F32)[:, None] * inv[None, :]
    cos, sin = jnp.cos(ang), jnp.sin(ang)
    return jnp.concatenate([cos, cos], axis=1), jnp.concatenate([-sin, sin], axis=1)


def _kv_out(rows, bsz):
    return rows.reshape(bsz, -1, 2, KV_HEADS, HEAD_DIM)


def kernel(x_prompt, x_sample, cache_kv_cmp, cache_kv_sel, cache_kv_win, state_ssm_conv, state_ssm, state_ffn_conv, page_table, norm_mix, norm_ffn, w_in_even, w_out_even, gmlp_v_norm, gmlp_ws, gmlp_bs, q_norm, k_norm, cmp_pool, w_in_odd, ssm_conv_w, ssm_conv_b, ssm_dt_bias, ssm_a_log, ssm_d, ssm_norm, w_out_odd, ffn_w_up, ffn_conv_w, ffn_conv_b, ffn_w_down):
    bp, sp, _ = x_prompt.shape
    bs, ss, _ = x_sample.shape
    depth = norm_mix.shape[0]
    n_pool = cache_kv_cmp.shape[1]
    n_pages = page_table.shape[1]
    past_len = n_pages * PAGE_SIZE
    wb = cache_kv_win.shape[2]
    mp, ms = bp * sp, bs * ss

    cos_p, sin_p = _rope_tables(jnp.arange(sp, dtype=jnp.int32))
    cos_p, sin_p = jnp.tile(cos_p, (bp, 1)), jnp.tile(sin_p, (bp, 1))
    cos_s, sin_s = _rope_tables(past_len + jnp.arange(ss, dtype=jnp.int32))
    cos_s, sin_s = jnp.tile(cos_s, (bs, 1)), jnp.tile(sin_s, (bs, 1))

    cache_c = cache_kv_cmp.reshape(-1, PAGE_ROWS, LANES)
    cache_s = cache_kv_sel.reshape(-1, BLOCK_ROWS, LANES)
    cache_w = cache_kv_win.reshape(-1, wb * KV_SLOTS, LANES)

    w_down16 = ffn_w_down.astype(BF16)
    w_out_odd16 = w_out_odd.astype(BF16)
    w_in_even_t = jnp.swapaxes(w_in_even, 1, 2)
    w_in_odd_t = jnp.swapaxes(w_in_odd, 1, 2)

    hp = x_prompt.reshape(mp, D_MODEL)
    hs = x_sample.reshape(ms, D_MODEL)
    outs = {k: [] for k in ("p_cmp", "p_sel", "p_win", "s_cmp", "s_sel", "s_win", "s_v",
                            "p_sconv", "p_sst", "s_sconv", "s_sst", "p_fconv", "s_fconv")}

    for layer in range(depth):
        i = layer // 2
        xp = rms_cast(hp, norm_mix[layer])
        xs = rms_cast(hs, norm_mix[layer])
        if layer % 2 == 0:
            w_gate = w_in_even_t[i, E_MAIN:]
            proj_p, proj_s = matmul_ws([xp], [xs], w_in_even_t, i, E_MAIN, bn=768, w_t=True, name="even_in")
            gl = matmul_narrow(xp, w_gate)
            u, v, qn, qr, rc, rs, rw, gate, ks16, kw16, vts16, vtw16 = even_prep(
                proj_p, gl, cos_p, sin_p, gmlp_v_norm[i], q_norm[i], k_norm[i], for_mxu=True)
            a_p = gmlp(u, v, gmlp_ws[i], gmlp_bs[i])
            kcmp = compress(rc, cmp_pool[i])
            b_p = nsa_prompt(qn, qr, kcmp, ks16, vts16, kw16, vtw16, gate, bp, sp)
            outs["p_cmp"].append(_kv_out(rc, bp))
            outs["p_sel"].append(_kv_out(rs, bp))
            outs["p_win"].append(_kv_out(rw, bp)[:, sp - min(WINDOW, sp):])
            gl = matmul_narrow(xs, w_gate)
            u, v, qn, qr, rc, rs, rw, gate = even_prep(
                proj_s, gl, cos_s, sin_s, gmlp_v_norm[i], q_norm[i], k_norm[i], for_mxu=False)
            lpad = ((0, 0), (0, A_CHUNK - ss), (0, 0))
            a_s = gmlp(jnp.pad(u.reshape(bs, ss, -1), lpad).reshape(bs * A_CHUNK, -1),
                       jnp.pad(v.reshape(bs, ss, -1), lpad).reshape(bs * A_CHUNK, -1),
                       gmlp_ws[i], gmlp_bs[i]).reshape(bs, A_CHUNK, -1)[:, :ss].reshape(ms, -1)
            kc_past = pool_past(cache_c, i * n_pool, page_table, cmp_pool[i])
            o_cmp, idx = sample_select(qn.reshape(bs, ss, -1), kc_past)
            b_s = sample_attn(idx, page_table, cache_s, i * n_pool, cache_w, i * bs,
                              qr.reshape(bs, ss, -1), rs.reshape(bs, ss, -1), rw.reshape(bs, ss, -1),
                              gate.reshape(bs, ss, -1), o_cmp).reshape(ms, B_WIDTH)
            outs["s_cmp"].append(_kv_out(rc, bs))
            outs["s_sel"].append(_kv_out(rs, bs))
            outs["s_win"].append(_kv_out(rw, bs))
            outs["s_v"].append(v.reshape(bs, ss, A_WIDTH))
            hp, hs = matmul_ws([a_p, b_p], [a_s, b_s], w_out_even, i, D_MODEL, bn=1024, res=(hp, hs), name="even_out")
        else:
            zx = SSM_INNER + SSM_CONV_DIM
            w_dt = w_in_odd_t[i, zx:]
            proj_p, proj_s = matmul_ws([xp], [xs], w_in_odd_t, i, zx, bn=1024, w_t=True, name="odd_in")
            dt = matmul_narrow(xp, w_dt, ssm_dt_bias[i])[:, :SSM_HEADS]
            xbc, tail = ssm_prep(proj_p, SSM_INNER, ssm_conv_w, ssm_conv_b, i, bp, sp)
            h0 = jnp.zeros((bp, SSM_STATE, SSM_INNER), F32)
            y, ht = ssd(xbc, proj_p, dt, ssm_a_log[i], ssm_d[i], ssm_norm[i], h0, bp, sp)
            hp = matmul(y, w_out_odd16, layer=i, bn=512, res=hp, name="odd_out")
            outs["p_sconv"].append(tail[:, SUBLANES - (SSM_CONV - 1):])
            outs["p_sst"].append(_state_from_cols(ht))
            dt = matmul_narrow(xs, w_dt, ssm_dt_bias[i])[:, :SSM_HEADS]
            xin = jnp.concatenate([state_ssm_conv[i], proj_s[:, SSM_INNER:].reshape(bs, ss, -1)], axis=1)
            views = [xin[:, k:k + ss].reshape(ms, -1) for k in range(SSM_CONV)]
            xbc = conv_taps(views, ssm_conv_w[i], ssm_conv_b[i], act="silu", tile=1024)
            cpad = ((0, 0), (0, SSM_CHUNK - ss), (0, 0))
            padrows = lambda a: jnp.pad(a.reshape(bs, ss, -1), cpad).reshape(bs * SSM_CHUNK, -1)
            y, ht = ssd(padrows(xbc), padrows(proj_s[:, :SSM_INNER]), padrows(dt), ssm_a_log[i], ssm_d[i], ssm_norm[i],
                        _state_to_cols(state_ssm[i].astype(F32)), bs, SSM_CHUNK)
            y = y.reshape(bs, SSM_CHUNK, -1)[:, :ss].reshape(ms, -1)
            hs = matmul(y, w_out_odd16, layer=i, bn=1024, res=hs, name="odd_out_s")
            outs["s_sconv"].append(xin[:, ss:])
            outs["s_sst"].append(_state_from_cols(ht))
        xp = rms_cast(hp, norm_ffn[layer])
        xs = rms_cast(hs, norm_ffn[layer])
        act, tail_g, tail_u, hu_g, hu_u = ffn_up(xp, xs, ffn_w_up, ffn_conv_w, ffn_conv_b, layer, bp, sp)
        hp = matmul(act, w_down16, layer=layer, bn=512, res=hp, name="ffn_down")
        outs["p_fconv"].append(jnp.concatenate([tail_g, tail_u], axis=2)[:, SUBLANES - (FFN_CONV - 1):])
        hu = jnp.concatenate([hu_g, hu_u], axis=1)
        xin = jnp.concatenate([state_ffn_conv[layer], hu.reshape(bs, ss, -1)], axis=1)
        views = [xin[:, k:k + ss].reshape(ms, -1) for k in range(FFN_CONV)]
        hc = conv_taps(views, ffn_conv_w[layer], ffn_conv_b[layer], act="none", tile=1024)
        hs = matmul(glu(hc, D_FF, 512), w_down16, layer=layer, bn=512, res=hs, name="ffn_down_s")
        outs["s_fconv"].append(xin[:, ss:])

    st = lambda k: jnp.stack(outs[k])
    return (hp.reshape(bp, sp, D_MODEL), hs.reshape(bs, ss, D_MODEL), st("p_cmp"), st("p_sel"), st("p_win"),
            st("p_sconv"), st("p_sst"), st("p_fconv"), st("s_cmp"), st("s_sel"), st("s_win"), st("s_v"),
            st("s_sconv"), st("s_sst"), st("s_fconv"))
```

```python
import functools

import jax
import jax.numpy as jnp
from jax import lax
from jax.experimental import pallas as pl
from jax.experimental.pallas import tpu as pltpu

F32 = jnp.float32
BF16 = jnp.bfloat16

D_MODEL = 2048
A_WIDTH = D_MODEL // 2
A_GROUPS = 8
A_CHUNK = 128
NSA_HEADS = 8
HEAD_DIM = 128
KV_HEADS = 2
NSA_GROUP = NSA_HEADS // KV_HEADS
B_WIDTH = NSA_HEADS * HEAD_DIM
KV_COLS = 2 * KV_HEADS * HEAD_DIM
KV_SLOTS = 2 * KV_HEADS
CMP_BLOCK = 64
SEL_TOPN = 16
WINDOW = 512
PAGE_SIZE = 128
ROPE_THETA = 10000.0
ATTN_SCALE = HEAD_DIM ** -0.5
LOG2E = 1.4426950408889634
SSM_INNER = 2 * D_MODEL
SSM_HEAD_DIM = 64
SSM_HEADS = SSM_INNER // SSM_HEAD_DIM
SSM_GROUPS = 8
SSM_GROUP_HEADS = SSM_HEADS // SSM_GROUPS
SSM_GROUP_WIDTH = SSM_INNER // SSM_GROUPS
SSM_STATE = 128
SSM_CONV = 4
SSM_CHUNK = 128
SSM_CONV_DIM = SSM_INNER + 2 * SSM_GROUPS * SSM_STATE
D_FF = 5632
FFN_CONV = 3
EPS = 1e-6
NEG = -1e30
FORCE = 1e4
TINY = 1e-30

VMEM_LIMIT_BYTES = 56 * 1024 * 1024
LANES = 128
SUBLANES = 8

E_MAIN = 2 * A_WIDTH + B_WIDTH + 3 * KV_COLS
N_GATES = 3 * NSA_HEADS
GATES_PER_KV = 3 * NSA_GROUP


def _cparams(sem):
    return pltpu.CompilerParams(dimension_semantics=sem, vmem_limit_bytes=VMEM_LIMIT_BYTES)


def _dot(a, b):
    return jnp.dot(a, b, preferred_element_type=F32)


def _dot_nt(a, b):
    return lax.dot_general(a, b, (((1,), (1,)), ((), ())), preferred_element_type=F32)


def _iota(shape, dim):
    return lax.broadcasted_iota(jnp.int32, shape, dim)


def _split3(x):
    hi = x.astype(BF16)
    r1 = x - hi.astype(F32)
    mid = r1.astype(BF16)
    lo = (r1 - mid.astype(F32)).astype(BF16)
    return hi, mid, lo


def _dot3_l(x, m):
    hi, mid, lo = _split3(x)
    return _dot(hi, m) + _dot(mid, m) + _dot(lo, m)


def _dot3_r(m, x):
    hi, mid, lo = _split3(x)
    return _dot(m, hi) + _dot(m, mid) + _dot(m, lo)


def _dot2_l(x, m):
    hi = x.astype(BF16)
    return _dot(hi, m) + _dot((x - hi.astype(F32)).astype(BF16), m)


def _rms_kernel(x_ref, g_ref, o_ref):
    x = x_ref[...]
    ms = jnp.mean(x * x, axis=-1, keepdims=True)
    o_ref[...] = (x * lax.rsqrt(ms + EPS) * g_ref[...]).astype(o_ref.dtype)


def rms_cast(x, g):
    m, d = x.shape
    tr = min(m, 512)
    return pl.pallas_call(
        _rms_kernel,
        grid=(m // tr,),
        in_specs=[pl.BlockSpec((tr, d), lambda i: (i, 0)), pl.BlockSpec((1, d), lambda i: (0, 0))],
        out_specs=pl.BlockSpec((tr, d), lambda i: (i, 0)),
        out_shape=jax.ShapeDtypeStruct((m, d), BF16),
        compiler_params=_cparams(("parallel",)),
        name="rms_cast",
    )(x, g.reshape(1, d))


def _mm_kernel(x_ref, w_ref, o_ref):
    o_ref[...] = _dot(x_ref[...], w_ref[0]).astype(o_ref.dtype)


def _mm_res_kernel(x_ref, w_ref, r_ref, o_ref):
    o_ref[...] = (r_ref[...] + _dot(x_ref[...], w_ref[0])).astype(o_ref.dtype)


def matmul(x, w, *, bn, layer=0, res=None, out_dtype=F32, name="mm"):
    m, k = x.shape
    if w.ndim == 2:
        w = w[None]
    n = w.shape[2]
    bm = min(m, 1024)
    in_specs = [pl.BlockSpec((bm, k), lambda i, j: (i, 0)), pl.BlockSpec((1, k, bn), lambda i, j: (layer, 0, j))]
    args = [x, w]
    kern = _mm_kernel
    if res is not None:
        in_specs.append(pl.BlockSpec((bm, bn), lambda i, j: (i, j)))
        args.append(res)
        kern = _mm_res_kernel
    return pl.pallas_call(
        kern,
        grid=(m // bm, n // bn),
        in_specs=in_specs,
        out_specs=pl.BlockSpec((bm, bn), lambda i, j: (i, j)),
        out_shape=jax.ShapeDtypeStruct((m, n), out_dtype),
        compiler_params=_cparams(("parallel", "parallel")),
        name=name,
    )(*args)


WS_BM = 1024


def _mm_ws_kernel(*refs, n_parts, has_res, w_t):
    xp, xs, w_ref = refs[:n_parts], refs[n_parts:2 * n_parts], refs[2 * n_parts]
    pos = 2 * n_parts + 1
    if has_res:
        r_ref, rs_ref = refs[pos], refs[pos + 1]
        pos += 2
    o_ref, os_ref, wb_ref = refs[pos], refs[pos + 1], refs[pos + 2]

    def mm(parts):
        acc, k0 = None, 0
        for p in parts:
            kp = p.shape[1]
            t = _dot(p[...], wb_ref[k0:k0 + kp, :])
            acc = t if acc is None else acc + t
            k0 += kp
        return acc

    @pl.when(pl.program_id(1) == 0)
    def _():
        wb_ref[...] = (w_ref[0].T if w_t else w_ref[0]).astype(BF16)
        ys = mm(xs)
        if has_res:
            ys = rs_ref[...] + ys
        os_ref[...] = ys.astype(os_ref.dtype)

    y = mm(xp)
    if has_res:
        y = r_ref[...] + y
    o_ref[...] = y.astype(o_ref.dtype)


def matmul_ws(xp_parts, xs_parts, w, layer, n_cols, *, bn, w_t=False, res=None, name="mm_ws"):
    mp = xp_parts[0].shape[0]
    ms = xs_parts[0].shape[0]
    k = w.shape[2] if w_t else w.shape[1]
    n_parts = len(xp_parts)
    in_specs = [pl.BlockSpec((WS_BM, p.shape[1]), lambda j, i: (i, 0)) for p in xp_parts]
    in_specs += [pl.BlockSpec((ms, p.shape[1]), lambda j, i: (0, 0)) for p in xs_parts]
    in_specs.append(pl.BlockSpec((1, bn, k), lambda j, i: (layer, j, 0)) if w_t
                    else pl.BlockSpec((1, k, bn), lambda j, i: (layer, 0, j)))
    args = list(xp_parts) + list(xs_parts) + [w]
    if res is not None:
        in_specs += [pl.BlockSpec((WS_BM, bn), lambda j, i: (i, j)), pl.BlockSpec((ms, bn), lambda j, i: (0, j))]
        args += list(res)
    return pl.pallas_call(
        functools.partial(_mm_ws_kernel, n_parts=n_parts, has_res=res is not None, w_t=w_t),
        grid=(n_cols // bn, mp // WS_BM),
        in_specs=in_specs,
        out_specs=[pl.BlockSpec((WS_BM, bn), lambda j, i: (i, j)), pl.BlockSpec((ms, bn), lambda j, i: (0, j))],
        out_shape=[jax.ShapeDtypeStruct((mp, n_cols), F32), jax.ShapeDtypeStruct((ms, n_cols), F32)],
        scratch_shapes=[pltpu.VMEM((k, bn), BF16)],
        compiler_params=_cparams(("parallel", "arbitrary")),
        name=name,
    )(*args)


def _mm_narrow_kernel(x_ref, wt_ref, b_ref, o_ref, *, softplus):
    y = _dot_nt(x_ref[...], wt_ref[...].astype(BF16))
    if softplus:
        y = jax.nn.softplus(y + b_ref[...])
    o_ref[...] = y


def matmul_narrow(x, wt, bias=None):
    m, k = x.shape
    n = wt.shape[0]
    bm = min(m, 1024)
    b = jnp.zeros((LANES,), F32) if bias is None else jnp.pad(bias.astype(F32), (0, LANES - n))
    return pl.pallas_call(
        functools.partial(_mm_narrow_kernel, softplus=bias is not None),
        grid=(m // bm,),
        in_specs=[pl.BlockSpec((bm, k), lambda i: (i, 0)), pl.BlockSpec((LANES, k), lambda i: (0, 0)),
                  pl.BlockSpec((1, LANES), lambda i: (0, 0))],
        out_specs=pl.BlockSpec((bm, LANES), lambda i: (i, 0)),
        out_shape=jax.ShapeDtypeStruct((m, LANES), F32),
        compiler_params=_cparams(("parallel",)),
        name="mm_narrow",
    )(x, jnp.pad(wt, ((0, LANES - n), (0, 0))), b.reshape(1, LANES))


def _head_rms(x, gain):
    ms = jnp.mean(x * x, axis=-1, keepdims=True)
    return x * lax.rsqrt(ms + EPS) * gain


def _rope(x, cosf, sinf):
    return x * cosf + pltpu.roll(x, HEAD_DIM // 2, axis=1) * sinf


def _even_prep_kernel(p_ref, gl_ref, cos_ref, sin_ref, vg_ref, qn_ref, kn_ref,
                      u_ref, v_ref, q_ref, qr_ref, rc_ref, rs_ref, rw_ref, gate_ref, *mxu_refs):
    cosf = cos_ref[...]
    sinf = sin_ref[...]
    u_ref[...] = jax.nn.gelu(p_ref[:, 0:A_WIDTH]).astype(u_ref.dtype)
    for g in range(A_GROUPS):
        sl = slice(g * LANES, (g + 1) * LANES)
        vg = jax.nn.gelu(p_ref[:, A_WIDTH + g * LANES:A_WIDTH + (g + 1) * LANES])
        v_ref[:, sl] = _head_rms(vg, vg_ref[:, sl])
    qgain = qn_ref[...]
    for h in range(NSA_HEADS):
        sl = slice(h * LANES, (h + 1) * LANES)
        q = _head_rms(p_ref[:, 2 * A_WIDTH + h * LANES:2 * A_WIDTH + (h + 1) * LANES], qgain)
        q_ref[:, sl] = q.astype(q_ref.dtype)
        qr_ref[:, sl] = _rope(q, cosf, sinf).astype(qr_ref.dtype)
    base = 2 * A_WIDTH + B_WIDTH
    for which, row_ref in enumerate((rc_ref, rs_ref, rw_ref)):
        gain = kn_ref[which:which + 1, :]
        off = base + which * KV_COLS
        k16_ref, vt_ref = (mxu_refs[which - 1], mxu_refs[which + 1]) if (mxu_refs and which > 0) else (None, None)
        for h in range(KV_HEADS):
            sl = slice(h * LANES, (h + 1) * LANES)
            k = _head_rms(p_ref[:, off + h * LANES:off + (h + 1) * LANES], gain)
            if which > 0:
                k = _rope(k, cosf, sinf)
            row_ref[:, sl] = k
            vals = p_ref[:, off + (KV_HEADS + h) * LANES:off + (KV_HEADS + h + 1) * LANES]
            row_ref[:, (KV_HEADS + h) * LANES:(KV_HEADS + h + 1) * LANES] = vals
            if k16_ref is not None:
                k16_ref[:, sl] = k.astype(BF16)
                vt_ref[0, h] = vals.T.astype(BF16)
    gate_ref[...] = jax.nn.sigmoid(gl_ref[...])


def even_prep(proj, gl, cosf, sinf, v_gain, q_gain, k_gain, *, for_mxu):
    m = proj.shape[0]
    tr = min(m, SEL_KC)
    row = lambda w: pl.BlockSpec((tr, w), lambda i: (i, 0))
    full = lambda r, w: pl.BlockSpec((r, w), lambda i: (0, 0))
    out_shapes = [
        jax.ShapeDtypeStruct((m, A_WIDTH), BF16),
        jax.ShapeDtypeStruct((m, A_WIDTH), F32),
        jax.ShapeDtypeStruct((m, B_WIDTH), BF16),
        jax.ShapeDtypeStruct((m, B_WIDTH), BF16),
        jax.ShapeDtypeStruct((m, KV_COLS), F32),
        jax.ShapeDtypeStruct((m, KV_COLS), F32),
        jax.ShapeDtypeStruct((m, KV_COLS), F32),
        jax.ShapeDtypeStruct((m, LANES), F32),
    ]
    out_specs = [row(s.shape[1]) for s in out_shapes]
    if for_mxu:
        assert tr == SEL_KC
        out_shapes += [jax.ShapeDtypeStruct((m, KV_HEADS * LANES), BF16)] * 2
        out_specs += [row(KV_HEADS * LANES)] * 2
        out_shapes += [jax.ShapeDtypeStruct((m // tr, KV_HEADS, HEAD_DIM, tr), BF16)] * 2
        out_specs += [pl.BlockSpec((1, KV_HEADS, HEAD_DIM, tr), lambda i: (i, 0, 0, 0))] * 2
    return pl.pallas_call(
        _even_prep_kernel,
        grid=(m // tr,),
        in_specs=[row(E_MAIN), row(LANES), row(LANES), row(LANES),
                  full(1, A_WIDTH), full(1, LANES), full(3, LANES)],
        out_specs=out_specs,
        out_shape=out_shapes,
        compiler_params=_cparams(("parallel",)),
        name="even_prep",
    )(proj, gl, cosf, sinf, v_gain.reshape(1, A_WIDTH), q_gain.reshape(1, LANES), k_gain)


def _gmlp_kernel(u_ref, v_ref, ws_ref, bst_ref, o_ref):
    tril = _iota((A_CHUNK, A_CHUNK), 0) >= _iota((A_CHUNK, A_CHUNK), 1)
    for g in range(A_GROUPS):
        sl = slice(g * LANES, (g + 1) * LANES)
        wm = jnp.where(tril, ws_ref[g], 0.0).astype(BF16)
        s = _dot(wm, v_ref[:, sl].astype(BF16)) + bst_ref[:, g:g + 1]
        o_ref[:, sl] = (u_ref[:, sl].astype(F32) * s).astype(o_ref.dtype)


def gmlp(u, v, ws, bs):
    m = u.shape[0]
    row = pl.BlockSpec((A_CHUNK, A_WIDTH), lambda i: (i, 0))
    return pl.pallas_call(
        _gmlp_kernel,
        grid=(m // A_CHUNK,),
        in_specs=[row, row,
                  pl.BlockSpec((A_GROUPS, A_CHUNK, A_CHUNK), lambda i: (0, 0, 0)),
                  pl.BlockSpec((A_CHUNK, A_GROUPS), lambda i: (0, 0))],
        out_specs=row,
        out_shape=jax.ShapeDtypeStruct((m, A_WIDTH), BF16),
        compiler_params=_cparams(("parallel",)),
        name="gmlp",
    )(u, v, ws, bs.T)


def _compress_kernel(x_ref, pool_ref, o_ref):
    pool = pool_ref[...]
    e = jnp.exp(pool - jnp.max(pool, axis=0, keepdims=True))
    w = e / jnp.sum(e, axis=0, keepdims=True)
    x = x_ref[...]
    nb = x.shape[0] // CMP_BLOCK
    o_ref[...] = jnp.sum(x.reshape(nb, CMP_BLOCK, KV_COLS) * w[None], axis=1)


def compress(rows, pool):
    m = rows.shape[0]
    tr = min(m, 1024)
    pool_cols = jnp.concatenate([jnp.repeat(pool.T, LANES, axis=1)] * 2, axis=1)
    return pl.pallas_call(
        _compress_kernel,
        grid=(m // tr,),
        in_specs=[pl.BlockSpec((tr, KV_COLS), lambda i: (i, 0)),
                  pl.BlockSpec((CMP_BLOCK, KV_COLS), lambda i: (0, 0))],
        out_specs=pl.BlockSpec((tr // CMP_BLOCK, KV_COLS), lambda i: (i, 0)),
        out_shape=jax.ShapeDtypeStruct((m // CMP_BLOCK, KV_COLS), F32),
        compiler_params=_cparams(("parallel",)),
        name="compress",
    )(rows, pool_cols)


NSA_TQ = 128
SEL_KC = 512
EXP2_SCALE = ATTN_SCALE * LOG2E


def _kv_gates(gate_ref, h):
    g = gate_ref[...]
    return jnp.where(h == 0, g[:, 0:GATES_PER_KV], g[:, GATES_PER_KV:2 * GATES_PER_KV])


def _pad_rows(x, rows):
    return jnp.concatenate([x, jnp.zeros((rows - x.shape[0], x.shape[1]), x.dtype)], axis=0)


def _nsa_prompt_kernel(qn_ref, qr_ref, kc_ref, vc_ref, ks_ref, vts_ref, kw_ref, vtw_ref, gate_ref, o_ref,
                       acc_ref, osel_ref, *, seq):
    h = pl.program_id(1)
    t = pl.program_id(2)
    nb = seq // CMP_BLOCK
    tq = NSA_TQ
    qpos_t = t * tq + _iota((1, tq), 1)
    blk_t = _iota((nb, 1), 0)

    heads = lambda ref: jnp.concatenate([ref[:, g * LANES:(g + 1) * LANES] for g in range(NSA_GROUP)], axis=0)
    wide = lambda x: jnp.concatenate([x] * NSA_GROUP, axis=1)
    qn4 = heads(qn_ref)
    qr4 = heads(qr_ref)

    kc = kc_ref[0].astype(BF16)
    vc_t = _pad_rows(vc_ref[0], LANES).T.astype(BF16)
    valid_ct = ((blk_t + 1) * CMP_BLOCK - 1) <= t * tq + _iota((1, NSA_GROUP * tq), 1) % tq
    st = jnp.where(valid_ct, _dot_nt(kc, qn4) * ATTN_SCALE, NEG)
    et = jnp.where(valid_ct, jnp.exp(st - jnp.max(st, axis=0, keepdims=True)), 0.0)
    pt = et / jnp.maximum(jnp.sum(et, axis=0, keepdims=True), TINY)
    o_cmp = _dot(vc_t, _pad_rows(pt, LANES).astype(BF16))
    imp_t = pt[:, 0:tq]
    for g in range(1, NSA_GROUP):
        imp_t = imp_t + pt[:, g * tq:(g + 1) * tq]

    cur_t = qpos_t // CMP_BLOCK
    forced = (blk_t == 0) | (blk_t == cur_t)
    score = jnp.where(forced, FORCE, jnp.where(blk_t > cur_t, NEG, imp_t))
    rank = jnp.zeros((nb, tq), jnp.int32)
    for j in range(nb):
        row = score[j:j + 1, :]
        beats = (row > score) | ((row == score) & (blk_t > j))
        rank = rank + beats.astype(jnp.int32)
    sel = _pad_rows((rank < min(SEL_TOPN, nb)).astype(F32), LANES).astype(BF16)

    def attend(lo, hi, k_ref, vt_ref, bias_fn):
        acc_ref[...] = jnp.zeros(acc_ref.shape, F32)

        def body(c, carry):
            m, l = carry
            start = pl.multiple_of(c * SEL_KC, SEL_KC)
            s = _dot_nt(k_ref[pl.ds(start, SEL_KC), :], qr4) + wide(bias_fn(c))
            m_new = jnp.maximum(m, jnp.max(s, axis=0, keepdims=True))
            p = jnp.exp2((s - m_new) * EXP2_SCALE)
            alpha = jnp.exp2((m - m_new) * EXP2_SCALE)
            acc_ref[...] = alpha * acc_ref[...] + _dot(vt_ref[c, 0], p.astype(BF16))
            return m_new, alpha * l + jnp.sum(p, axis=0, keepdims=True)

        init = (jnp.full((1, NSA_GROUP * tq), NEG, F32), jnp.zeros((1, NSA_GROUP * tq), F32))
        return lax.fori_loop(lo, hi, body, init)[1]

    def kpos_col(c):
        return c * SEL_KC + _iota((SEL_KC, 1), 0)

    def sel_bias(c):
        expand = ((c * SEL_KC + _iota((SEL_KC, LANES), 0)) // CMP_BLOCK == _iota((SEL_KC, LANES), 1)).astype(BF16)
        chosen = _dot(expand, sel) > 0.5
        return jnp.where(chosen & (kpos_col(c) <= qpos_t), 0.0, NEG)

    def win_bias(c):
        kpos = kpos_col(c)
        return jnp.where((kpos <= qpos_t) & (kpos >= qpos_t - WINDOW), 0.0, NEG)

    hi = (t * tq + tq - 1) // SEL_KC + 1
    l_sel = attend(0, hi, ks_ref, vts_ref, sel_bias)
    osel_ref[...] = acc_ref[...] / l_sel
    l_win = attend(jnp.maximum(t * tq - WINDOW, 0) // SEL_KC, hi, kw_ref, vtw_ref, win_bias)
    o_win = acc_ref[...] / l_win

    g_all = gate_ref[...]
    gate_t = jnp.where(h == 0, g_all, pltpu.roll(g_all, LANES - GATES_PER_KV, axis=1)).T
    for g in range(NSA_GROUP):
        cs = slice(g * tq, (g + 1) * tq)
        out_t = (gate_t[3 * g:3 * g + 1] * o_cmp[:, cs] + gate_t[3 * g + 1:3 * g + 2] * osel_ref[:, cs]
                 + gate_t[3 * g + 2:3 * g + 3] * o_win[:, cs])
        o_ref[:, g * LANES:(g + 1) * LANES] = out_t.T.astype(o_ref.dtype)


def nsa_prompt(qn, qr, kcmp, ks16, vts16, kw16, vtw16, gate, bsz, seq):
    nt = seq // NSA_TQ
    nb = seq // CMP_BLOCK
    nc = seq // SEL_KC
    assert nb <= LANES and seq % SEL_KC == 0
    qspec = pl.BlockSpec((NSA_TQ, NSA_GROUP * LANES), lambda b, h, t: (b * nt + t, h))
    cmp_k = pl.BlockSpec((1, nb, LANES), lambda b, h, t: (b, 0, h))
    cmp_v = pl.BlockSpec((1, nb, LANES), lambda b, h, t: (b, 0, KV_HEADS + h))
    key = pl.BlockSpec((seq, LANES), lambda b, h, t: (b, h))
    val_t = pl.BlockSpec((nc, 1, HEAD_DIM, SEL_KC), lambda b, h, t: (b, h, 0, 0))
    kc3 = kcmp.reshape(bsz, nb, KV_COLS)
    return pl.pallas_call(
        functools.partial(_nsa_prompt_kernel, seq=seq),
        grid=(bsz, KV_HEADS, nt),
        in_specs=[qspec, qspec, cmp_k, cmp_v, key, val_t, key, val_t,
                  pl.BlockSpec((NSA_TQ, LANES), lambda b, h, t: (b * nt + t, 0))],
        out_specs=qspec,
        out_shape=jax.ShapeDtypeStruct((bsz * seq, B_WIDTH), BF16),
        scratch_shapes=[pltpu.VMEM((HEAD_DIM, NSA_GROUP * NSA_TQ), F32),
                        pltpu.VMEM((HEAD_DIM, NSA_GROUP * NSA_TQ), F32)],
        compiler_params=_cparams(("parallel", "parallel", "arbitrary")),
        name="nsa_prompt",
    )(qn, qr, kc3, kc3, ks16, vts16, kw16, vtw16, gate)


PAGES_PER_STEP = 16
PAGE_ROWS = PAGE_SIZE * KV_SLOTS
BLOCK_ROWS = CMP_BLOCK * KV_SLOTS
BLOCKS_PER_PAGE = PAGE_SIZE // CMP_BLOCK


def _pool_past_kernel(pt_ref, *refs, pps):
    page_refs, pool_ref, o_ref = refs[:pps], refs[pps], refs[pps + 1]
    tiles = BLOCK_ROWS // SUBLANES
    pool = pool_ref[...].reshape(tiles, SUBLANES, LANES)
    fold = lambda a, op: op(a, pltpu.roll(a, KV_SLOTS, axis=0))
    mx = fold(jnp.max(pool, axis=0), jnp.maximum)
    e = jnp.exp(pool - mx[None])
    w = e / fold(jnp.sum(e, axis=0), jnp.add)[None]
    first = _iota((SUBLANES, LANES), 0) < KV_SLOTS
    for k, ref in enumerate(page_refs):
        x = ref[0].reshape(BLOCKS_PER_PAGE, tiles, SUBLANES, LANES)
        sums = [fold(jnp.sum(x[b] * w, axis=0), jnp.add) for b in range(BLOCKS_PER_PAGE)]
        o_ref[0, k * SUBLANES:(k + 1) * SUBLANES, :] = jnp.where(first, sums[0], sums[1])


def pool_past(cache, layer_off, page_table, pool):
    bsz, n_pages = page_table.shape
    pps = min(PAGES_PER_STEP, n_pages)
    steps = n_pages // pps
    assert BLOCKS_PER_PAGE * KV_SLOTS == SUBLANES
    pool_rows = jnp.broadcast_to(jnp.tile(pool.T, (1, 2)).reshape(BLOCK_ROWS, 1), (BLOCK_ROWS, LANES))

    def page_spec(k):
        return pl.BlockSpec((1, PAGE_ROWS, LANES),
                            lambda b, s, pt: (layer_off + pt[b * n_pages + s * pps + k], 0, 0))

    grid_spec = pltpu.PrefetchScalarGridSpec(
        num_scalar_prefetch=1,
        grid=(bsz, steps),
        in_specs=[page_spec(k) for k in range(pps)]
        + [pl.BlockSpec((BLOCK_ROWS, LANES), lambda b, s, pt: (0, 0))],
        out_specs=pl.BlockSpec((1, pps * SUBLANES, LANES), lambda b, s, pt: (b, s, 0)),
    )
    return pl.pallas_call(
        functools.partial(_pool_past_kernel, pps=pps),
        grid_spec=grid_spec,
        out_shape=jax.ShapeDtypeStruct((bsz, n_pages * SUBLANES, LANES), F32),
        compiler_params=_cparams(("parallel", "arbitrary")),
        name="pool_past",
    )(page_table.reshape(-1), *([cache] * pps), pool_rows)


def _sample_select_kernel(q_ref, kc_ref, oc_ref, idx_ref, *, dec):
    rows = kc_ref.shape[1]
    x = kc_ref[0].astype(BF16)
    lane = _iota((dec, rows), 1)
    slot = _iota((dec, SEL_TOPN), 1)
    for h in range(KV_HEADS):
        is_key = lane % KV_SLOTS == h
        imp = jnp.zeros((dec, rows), F32)
        for g in range(NSA_GROUP):
            hd = h * NSA_GROUP + g
            q = q_ref[0, :, hd * LANES:(hd + 1) * LANES]
            s = jnp.where(is_key, _dot_nt(q, x) * ATTN_SCALE, NEG)
            e = jnp.where(is_key, jnp.exp(s - jnp.max(s, axis=-1, keepdims=True)), 0.0)
            p = e / jnp.maximum(jnp.sum(e, axis=-1, keepdims=True), TINY)
            imp = imp + p
            oc_ref[0, :, hd * LANES:(hd + 1) * LANES] = _dot(pltpu.roll(p, KV_HEADS, axis=1).astype(BF16), x)
        score = jnp.where(is_key & (lane >= KV_SLOTS), imp, -jnp.inf)
        picked = jnp.zeros((dec, SEL_TOPN), jnp.int32)
        for r in range(SEL_TOPN - 2):
            best = jnp.max(score, axis=-1, keepdims=True)
            arg = jnp.min(jnp.where(score == best, lane, rows), axis=-1, keepdims=True)
            picked = jnp.where(slot == r + 1, arg // KV_SLOTS, picked)
            score = jnp.where(lane == arg, -jnp.inf, score)
        idx_ref[0, h * dec:(h + 1) * dec, :] = picked


def sample_select(qn, kcmp_past):
    bsz, dec, _ = qn.shape
    rows = kcmp_past.shape[1]
    return pl.pallas_call(
        functools.partial(_sample_select_kernel, dec=dec),
        grid=(bsz,),
        in_specs=[pl.BlockSpec((1, dec, B_WIDTH), lambda b: (b, 0, 0)),
                  pl.BlockSpec((1, rows, LANES), lambda b: (b, 0, 0))],
        out_specs=[pl.BlockSpec((1, dec, B_WIDTH), lambda b: (b, 0, 0)),
                   pl.BlockSpec((1, KV_HEADS * dec, SEL_TOPN), lambda b: (b, 0, 0))],
        out_shape=[jax.ShapeDtypeStruct((bsz, dec, B_WIDTH), F32),
                   jax.ShapeDtypeStruct((bsz, KV_HEADS * dec, SEL_TOPN), jnp.int32)],
        compiler_params=_cparams(("parallel",)),
        name="sample_select",
    )(qn, kcmp_past)


def _rows_to_tile(rows):
    r = _iota((SUBLANES, rows[0].shape[1]), 0)
    out = jnp.zeros((SUBLANES, rows[0].shape[1]), rows[0].dtype)
    for i, x in enumerate(rows):
        out = jnp.where(r == i, jnp.broadcast_to(x, out.shape), out)
    return out


def _sample_attn_kernel(idx_ref, pt_ref, *refs, dec):
    n = SEL_TOPN
    blk_refs = refs[:n]
    (qr_ref, ksn_ref, vsn_ref, wbuf_ref, kwn_ref, vwn_ref, gate_ref, oc_ref, o_ref) = refs[n:]
    h = pl.program_id(1)
    qi = pl.program_id(2)
    q8 = _rows_to_tile([qr_ref[0, 0, :, g * LANES:(g + 1) * LANES].astype(F32)
                        for g in range(NSA_GROUP)]).astype(BF16)
    newpos = _iota((1, dec), 1)

    def attend(x_old, mask_old, k_new, v_new):
        s_old = jnp.where(mask_old, _dot_nt(q8, x_old) * ATTN_SCALE, NEG)
        mask_new = newpos <= qi
        s_new = jnp.where(mask_new, _dot_nt(q8, k_new) * ATTN_SCALE, NEG)
        m = jnp.maximum(jnp.max(s_old, axis=-1, keepdims=True), jnp.max(s_new, axis=-1, keepdims=True))
        p_old = jnp.where(mask_old, jnp.exp(s_old - m), 0.0)
        p_new = jnp.where(mask_new, jnp.exp(s_new - m), 0.0)
        l = jnp.sum(p_old, axis=-1, keepdims=True) + jnp.sum(p_new, axis=-1, keepdims=True)
        pv = _dot(pltpu.roll(p_old, KV_HEADS, axis=1).astype(BF16), x_old)
        return (pv + _dot(p_new.astype(BF16), v_new)) / l

    x_sel = jnp.concatenate([r[0].astype(BF16) for r in blk_refs], axis=0)
    lane = _iota((1, n * BLOCK_ROWS), 1)
    mask_sel = (lane % KV_SLOTS == h) & (lane < (n - 1) * BLOCK_ROWS)
    o_sel = attend(x_sel, mask_sel, ksn_ref[0].astype(BF16), vsn_ref[0].astype(BF16))

    wrows = wbuf_ref.shape[1]
    wb = wrows // KV_SLOTS
    lane_w = _iota((1, wrows), 1)
    mask_win = (lane_w % KV_SLOTS == h) & (lane_w // KV_SLOTS >= qi + (wb - WINDOW))
    o_win = attend(wbuf_ref[0].astype(BF16), mask_win, kwn_ref[0].astype(BF16), vwn_ref[0].astype(BF16))

    gate = _kv_gates(gate_ref.at[0, 0], h)
    outs = []
    for g in range(NSA_GROUP):
        outs.append(gate[:, 3 * g:3 * g + 1] * oc_ref[0, 0, :, g * LANES:(g + 1) * LANES]
                    + gate[:, 3 * g + 1:3 * g + 2] * o_sel[g:g + 1]
                    + gate[:, 3 * g + 2:3 * g + 3] * o_win[g:g + 1])
    o_ref[0, 0] = jnp.concatenate(outs, axis=1).astype(o_ref.dtype)


def sample_attn(idx, page_table, cache_sel, sel_off, cache_win, win_off, qr, rows_s, rows_w, gate, o_cmp):
    bsz, dec, _ = rows_s.shape
    n_pages = page_table.shape[1]
    grp = NSA_GROUP * LANES

    def phys(b, h, q, k, idx_r, pt_r):
        blk = idx_r[((b * KV_HEADS + h) * dec + q) * SEL_TOPN + k]
        return (sel_off + pt_r[b * n_pages + blk // BLOCKS_PER_PAGE]) * BLOCKS_PER_PAGE + blk % BLOCKS_PER_PAGE

    def bspec(k):
        return pl.BlockSpec((1, BLOCK_ROWS, LANES),
                            lambda b, h, q, idx_r, pt_r: (phys(b, h, q, k, idx_r, pt_r), 0, 0))

    qmap = lambda b, h, q, idx_r, pt_r: (b, q, 0, h)
    newk = pl.BlockSpec((1, dec, LANES), lambda b, h, q, idx_r, pt_r: (b, 0, h))
    newv = pl.BlockSpec((1, dec, LANES), lambda b, h, q, idx_r, pt_r: (b, 0, KV_HEADS + h))
    wbuf = pl.BlockSpec((1, cache_win.shape[1], LANES), lambda b, h, q, idx_r, pt_r: (win_off + b, 0, 0))
    grid_spec = pltpu.PrefetchScalarGridSpec(
        num_scalar_prefetch=2,
        grid=(bsz, KV_HEADS, dec),
        in_specs=[bspec(k) for k in range(SEL_TOPN)]
        + [pl.BlockSpec((1, 1, 1, grp), qmap), newk, newv, wbuf, newk, newv,
           pl.BlockSpec((1, 1, 1, LANES), lambda b, h, q, idx_r, pt_r: (b, q, 0, 0)),
           pl.BlockSpec((1, 1, 1, grp), qmap)],
        out_specs=pl.BlockSpec((1, 1, 1, grp), qmap),
    )
    q4 = lambda a: a.reshape(bsz, dec, 1, a.shape[-1])
    return pl.pallas_call(
        functools.partial(_sample_attn_kernel, dec=dec),
        grid_spec=grid_spec,
        out_shape=jax.ShapeDtypeStruct((bsz, dec, 1, B_WIDTH), BF16),
        compiler_params=_cparams(("parallel", "parallel", "arbitrary")),
        name="sample_attn",
    )(idx.reshape(-1), page_table.reshape(-1), *([cache_sel] * SEL_TOPN),
      q4(qr), rows_s, rows_s, cache_win, rows_w, rows_w, q4(gate), q4(o_cmp))


def _shift_rows(x, prev, k):
    xr = pltpu.roll(x, k, axis=0)
    pr = pltpu.roll(prev, k, axis=0)
    top = jnp.where(_iota((SUBLANES, 1), 0) < k, pr, xr[:SUBLANES])
    return jnp.concatenate([top, xr[SUBLANES:]], axis=0)


SSM_PREP_ROWS = 512
SSM_PREP_COLS = 1024


def _ssm_prep_kernel(x_ref, w_ref, b_ref, o_ref, tail_ref, carry_ref):
    @pl.when(pl.program_id(2) == 0)
    def _():
        carry_ref[...] = jnp.zeros_like(carry_ref)

    x = x_ref[...]
    prev = carry_ref[...]
    y = x * w_ref[0, SSM_CONV - 1:SSM_CONV, :] + b_ref[0]
    for k in range(1, SSM_CONV):
        y = y + _shift_rows(x, prev, k) * w_ref[0, SSM_CONV - 1 - k:SSM_CONV - k, :]
    o_ref[...] = y * jax.nn.sigmoid(y)
    carry_ref[...] = x[x.shape[0] - SUBLANES:, :]
    tail_ref[0] = carry_ref[...]


def ssm_prep(proj, col0, conv_w, conv_b, layer, bsz, seq):
    tr, tc = SSM_PREP_ROWS, SSM_PREP_COLS
    nt = seq // tr
    c0 = col0 // tc
    return pl.pallas_call(
        _ssm_prep_kernel,
        grid=(SSM_CONV_DIM // tc, bsz, nt),
        in_specs=[pl.BlockSpec((tr, tc), lambda c, b, i: (b * nt + i, c0 + c)),
                  pl.BlockSpec((1, SSM_CONV, tc), lambda c, b, i: (layer, 0, c)),
                  pl.BlockSpec((1, 1, tc), lambda c, b, i: (layer, 0, c))],
        out_specs=[pl.BlockSpec((tr, tc), lambda c, b, i: (b * nt + i, c)),
                   pl.BlockSpec((1, SUBLANES, tc), lambda c, b, i: (b, 0, c))],
        out_shape=[jax.ShapeDtypeStruct((bsz * seq, SSM_CONV_DIM), F32),
                   jax.ShapeDtypeStruct((bsz, SUBLANES, SSM_CONV_DIM), F32)],
        scratch_shapes=[pltpu.VMEM((SUBLANES, tc), F32)],
        compiler_params=_cparams(("parallel", "parallel", "arbitrary")),
        name="ssm_prep",
    )(proj, conv_w, conv_b.reshape(conv_b.shape[0], 1, -1))


def _taps_kernel(*refs, width, act):
    x_refs, w_ref, b_ref, o_ref = refs[:width], refs[width], refs[width + 1], refs[width + 2]
    y = b_ref[...] + x_refs[0][...] * w_ref[0:1, :]
    for k in range(1, width):
        y = y + x_refs[k][...] * w_ref[k:k + 1, :]
    if act == "silu":
        o_ref[...] = y * jax.nn.sigmoid(y)
    else:
        o_ref[...] = y


def conv_taps(views, w, b, *, act, tile):
    width = len(views)
    m, n = views[0].shape
    col = pl.BlockSpec((m, tile), lambda j: (0, j))
    return pl.pallas_call(
        functools.partial(_taps_kernel, width=width, act=act),
        grid=(n // tile,),
        in_specs=[col] * width + [pl.BlockSpec((width, tile), lambda j: (0, j)),
                                  pl.BlockSpec((1, tile), lambda j: (0, j))],
        out_specs=col,
        out_shape=jax.ShapeDtypeStruct((m, n), F32),
        compiler_params=_cparams(("parallel",)),
        name="conv_taps_" + act,
    )(*views, w, b.reshape(1, -1))


def _glu_kernel(g_ref, u_ref, o_ref):
    g = g_ref[...]
    o_ref[...] = (g * jax.nn.sigmoid(g) * u_ref[...]).astype(o_ref.dtype)


def glu(hc, half, tile):
    m = hc.shape[0]
    nt = half // tile
    return pl.pallas_call(
        _glu_kernel,
        grid=(nt,),
        in_specs=[pl.BlockSpec((m, tile), lambda j: (0, j)), pl.BlockSpec((m, tile), lambda j: (0, nt + j))],
        out_specs=pl.BlockSpec((m, tile), lambda j: (0, j)),
        out_shape=jax.ShapeDtypeStruct((m, half), BF16),
        compiler_params=_cparams(("parallel",)),
        name="glu",
    )(hc, hc)


SSD_GPS = 4


def _ssd_kernel(xs_ref, bm_ref, cm_ref, z_ref, dt_ref, dtt_ref, al_ref, alt_ref, dsk_ref, ng_ref, h0_ref,
                y_ref, ht_ref, st_ref, *, n_chunks):
    c = pl.program_id(2)
    q = SSM_CHUNK
    e_heads = SSM_GROUP_HEADS
    gw = SSM_GROUP_WIDTH

    @pl.when(c == 0)
    def _():
        st_ref[...] = h0_ref[0]

    ri = _iota((q, q), 0)
    ci = _iota((q, q), 1)
    tril = ri >= ci
    tril_b = tril.astype(BF16)
    triu_b = (ri <= ci).astype(BF16)
    expand = (_iota((e_heads, gw), 0) == _iota((e_heads, gw), 1) // SSM_HEAD_DIM).astype(BF16)
    lane = _iota((q, 2 * SSM_HEAD_DIM), 1)

    for k in range(SSD_GPS):
        gs = slice(k * gw, (k + 1) * gw)
        ns = slice(k * SSM_STATE, (k + 1) * SSM_STATE)
        dt = dt_ref[k]
        da = dt * (-jnp.exp(al_ref[k]))
        da_t = dtt_ref[k] * (-jnp.exp(alt_ref[k]))
        acs = _dot3_r(tril_b, da)
        acs_t = _dot3_l(da_t, triu_b)
        tot = acs[q - 1:q, :]

        xs = xs_ref[:, gs]
        xdt = xs * _dot2_l(dt, expand)
        xdt_b = xdt.astype(BF16)
        bm = bm_ref[:, ns]
        cm_b = cm_ref[:, ns].astype(BF16)
        cb = _dot_nt(cm_b, bm.astype(BF16))

        y_parts = []
        for pair in range(e_heads // 2):
            cols = slice(pair * 2 * SSM_HEAD_DIM, (pair + 1) * 2 * SSM_HEAD_DIM)
            res = []
            for e in (2 * pair, 2 * pair + 1):
                seg = jnp.where(tril, acs[:, e:e + 1] - acs_t[e:e + 1, :], NEG)
                mix = (cb * jnp.exp(seg)).astype(BF16)
                res.append(_dot(mix, xdt_b[:, cols]))
            y_parts.append(jnp.where(lane < SSM_HEAD_DIM, res[0], res[1]))
        y = jnp.concatenate(y_parts, axis=1)

        state = st_ref[:, gs]
        ex = _dot2_l(jnp.concatenate([jnp.exp(acs), jnp.exp(tot - acs)], axis=0), expand)
        y = y + _dot(cm_b, state.astype(BF16)) * ex[0:q]
        contrib = _dot(bm.T.astype(BF16), (xdt * ex[q:2 * q]).astype(BF16))
        st_ref[:, gs] = state * ex[q - 1:q] + contrib

        y = y + xs * dsk_ref[:, gs]
        z = z_ref[:, gs]
        y = y * (z * jax.nn.sigmoid(z))
        ms = jnp.mean(y * y, axis=-1, keepdims=True)
        y_ref[:, gs] = (y * lax.rsqrt(ms + EPS) * ng_ref[:, gs]).astype(y_ref.dtype)

    @pl.when(c == n_chunks - 1)
    def _():
        ht_ref[0] = st_ref[...]


def ssd(xbc, zsrc, dt, a_log, d_skip, norm_g, h0, bsz, seq):
    nc = seq // SSM_CHUNK
    gps = SSD_GPS
    gw = SSM_GROUP_WIDTH * gps
    nw = SSM_STATE * gps
    e = SSM_GROUP_HEADS
    m = bsz * seq
    dt_g = dt.reshape(m, SSM_GROUPS, e).transpose(1, 0, 2)
    dt_t = dt_g.transpose(0, 2, 1)
    al = a_log.astype(F32).reshape(SSM_GROUPS, 1, e)
    al_t = a_log.astype(F32).reshape(SSM_GROUPS, e, 1)
    dsk = jnp.repeat(d_skip.astype(F32), SSM_HEAD_DIM).reshape(1, SSM_INNER)
    b_off = SSM_INNER // nw
    c_off = b_off + SSM_GROUPS // gps
    rowmap = lambda col: (lambda b, g, c: (b * nc + c, col(g)))
    return pl.pallas_call(
        functools.partial(_ssd_kernel, n_chunks=nc),
        grid=(bsz, SSM_GROUPS // gps, nc),
        in_specs=[
            pl.BlockSpec((SSM_CHUNK, gw), rowmap(lambda g: g)),
            pl.BlockSpec((SSM_CHUNK, nw), rowmap(lambda g: b_off + g)),
            pl.BlockSpec((SSM_CHUNK, nw), rowmap(lambda g: c_off + g)),
            pl.BlockSpec((SSM_CHUNK, gw), rowmap(lambda g: g)),
            pl.BlockSpec((gps, SSM_CHUNK, e), lambda b, g, c: (g, b * nc + c, 0)),
            pl.BlockSpec((gps, e, SSM_CHUNK), lambda b, g, c: (g, 0, b * nc + c)),
            pl.BlockSpec((gps, 1, e), lambda b, g, c: (g, 0, 0)),
            pl.BlockSpec((gps, e, 1), lambda b, g, c: (g, 0, 0)),
            pl.BlockSpec((1, gw), lambda b, g, c: (0, g)),
            pl.BlockSpec((1, gw), lambda b, g, c: (0, g)),
            pl.BlockSpec((1, SSM_STATE, gw), lambda b, g, c: (b, 0, g)),
        ],
        out_specs=[pl.BlockSpec((SSM_CHUNK, gw), rowmap(lambda g: g)),
                   pl.BlockSpec((1, SSM_STATE, gw), lambda b, g, c: (b, 0, g))],
        out_shape=[jax.ShapeDtypeStruct((m, SSM_INNER), BF16),
                   jax.ShapeDtypeStruct((bsz, SSM_STATE, SSM_INNER), F32)],
        scratch_shapes=[pltpu.VMEM((SSM_STATE, gw), F32)],
        compiler_params=_cparams(("parallel", "parallel", "arbitrary")),
        name="ssd",
    )(xbc, xbc, xbc, zsrc, dt_g, dt_t, al, al_t, dsk, norm_g.astype(F32).reshape(1, SSM_INNER), h0)


def _state_to_cols(h):
    b = h.shape[0]
    return h.transpose(0, 3, 1, 2).reshape(b, SSM_STATE, SSM_INNER)


def _state_from_cols(s):
    b = s.shape[0]
    return s.reshape(b, SSM_STATE, SSM_HEADS, SSM_HEAD_DIM).transpose(0, 2, 3, 1)


FFN_BM = 1024
FFN_SUB = 128
FFN_TILE = 512
FFN_TILES = D_FF // FFN_TILE


def _conv3(h_ref, cw_ref, cb_ref):
    lo = SUBLANES
    return (h_ref[lo - 2:lo - 2 + FFN_SUB, :] * cw_ref[0, 0:1, :] + h_ref[lo - 1:lo - 1 + FFN_SUB, :] * cw_ref[0, 1:2, :]
            + h_ref[lo:lo + FFN_SUB, :] * cw_ref[0, 2:3, :] + cb_ref[0])


def _ffn_up_kernel(x_ref, xs_ref, wg_ref, wu_ref, cwg_ref, cwu_ref, cbg_ref, cbu_ref,
                   act_ref, tg_ref, tu_ref, sg_ref, su_ref, wgb_ref, wub_ref, hg_ref, hu_ref, *, tiles_per_seq):
    i = pl.program_id(1)
    hist = slice(0, SUBLANES)
    body = slice(SUBLANES, SUBLANES + FFN_SUB)

    @pl.when(i == 0)
    def _():
        wgb_ref[...] = wg_ref[0].astype(BF16)
        wub_ref[...] = wu_ref[0].astype(BF16)
        sg_ref[...] = _dot(xs_ref[...], wgb_ref[...])
        su_ref[...] = _dot(xs_ref[...], wub_ref[...])

    @pl.when(i % tiles_per_seq == 0)
    def _():
        hg_ref[hist, :] = jnp.zeros((SUBLANES, FFN_TILE), F32)
        hu_ref[hist, :] = jnp.zeros((SUBLANES, FFN_TILE), F32)

    for r in range(FFN_BM // FFN_SUB):
        rs = slice(r * FFN_SUB, (r + 1) * FFN_SUB)
        x = x_ref[rs, :]
        hg_ref[body, :] = _dot(x, wgb_ref[...])
        hu_ref[body, :] = _dot(x, wub_ref[...])
        g = _conv3(hg_ref, cwg_ref, cbg_ref)
        u = _conv3(hu_ref, cwu_ref, cbu_ref)
        act_ref[rs, :] = (g * jax.nn.sigmoid(g) * u).astype(act_ref.dtype)
        hg_ref[hist, :] = hg_ref[FFN_SUB:FFN_SUB + SUBLANES, :]
        hu_ref[hist, :] = hu_ref[FFN_SUB:FFN_SUB + SUBLANES, :]
    tg_ref[0] = hg_ref[hist, :]
    tu_ref[0] = hu_ref[hist, :]


def ffn_up(xp, xs, w_up, conv_w, conv_b, layer, bsz, seq):
    m, k = xp.shape
    ms = xs.shape[0]
    tiles_per_seq = seq // FFN_BM
    nt = FFN_TILES
    wspec = lambda off: pl.BlockSpec((1, k, FFN_TILE), lambda j, i: (layer, 0, off + j))
    cwspec = lambda off: pl.BlockSpec((1, FFN_CONV, FFN_TILE), lambda j, i: (layer, 0, off + j))
    cbspec = lambda off: pl.BlockSpec((1, 1, FFN_TILE), lambda j, i: (layer, 0, off + j))
    tail = pl.BlockSpec((1, SUBLANES, FFN_TILE), lambda j, i: (i // tiles_per_seq, 0, j))
    samp = pl.BlockSpec((ms, FFN_TILE), lambda j, i: (0, j))
    cb3 = conv_b.reshape(conv_b.shape[0], 1, -1)
    raw = pltpu.VMEM((SUBLANES + FFN_SUB, FFN_TILE), F32)
    return pl.pallas_call(
        functools.partial(_ffn_up_kernel, tiles_per_seq=tiles_per_seq),
        grid=(nt, m // FFN_BM),
        in_specs=[pl.BlockSpec((FFN_BM, k), lambda j, i: (i, 0)), pl.BlockSpec((ms, k), lambda j, i: (0, 0)),
                  wspec(0), wspec(nt), cwspec(0), cwspec(nt), cbspec(0), cbspec(nt)],
        out_specs=[pl.BlockSpec((FFN_BM, FFN_TILE), lambda j, i: (i, j)), tail, tail, samp, samp],
        out_shape=[jax.ShapeDtypeStruct((m, D_FF), BF16),
                   jax.ShapeDtypeStruct((bsz, SUBLANES, D_FF), F32), jax.ShapeDtypeStruct((bsz, SUBLANES, D_FF), F32),
                   jax.ShapeDtypeStruct((ms, D_FF), F32), jax.ShapeDtypeStruct((ms, D_FF), F32)],
        scratch_shapes=[pltpu.VMEM((k, FFN_TILE), BF16), pltpu.VMEM((k, FFN_TILE), BF16), raw, raw],
        compiler_params=_cparams(("parallel", "arbitrary")),
        name="ffn_up",
    )(xp, xs, w_up, w_up, conv_w, conv_w, cb3, cb3)


def _rope_tables(pos):
    half = HEAD_DIM // 2
    inv = ROPE_THETA ** (-jnp.arange(half, dtype=F32) / half)
    ang = pos.astype(F32)[:, None] * inv[None, :]
    cos, sin = jnp.cos(ang), jnp.sin(ang)
    return jnp.concatenate([cos, cos], axis=1), jnp.concatenate([-sin, sin], axis=1)


def _kv_out(rows, bsz):
    return rows.reshape(bsz, -1, 2, KV_HEADS, HEAD_DIM)


def kernel(x_prompt, x_sample, cache_kv_cmp, cache_kv_sel, cache_kv_win, state_ssm_conv, state_ssm, state_ffn_conv, page_table, norm_mix, norm_ffn, w_in_even, w_out_even, gmlp_v_norm, gmlp_ws, gmlp_bs, q_norm, k_norm, cmp_pool, w_in_odd, ssm_conv_w, ssm_conv_b, ssm_dt_bias, ssm_a_log, ssm_d, ssm_norm, w_out_odd, ffn_w_up, ffn_conv_w, ffn_conv_b, ffn_w_down):
    bp, sp, _ = x_prompt.shape
    bs, ss, _ = x_sample.shape
    depth = norm_mix.shape[0]
    n_pool = cache_kv_cmp.shape[1]
    n_pages = page_table.shape[1]
    past_len = n_pages * PAGE_SIZE
    wb = cache_kv_win.shape[2]
    mp, ms = bp * sp, bs * ss

    cos_p, sin_p = _rope_tables(jnp.arange(sp, dtype=jnp.int32))
    cos_p, sin_p = jnp.tile(cos_p, (bp, 1)), jnp.tile(sin_p, (bp, 1))
    cos_s, sin_s = _rope_tables(past_len + jnp.arange(ss, dtype=jnp.int32))
    cos_s, sin_s = jnp.tile(cos_s, (bs, 1)), jnp.tile(sin_s, (bs, 1))

    cache_c = cache_kv_cmp.reshape(-1, PAGE_ROWS, LANES)
    cache_s = cache_kv_sel.reshape(-1, BLOCK_ROWS, LANES)
    cache_w = cache_kv_win.reshape(-1, wb * KV_SLOTS, LANES)

    w_down16 = ffn_w_down.astype(BF16)
    w_out_odd16 = w_out_odd.astype(BF16)
    w_in_even_t = jnp.swapaxes(w_in_even, 1, 2)
    w_in_odd_t = jnp.swapaxes(w_in_odd, 1, 2)

    hp = x_prompt.reshape(mp, D_MODEL)
    hs = x_sample.reshape(ms, D_MODEL)
    outs = {k: [] for k in ("p_cmp", "p_sel", "p_win", "s_cmp", "s_sel", "s_win", "s_v",
                            "p_sconv", "p_sst", "s_sconv", "s_sst", "p_fconv", "s_fconv")}

    for layer in range(depth):
        i = layer // 2
        xp = rms_cast(hp, norm_mix[layer])
        xs = rms_cast(hs, norm_mix[layer])
        if layer % 2 == 0:
            w_gate = w_in_even_t[i, E_MAIN:]
            proj_p, proj_s = matmul_ws([xp], [xs], w_in_even_t, i, E_MAIN, bn=768, w_t=True, name="even_in")
            gl = matmul_narrow(xp, w_gate)
            u, v, qn, qr, rc, rs, rw, gate, ks16, kw16, vts16, vtw16 = even_prep(
                proj_p, gl, cos_p, sin_p, gmlp_v_norm[i], q_norm[i], k_norm[i], for_mxu=True)
            a_p = gmlp(u, v, gmlp_ws[i], gmlp_bs[i])
            kcmp = compress(rc, cmp_pool[i])
            b_p = nsa_prompt(qn, qr, kcmp, ks16, vts16, kw16, vtw16, gate, bp, sp)
            outs["p_cmp"].append(_kv_out(rc, bp))
            outs["p_sel"].append(_kv_out(rs, bp))
            outs["p_win"].append(_kv_out(rw, bp)[:, sp - min(WINDOW, sp):])
            gl = matmul_narrow(xs, w_gate)
            u, v, qn, qr, rc, rs, rw, gate = even_prep(
                proj_s, gl, cos_s, sin_s, gmlp_v_norm[i], q_norm[i], k_norm[i], for_mxu=False)
            lpad = ((0, 0), (0, A_CHUNK - ss), (0, 0))
            a_s = gmlp(jnp.pad(u.reshape(bs, ss, -1), lpad).reshape(bs * A_CHUNK, -1),
                       jnp.pad(v.reshape(bs, ss, -1), lpad).reshape(bs * A_CHUNK, -1),
                       gmlp_ws[i], gmlp_bs[i]).reshape(bs, A_CHUNK, -1)[:, :ss].reshape(ms, -1)
            kc_past = pool_past(cache_c, i * n_pool, page_table, cmp_pool[i])
            o_cmp, idx = sample_select(qn.reshape(bs, ss, -1), kc_past)
            b_s = sample_attn(idx, page_table, cache_s, i * n_pool, cache_w, i * bs,
                              qr.reshape(bs, ss, -1), rs.reshape(bs, ss, -1), rw.reshape(bs, ss, -1),
                              gate.reshape(bs, ss, -1), o_cmp).reshape(ms, B_WIDTH)
            outs["s_cmp"].append(_kv_out(rc, bs))
            outs["s_sel"].append(_kv_out(rs, bs))
            outs["s_win"].append(_kv_out(rw, bs))
            outs["s_v"].append(v.reshape(bs, ss, A_WIDTH))
            hp, hs = matmul_ws([a_p, b_p], [a_s, b_s], w_out_even, i, D_MODEL, bn=1024, res=(hp, hs), name="even_out")
        else:
            zx = SSM_INNER + SSM_CONV_DIM
            w_dt = w_in_odd_t[i, zx:]
            proj_p, proj_s = matmul_ws([xp], [xs], w_in_odd_t, i, zx, bn=1024, w_t=True, name="odd_in")
            dt = matmul_narrow(xp, w_dt, ssm_dt_bias[i])[:, :SSM_HEADS]
            xbc, tail = ssm_prep(proj_p, SSM_INNER, ssm_conv_w, ssm_conv_b, i, bp, sp)
            h0 = jnp.zeros((bp, SSM_STATE, SSM_INNER), F32)
            y, ht = ssd(xbc, proj_p, dt, ssm_a_log[i], ssm_d[i], ssm_norm[i], h0, bp, sp)
            hp = matmul(y, w_out_odd16, layer=i, bn=512, res=hp, name="odd_out")
            outs["p_sconv"].append(tail[:, SUBLANES - (SSM_CONV - 1):])
            outs["p_sst"].append(_state_from_cols(ht))
            dt = matmul_narrow(xs, w_dt, ssm_dt_bias[i])[:, :SSM_HEADS]
            xin = jnp.concatenate([state_ssm_conv[i], proj_s[:, SSM_INNER:].reshape(bs, ss, -1)], axis=1)
            views = [xin[:, k:k + ss].reshape(ms, -1) for k in range(SSM_CONV)]
            xbc = conv_taps(views, ssm_conv_w[i], ssm_conv_b[i], act="silu", tile=1024)
            cpad = ((0, 0), (0, SSM_CHUNK - ss), (0, 0))
            padrows = lambda a: jnp.pad(a.reshape(bs, ss, -1), cpad).reshape(bs * SSM_CHUNK, -1)
            y, ht = ssd(padrows(xbc), padrows(proj_s[:, :SSM_INNER]), padrows(dt), ssm_a_log[i], ssm_d[i], ssm_norm[i],
                        _state_to_cols(state_ssm[i].astype(F32)), bs, SSM_CHUNK)
            y = y.reshape(bs, SSM_CHUNK, -1)[:, :ss].reshape(ms, -1)
            hs = matmul(y, w_out_odd16, layer=i, bn=1024, res=hs, name="odd_out_s")
            outs["s_sconv"].append(xin[:, ss:])
            outs["s_sst"].append(_state_from_cols(ht))
        xp = rms_cast(hp, norm_ffn[layer])
        xs = rms_cast(hs, norm_ffn[layer])
        act, tail_g, tail_u, hu_g, hu_u = ffn_up(xp, xs, ffn_w_up, ffn_conv_w, ffn_conv_b, layer, bp, sp)
        hp = matmul(act, w_down16, layer=layer, bn=512, res=hp, name="ffn_down")
        outs["p_fconv"].append(jnp.concatenate([tail_g, tail_u], axis=2)[:, SUBLANES - (FFN_CONV - 1):])
        hu = jnp.concatenate([hu_g, hu_u], axis=1)
        xin = jnp.concatenate([state_ffn_conv[layer], hu.reshape(bs, ss, -1)], axis=1)
        views = [xin[:, k:k + ss].reshape(ms, -1) for k in range(FFN_CONV)]
        hc = conv_taps(views, ffn_conv_w[layer], ffn_conv_b[layer], act="none", tile=1024)
        hs = matmul(glu(hc, D_FF, 512), w_down16, layer=layer, bn=512, res=hs, name="ffn_down_s")
        outs["s_fconv"].append(xin[:, ss:])

    st = lambda k: jnp.stack(outs[k])
    return (hp.reshape(bp, sp, D_MODEL), hs.reshape(bs, ss, D_MODEL), st("p_cmp"), st("p_sel"), st("p_win"),
            st("p_sconv"), st("p_sst"), st("p_fconv"), st("s_cmp"), st("s_sel"), st("s_win"), st("s_v"),
            st("s_sconv"), st("s_sst"), st("s_fconv"))
```

```python
import functools

import jax
import jax.numpy as jnp
from jax import lax
from jax.experimental import pallas as pl
from jax.experimental.pallas import tpu as pltpu

F32 = jnp.float32
BF16 = jnp.bfloat16

D_MODEL = 2048
A_WIDTH = D_MODEL // 2
A_GROUPS = 8
A_CHUNK = 128
NSA_HEADS = 8
HEAD_DIM = 128
KV_HEADS = 2
NSA_GROUP = NSA_HEADS // KV_HEADS
B_WIDTH = NSA_HEADS * HEAD_DIM
KV_COLS = 2 * KV_HEADS * HEAD_DIM
KV_SLOTS = 2 * KV_HEADS
CMP_BLOCK = 64
SEL_TOPN = 16
WINDOW = 512
PAGE_SIZE = 128
ROPE_THETA = 10000.0
ATTN_SCALE = HEAD_DIM ** -0.5
LOG2E = 1.4426950408889634
SSM_INNER = 2 * D_MODEL
SSM_HEAD_DIM = 64
SSM_HEADS = SSM_INNER // SSM_HEAD_DIM
SSM_GROUPS = 8
SSM_GROUP_HEADS = SSM_HEADS // SSM_GROUPS
SSM_GROUP_WIDTH = SSM_INNER // SSM_GROUPS
SSM_STATE = 128
SSM_CONV = 4
SSM_CHUNK = 128
SSM_CONV_DIM = SSM_INNER + 2 * SSM_GROUPS * SSM_STATE
D_FF = 5632
FFN_CONV = 3
EPS = 1e-6
NEG = -1e30
FORCE = 1e4
TINY = 1e-30

VMEM_LIMIT_BYTES = 56 * 1024 * 1024
LANES = 128
SUBLANES = 8

E_MAIN = 2 * A_WIDTH + B_WIDTH + 3 * KV_COLS
N_GATES = 3 * NSA_HEADS
GATES_PER_KV = 3 * NSA_GROUP


def _cparams(sem):
    return pltpu.CompilerParams(dimension_semantics=sem, vmem_limit_bytes=VMEM_LIMIT_BYTES)


def _dot(a, b):
    return jnp.dot(a, b, preferred_element_type=F32)


def _dot_nt(a, b):
    return lax.dot_general(a, b, (((1,), (1,)), ((), ())), preferred_element_type=F32)


def _iota(shape, dim):
    return lax.broadcasted_iota(jnp.int32, shape, dim)


def _split3(x):
    hi = x.astype(BF16)
    r1 = x - hi.astype(F32)
    mid = r1.astype(BF16)
    lo = (r1 - mid.astype(F32)).astype(BF16)
    return hi, mid, lo


def _dot3_l(x, m):
    hi, mid, lo = _split3(x)
    return _dot(hi, m) + _dot(mid, m) + _dot(lo, m)


def _dot3_r(m, x):
    hi, mid, lo = _split3(x)
    return _dot(m, hi) + _dot(m, mid) + _dot(m, lo)


def _dot2_l(x, m):
    hi = x.astype(BF16)
    return _dot(hi, m) + _dot((x - hi.astype(F32)).astype(BF16), m)


def _rms_kernel(x_ref, g_ref, o_ref):
    x = x_ref[...]
    ms = jnp.mean(x * x, axis=-1, keepdims=True)
    o_ref[...] = (x * lax.rsqrt(ms + EPS) * g_ref[...]).astype(o_ref.dtype)


def rms_cast(x, g):
    m, d = x.shape
    tr = min(m, 512)
    return pl.pallas_call(
        _rms_kernel,
        grid=(m // tr,),
        in_specs=[pl.BlockSpec((tr, d), lambda i: (i, 0)), pl.BlockSpec((1, d), lambda i: (0, 0))],
        out_specs=pl.BlockSpec((tr, d), lambda i: (i, 0)),
        out_shape=jax.ShapeDtypeStruct((m, d), BF16),
        compiler_params=_cparams(("parallel",)),
        name="rms_cast",
    )(x, g.reshape(1, d))


def _mm_kernel(x_ref, w_ref, o_ref):
    o_ref[...] = _dot(x_ref[...], w_ref[0]).astype(o_ref.dtype)


def _mm_res_kernel(x_ref, w_ref, r_ref, o_ref):
    o_ref[...] = (r_ref[...] + _dot(x_ref[...], w_ref[0])).astype(o_ref.dtype)


def matmul(x, w, *, bn, layer=0, res=None, out_dtype=F32, name="mm"):
    m, k = x.shape
    if w.ndim == 2:
        w = w[None]
    n = w.shape[2]
    bm = min(m, 1024)
    in_specs = [pl.BlockSpec((bm, k), lambda i, j: (i, 0)), pl.BlockSpec((1, k, bn), lambda i, j: (layer, 0, j))]
    args = [x, w]
    kern = _mm_kernel
    if res is not None:
        in_specs.append(pl.BlockSpec((bm, bn), lambda i, j: (i, j)))
        args.append(res)
        kern = _mm_res_kernel
    return pl.pallas_call(
        kern,
        grid=(m // bm, n // bn),
        in_specs=in_specs,
        out_specs=pl.BlockSpec((bm, bn), lambda i, j: (i, j)),
        out_shape=jax.ShapeDtypeStruct((m, n), out_dtype),
        compiler_params=_cparams(("parallel", "parallel")),
        name=name,
    )(*args)


WS_BM = 1024


def _mm_ws_kernel(*refs, n_parts, has_res, w_t):
    xp, xs, w_ref = refs[:n_parts], refs[n_parts:2 * n_parts], refs[2 * n_parts]
    pos = 2 * n_parts + 1
    if has_res:
        r_ref, rs_ref = refs[pos], refs[pos + 1]
        pos += 2
    o_ref, os_ref, wb_ref = refs[pos], refs[pos + 1], refs[pos + 2]

    def mm(parts):
        acc, k0 = None, 0
        for p in parts:
            kp = p.shape[1]
            t = _dot(p[...], wb_ref[k0:k0 + kp, :])
            acc = t if acc is None else acc + t
            k0 += kp
        return acc

    @pl.when(pl.program_id(1) == 0)
    def _():
        wb_ref[...] = (w_ref[0].T if w_t else w_ref[0]).astype(BF16)
        ys = mm(xs)
        if has_res:
            ys = rs_ref[...] + ys
        os_ref[...] = ys.astype(os_ref.dtype)

    y = mm(xp)
    if has_res:
        y = r_ref[...] + y
    o_ref[...] = y.astype(o_ref.dtype)


def matmul_ws(xp_parts, xs_parts, w, layer, n_cols, *, bn, w_t=False, res=None, name="mm_ws"):
    mp = xp_parts[0].shape[0]
    ms = xs_parts[0].shape[0]
    k = w.shape[2] if w_t else w.shape[1]
    n_parts = len(xp_parts)
    in_specs = [pl.BlockSpec((WS_BM, p.shape[1]), lambda j, i: (i, 0)) for p in xp_parts]
    in_specs += [pl.BlockSpec((ms, p.shape[1]), lambda j, i: (0, 0)) for p in xs_parts]
    in_specs.append(pl.BlockSpec((1, bn, k), lambda j, i: (layer, j, 0)) if w_t
                    else pl.BlockSpec((1, k, bn), lambda j, i: (layer, 0, j)))
    args = list(xp_parts) + list(xs_parts) + [w]
    if res is not None:
        in_specs += [pl.BlockSpec((WS_BM, bn), lambda j, i: (i, j)), pl.BlockSpec((ms, bn), lambda j, i: (0, j))]
        args += list(res)
    return pl.pallas_call(
        functools.partial(_mm_ws_kernel, n_parts=n_parts, has_res=res is not None, w_t=w_t),
        grid=(n_cols // bn, mp // WS_BM),
        in_specs=in_specs,
        out_specs=[pl.BlockSpec((WS_BM, bn), lambda j, i: (i, j)), pl.BlockSpec((ms, bn), lambda j, i: (0, j))],
        out_shape=[jax.ShapeDtypeStruct((mp, n_cols), F32), jax.ShapeDtypeStruct((ms, n_cols), F32)],
        scratch_shapes=[pltpu.VMEM((k, bn), BF16)],
        compiler_params=_cparams(("parallel", "arbitrary")),
        name=name,
    )(*args)


def _mm_narrow_kernel(x_ref, wt_ref, b_ref, o_ref, *, softplus):
    y = _dot_nt(x_ref[...], wt_ref[...].astype(BF16))
    if softplus:
        y = jax.nn.softplus(y + b_ref[...])
    o_ref[...] = y


def matmul_narrow(x, wt, bias=None):
    m, k = x.shape
    n = wt.shape[0]
    bm = min(m, 1024)
    b = jnp.zeros((LANES,), F32) if bias is None else jnp.pad(bias.astype(F32), (0, LANES - n))
    return pl.pallas_call(
        functools.partial(_mm_narrow_kernel, softplus=bias is not None),
        grid=(m // bm,),
        in_specs=[pl.BlockSpec((bm, k), lambda i: (i, 0)), pl.BlockSpec((LANES, k), lambda i: (0, 0)),
                  pl.BlockSpec((1, LANES), lambda i: (0, 0))],
        out_specs=pl.BlockSpec((bm, LANES), lambda i: (i, 0)),
        out_shape=jax.ShapeDtypeStruct((m, LANES), F32),
        compiler_params=_cparams(("parallel",)),
        name="mm_narrow",
    )(x, jnp.pad(wt, ((0, LANES - n), (0, 0))), b.reshape(1, LANES))


def _head_rms(x, gain):
    ms = jnp.mean(x * x, axis=-1, keepdims=True)
    return x * lax.rsqrt(ms + EPS) * gain


def _rope(x, cosf, sinf):
    return x * cosf + pltpu.roll(x, HEAD_DIM // 2, axis=1) * sinf


def _even_prep_kernel(p_ref, gl_ref, cos_ref, sin_ref, vg_ref, qn_ref, kn_ref,
                      u_ref, v_ref, q_ref, qr_ref, rc_ref, rs_ref, rw_ref, gate_ref, *mxu_refs):
    cosf = cos_ref[...]
    sinf = sin_ref[...]
    u_ref[...] = jax.nn.gelu(p_ref[:, 0:A_WIDTH]).astype(u_ref.dtype)
    for g in range(A_GROUPS):
        sl = slice(g * LANES, (g + 1) * LANES)
        vg = jax.nn.gelu(p_ref[:, A_WIDTH + g * LANES:A_WIDTH + (g + 1) * LANES])
        v_ref[:, sl] = _head_rms(vg, vg_ref[:, sl])
    qgain = qn_ref[...]
    for h in range(NSA_HEADS):
        sl = slice(h * LANES, (h + 1) * LANES)
        q = _head_rms(p_ref[:, 2 * A_WIDTH + h * LANES:2 * A_WIDTH + (h + 1) * LANES], qgain)
        q_ref[:, sl] = q.astype(q_ref.dtype)
        qr_ref[:, sl] = _rope(q, cosf, sinf).astype(qr_ref.dtype)
    base = 2 * A_WIDTH + B_WIDTH
    for which, row_ref in enumerate((rc_ref, rs_ref, rw_ref)):
        gain = kn_ref[which:which + 1, :]
        off = base + which * KV_COLS
        k16_ref, vt_ref = (mxu_refs[which - 1], mxu_refs[which + 1]) if (mxu_refs and which > 0) else (None, None)
        stored_ref = (mxu_refs[4] if which == 0 else row_ref) if mxu_refs else None
        tr = p_ref.shape[0]
        for h in range(KV_HEADS):
            sl = slice(h * LANES, (h + 1) * LANES)
            k = _head_rms(p_ref[:, off + h * LANES:off + (h + 1) * LANES], gain)
            if which > 0:
                k = _rope(k, cosf, sinf)
            vals = p_ref[:, off + (KV_HEADS + h) * LANES:off + (KV_HEADS + h + 1) * LANES]
            if stored_ref is not None:
                stored_ref[pl.ds(h, tr, stride=KV_SLOTS), :] = k
                stored_ref[pl.ds(KV_HEADS + h, tr, stride=KV_SLOTS), :] = vals
            if stored_ref is not row_ref:
                row_ref[:, sl] = k
                row_ref[:, (KV_HEADS + h) * LANES:(KV_HEADS + h + 1) * LANES] = vals
            if k16_ref is not None:
                k16_ref[:, sl] = k.astype(BF16)
                vt_ref[0, h] = vals.T.astype(BF16)
    gate_ref[...] = jax.nn.sigmoid(gl_ref[...])


def even_prep(proj, gl, cosf, sinf, v_gain, q_gain, k_gain, *, for_mxu):
    m = proj.shape[0]
    tr = min(m, SEL_KC)
    row = lambda w: pl.BlockSpec((tr, w), lambda i: (i, 0))
    full = lambda r, w: pl.BlockSpec((r, w), lambda i: (0, 0))
    out_shapes = [
        jax.ShapeDtypeStruct((m, A_WIDTH), BF16),
        jax.ShapeDtypeStruct((m, A_WIDTH), F32),
        jax.ShapeDtypeStruct((m, B_WIDTH), BF16),
        jax.ShapeDtypeStruct((m, B_WIDTH), BF16),
        jax.ShapeDtypeStruct((m, KV_COLS), F32),
        jax.ShapeDtypeStruct((m, KV_COLS), F32),
        jax.ShapeDtypeStruct((m, KV_COLS), F32),
        jax.ShapeDtypeStruct((m, LANES), F32),
    ]
    out_specs = [row(s.shape[1]) for s in out_shapes]
    if for_mxu:
        assert tr == SEL_KC
        stored = jax.ShapeDtypeStruct((m * KV_SLOTS, LANES), F32)
        stored_spec = pl.BlockSpec((tr * KV_SLOTS, LANES), lambda i: (i, 0))
        out_shapes[5:7] = [stored, stored]
        out_specs[5:7] = [stored_spec, stored_spec]
        out_shapes += [jax.ShapeDtypeStruct((m, KV_HEADS * LANES), BF16)] * 2
        out_specs += [row(KV_HEADS * LANES)] * 2
        out_shapes += [jax.ShapeDtypeStruct((m // tr, KV_HEADS, HEAD_DIM, tr), BF16)] * 2
        out_specs += [pl.BlockSpec((1, KV_HEADS, HEAD_DIM, tr), lambda i: (i, 0, 0, 0))] * 2
        out_shapes.append(stored)
        out_specs.append(stored_spec)
    return pl.pallas_call(
        _even_prep_kernel,
        grid=(m // tr,),
        in_specs=[row(E_MAIN), row(LANES), row(LANES), row(LANES),
                  full(1, A_WIDTH), full(1, LANES), full(3, LANES)],
        out_specs=out_specs,
        out_shape=out_shapes,
        compiler_params=_cparams(("parallel",)),
        name="even_prep",
    )(proj, gl, cosf, sinf, v_gain.reshape(1, A_WIDTH), q_gain.reshape(1, LANES), k_gain)


def _gmlp_kernel(u_ref, v_ref, ws_ref, bst_ref, o_ref):
    tril = _iota((A_CHUNK, A_CHUNK), 0) >= _iota((A_CHUNK, A_CHUNK), 1)
    for g in range(A_GROUPS):
        sl = slice(g * LANES, (g + 1) * LANES)
        wm = jnp.where(tril, ws_ref[g], 0.0).astype(BF16)
        s = _dot(wm, v_ref[:, sl].astype(BF16)) + bst_ref[:, g:g + 1]
        o_ref[:, sl] = (u_ref[:, sl].astype(F32) * s).astype(o_ref.dtype)


def gmlp(u, v, ws, bs):
    m = u.shape[0]
    row = pl.BlockSpec((A_CHUNK, A_WIDTH), lambda i: (i, 0))
    return pl.pallas_call(
        _gmlp_kernel,
        grid=(m // A_CHUNK,),
        in_specs=[row, row,
                  pl.BlockSpec((A_GROUPS, A_CHUNK, A_CHUNK), lambda i: (0, 0, 0)),
                  pl.BlockSpec((A_CHUNK, A_GROUPS), lambda i: (0, 0))],
        out_specs=row,
        out_shape=jax.ShapeDtypeStruct((m, A_WIDTH), BF16),
        compiler_params=_cparams(("parallel",)),
        name="gmlp",
    )(u, v, ws, bs.T)


def _compress_kernel(x_ref, pool_ref, o_ref):
    pool = pool_ref[...]
    e = jnp.exp(pool - jnp.max(pool, axis=0, keepdims=True))
    w = e / jnp.sum(e, axis=0, keepdims=True)
    x = x_ref[...]
    nb = x.shape[0] // CMP_BLOCK
    o_ref[...] = jnp.sum(x.reshape(nb, CMP_BLOCK, KV_COLS) * w[None], axis=1)


def compress(rows, pool):
    m = rows.shape[0]
    tr = min(m, 1024)
    pool_cols = jnp.concatenate([jnp.repeat(pool.T, LANES, axis=1)] * 2, axis=1)
    return pl.pallas_call(
        _compress_kernel,
        grid=(m // tr,),
        in_specs=[pl.BlockSpec((tr, KV_COLS), lambda i: (i, 0)),
                  pl.BlockSpec((CMP_BLOCK, KV_COLS), lambda i: (0, 0))],
        out_specs=pl.BlockSpec((tr // CMP_BLOCK, KV_COLS), lambda i: (i, 0)),
        out_shape=jax.ShapeDtypeStruct((m // CMP_BLOCK, KV_COLS), F32),
        compiler_params=_cparams(("parallel",)),
        name="compress",
    )(rows, pool_cols)


NSA_TQ = 128
SEL_KC = 512
EXP2_SCALE = ATTN_SCALE * LOG2E


def _kv_gates(gate_ref, h):
    g = gate_ref[...]
    return jnp.where(h == 0, g[:, 0:GATES_PER_KV], g[:, GATES_PER_KV:2 * GATES_PER_KV])


def _pad_rows(x, rows):
    return jnp.concatenate([x, jnp.zeros((rows - x.shape[0], x.shape[1]), x.dtype)], axis=0)


def _nsa_prompt_kernel(qn_ref, qr_ref, kc_ref, vc_ref, ks_ref, vts_ref, kw_ref, vtw_ref, gate_ref, o_ref,
                       acc_ref, osel_ref, *, seq):
    h = pl.program_id(1)
    t = pl.program_id(2)
    nb = seq // CMP_BLOCK
    tq = NSA_TQ
    qpos_t = t * tq + _iota((1, tq), 1)
    blk_t = _iota((nb, 1), 0)

    heads = lambda ref: jnp.concatenate([ref[:, g * LANES:(g + 1) * LANES] for g in range(NSA_GROUP)], axis=0)
    wide = lambda x: jnp.concatenate([x] * NSA_GROUP, axis=1)
    qn4 = heads(qn_ref)
    qr4 = heads(qr_ref)

    kc = kc_ref[0].astype(BF16)
    vc_t = _pad_rows(vc_ref[0], LANES).T.astype(BF16)
    valid_ct = ((blk_t + 1) * CMP_BLOCK - 1) <= t * tq + _iota((1, NSA_GROUP * tq), 1) % tq
    st = jnp.where(valid_ct, _dot_nt(kc, qn4) * ATTN_SCALE, NEG)
    et = jnp.where(valid_ct, jnp.exp(st - jnp.max(st, axis=0, keepdims=True)), 0.0)
    pt = et / jnp.maximum(jnp.sum(et, axis=0, keepdims=True), TINY)
    o_cmp = _dot(vc_t, _pad_rows(pt, LANES).astype(BF16))
    imp_t = pt[:, 0:tq]
    for g in range(1, NSA_GROUP):
        imp_t = imp_t + pt[:, g * tq:(g + 1) * tq]

    cur_t = qpos_t // CMP_BLOCK
    forced = (blk_t == 0) | (blk_t == cur_t)
    score = jnp.where(forced, FORCE, jnp.where(blk_t > cur_t, NEG, imp_t))
    rank = jnp.zeros((nb, tq), jnp.int32)
    for j in range(nb):
        row = score[j:j + 1, :]
        beats = (row > score) | ((row == score) & (blk_t > j))
        rank = rank + beats.astype(jnp.int32)
    sel = _pad_rows((rank < min(SEL_TOPN, nb)).astype(F32), LANES).astype(BF16)

    def attend(lo, hi, k_ref, vt_ref, bias_fn):
        acc_ref[...] = jnp.zeros(acc_ref.shape, F32)

        def body(c, carry):
            m, l = carry
            start = pl.multiple_of(c * SEL_KC, SEL_KC)
            s = _dot_nt(k_ref[pl.ds(start, SEL_KC), :], qr4) + wide(bias_fn(c))
            m_new = jnp.maximum(m, jnp.max(s, axis=0, keepdims=True))
            p = jnp.exp2((s - m_new) * EXP2_SCALE)
            alpha = jnp.exp2((m - m_new) * EXP2_SCALE)
            acc_ref[...] = alpha * acc_ref[...] + _dot(vt_ref[c, 0], p.astype(BF16))
            return m_new, alpha * l + jnp.sum(p, axis=0, keepdims=True)

        init = (jnp.full((1, NSA_GROUP * tq), NEG, F32), jnp.zeros((1, NSA_GROUP * tq), F32))
        return lax.fori_loop(lo, hi, body, init)[1]

    def kpos_col(c):
        return c * SEL_KC + _iota((SEL_KC, 1), 0)

    def sel_bias(c):
        expand = ((c * SEL_KC + _iota((SEL_KC, LANES), 0)) // CMP_BLOCK == _iota((SEL_KC, LANES), 1)).astype(BF16)
        chosen = _dot(expand, sel) > 0.5
        return jnp.where(chosen & (kpos_col(c) <= qpos_t), 0.0, NEG)

    def win_bias(c):
        kpos = kpos_col(c)
        return jnp.where((kpos <= qpos_t) & (kpos >= qpos_t - WINDOW), 0.0, NEG)

    hi = (t * tq + tq - 1) // SEL_KC + 1
    l_sel = attend(0, hi, ks_ref, vts_ref, sel_bias)
    osel_ref[...] = acc_ref[...] / l_sel
    l_win = attend(jnp.maximum(t * tq - WINDOW, 0) // SEL_KC, hi, kw_ref, vtw_ref, win_bias)
    o_win = acc_ref[...] / l_win

    g_all = gate_ref[...]
    gate_t = jnp.where(h == 0, g_all, pltpu.roll(g_all, LANES - GATES_PER_KV, axis=1)).T
    for g in range(NSA_GROUP):
        cs = slice(g * tq, (g + 1) * tq)
        out_t = (gate_t[3 * g:3 * g + 1] * o_cmp[:, cs] + gate_t[3 * g + 1:3 * g + 2] * osel_ref[:, cs]
                 + gate_t[3 * g + 2:3 * g + 3] * o_win[:, cs])
        o_ref[:, g * LANES:(g + 1) * LANES] = out_t.T.astype(o_ref.dtype)


def nsa_prompt(qn, qr, kcmp, ks16, vts16, kw16, vtw16, gate, bsz, seq):
    nt = seq // NSA_TQ
    nb = seq // CMP_BLOCK
    nc = seq // SEL_KC
    assert nb <= LANES and seq % SEL_KC == 0
    qspec = pl.BlockSpec((NSA_TQ, NSA_GROUP * LANES), lambda b, h, t: (b * nt + t, h))
    cmp_k = pl.BlockSpec((1, nb, LANES), lambda b, h, t: (b, 0, h))
    cmp_v = pl.BlockSpec((1, nb, LANES), lambda b, h, t: (b, 0, KV_HEADS + h))
    key = pl.BlockSpec((seq, LANES), lambda b, h, t: (b, h))
    val_t = pl.BlockSpec((nc, 1, HEAD_DIM, SEL_KC), lambda b, h, t: (b, h, 0, 0))
    kc3 = kcmp.reshape(bsz, nb, KV_COLS)
    return pl.pallas_call(
        functools.partial(_nsa_prompt_kernel, seq=seq),
        grid=(bsz, KV_HEADS, nt),
        in_specs=[qspec, qspec, cmp_k, cmp_v, key, val_t, key, val_t,
                  pl.BlockSpec((NSA_TQ, LANES), lambda b, h, t: (b * nt + t, 0))],
        out_specs=qspec,
        out_shape=jax.ShapeDtypeStruct((bsz * seq, B_WIDTH), BF16),
        scratch_shapes=[pltpu.VMEM((HEAD_DIM, NSA_GROUP * NSA_TQ), F32),
                        pltpu.VMEM((HEAD_DIM, NSA_GROUP * NSA_TQ), F32)],
        compiler_params=_cparams(("parallel", "parallel", "arbitrary")),
        name="nsa_prompt",
    )(qn, qr, kc3, kc3, ks16, vts16, kw16, vtw16, gate)


PAGES_PER_STEP = 16
PAGE_ROWS = PAGE_SIZE * KV_SLOTS
BLOCK_ROWS = CMP_BLOCK * KV_SLOTS
BLOCKS_PER_PAGE = PAGE_SIZE // CMP_BLOCK


def _pool_past_kernel(pt_ref, *refs, pps):
    page_refs, pool_ref, o_ref = refs[:pps], refs[pps], refs[pps + 1]
    tiles = BLOCK_ROWS // SUBLANES
    pool = pool_ref[...].reshape(tiles, SUBLANES, LANES)
    fold = lambda a, op: op(a, pltpu.roll(a, KV_SLOTS, axis=0))
    mx = fold(jnp.max(pool, axis=0), jnp.maximum)
    e = jnp.exp(pool - mx[None])
    w = e / fold(jnp.sum(e, axis=0), jnp.add)[None]
    first = _iota((SUBLANES, LANES), 0) < KV_SLOTS
    for k, ref in enumerate(page_refs):
        x = ref[0].reshape(BLOCKS_PER_PAGE, tiles, SUBLANES, LANES)
        sums = [fold(jnp.sum(x[b] * w, axis=0), jnp.add) for b in range(BLOCKS_PER_PAGE)]
        o_ref[0, k * SUBLANES:(k + 1) * SUBLANES, :] = jnp.where(first, sums[0], sums[1])


def pool_past(cache, layer_off, page_table, pool):
    bsz, n_pages = page_table.shape
    pps = min(PAGES_PER_STEP, n_pages)
    steps = n_pages // pps
    assert BLOCKS_PER_PAGE * KV_SLOTS == SUBLANES
    pool_rows = jnp.broadcast_to(jnp.tile(pool.T, (1, 2)).reshape(BLOCK_ROWS, 1), (BLOCK_ROWS, LANES))

    def page_spec(k):
        return pl.BlockSpec((1, PAGE_ROWS, LANES),
                            lambda b, s, pt: (layer_off + pt[b * n_pages + s * pps + k], 0, 0))

    grid_spec = pltpu.PrefetchScalarGridSpec(
        num_scalar_prefetch=1,
        grid=(bsz, steps),
        in_specs=[page_spec(k) for k in range(pps)]
        + [pl.BlockSpec((BLOCK_ROWS, LANES), lambda b, s, pt: (0, 0))],
        out_specs=pl.BlockSpec((1, pps * SUBLANES, LANES), lambda b, s, pt: (b, s, 0)),
    )
    return pl.pallas_call(
        functools.partial(_pool_past_kernel, pps=pps),
        grid_spec=grid_spec,
        out_shape=jax.ShapeDtypeStruct((bsz, n_pages * SUBLANES, LANES), F32),
        compiler_params=_cparams(("parallel", "arbitrary")),
        name="pool_past",
    )(page_table.reshape(-1), *([cache] * pps), pool_rows)


def _sample_select_kernel(q_ref, kc_ref, oc_ref, idx_ref, *, dec):
    rows = kc_ref.shape[1]
    x = kc_ref[0].astype(BF16)
    lane = _iota((dec, rows), 1)
    slot = _iota((dec, SEL_TOPN), 1)
    for h in range(KV_HEADS):
        is_key = lane % KV_SLOTS == h
        imp = jnp.zeros((dec, rows), F32)
        for g in range(NSA_GROUP):
            hd = h * NSA_GROUP + g
            q = q_ref[0, :, hd * LANES:(hd + 1) * LANES]
            s = jnp.where(is_key, _dot_nt(q, x) * ATTN_SCALE, NEG)
            e = jnp.where(is_key, jnp.exp(s - jnp.max(s, axis=-1, keepdims=True)), 0.0)
            p = e / jnp.maximum(jnp.sum(e, axis=-1, keepdims=True), TINY)
            imp = imp + p
            oc_ref[0, :, hd * LANES:(hd + 1) * LANES] = _dot(pltpu.roll(p, KV_HEADS, axis=1).astype(BF16), x)
        score = jnp.where(is_key & (lane >= KV_SLOTS), imp, -jnp.inf)
        picked = jnp.zeros((dec, SEL_TOPN), jnp.int32)
        for r in range(SEL_TOPN - 2):
            best = jnp.max(score, axis=-1, keepdims=True)
            arg = jnp.min(jnp.where(score == best, lane, rows), axis=-1, keepdims=True)
            picked = jnp.where(slot == r + 1, arg // KV_SLOTS, picked)
            score = jnp.where(lane == arg, -jnp.inf, score)
        idx_ref[0, h * dec:(h + 1) * dec, :] = picked


def sample_select(qn, kcmp_past):
    bsz, dec, _ = qn.shape
    rows = kcmp_past.shape[1]
    return pl.pallas_call(
        functools.partial(_sample_select_kernel, dec=dec),
        grid=(bsz,),
        in_specs=[pl.BlockSpec((1, dec, B_WIDTH), lambda b: (b, 0, 0)),
                  pl.BlockSpec((1, rows, LANES), lambda b: (b, 0, 0))],
        out_specs=[pl.BlockSpec((1, dec, B_WIDTH), lambda b: (b, 0, 0)),
                   pl.BlockSpec((1, KV_HEADS * dec, SEL_TOPN), lambda b: (b, 0, 0))],
        out_shape=[jax.ShapeDtypeStruct((bsz, dec, B_WIDTH), F32),
                   jax.ShapeDtypeStruct((bsz, KV_HEADS * dec, SEL_TOPN), jnp.int32)],
        compiler_params=_cparams(("parallel",)),
        name="sample_select",
    )(qn, kcmp_past)


def _rows_to_tile(rows):
    r = _iota((SUBLANES, rows[0].shape[1]), 0)
    out = jnp.zeros((SUBLANES, rows[0].shape[1]), rows[0].dtype)
    for i, x in enumerate(rows):
        out = jnp.where(r == i, jnp.broadcast_to(x, out.shape), out)
    return out


def _sample_attn_kernel(idx_ref, pt_ref, *refs, dec):
    n = SEL_TOPN
    blk_refs = refs[:n]
    (qr_ref, ksn_ref, vsn_ref, wbuf_ref, kwn_ref, vwn_ref, gate_ref, oc_ref, o_ref) = refs[n:]
    h = pl.program_id(1)
    qi = pl.program_id(2)
    q8 = _rows_to_tile([qr_ref[0, 0, :, g * LANES:(g + 1) * LANES].astype(F32)
                        for g in range(NSA_GROUP)]).astype(BF16)
    newpos = _iota((1, dec), 1)

    def attend(x_old, mask_old, k_new, v_new):
        s_old = jnp.where(mask_old, _dot_nt(q8, x_old) * ATTN_SCALE, NEG)
        mask_new = newpos <= qi
        s_new = jnp.where(mask_new, _dot_nt(q8, k_new) * ATTN_SCALE, NEG)
        m = jnp.maximum(jnp.max(s_old, axis=-1, keepdims=True), jnp.max(s_new, axis=-1, keepdims=True))
        p_old = jnp.where(mask_old, jnp.exp(s_old - m), 0.0)
        p_new = jnp.where(mask_new, jnp.exp(s_new - m), 0.0)
        l = jnp.sum(p_old, axis=-1, keepdims=True) + jnp.sum(p_new, axis=-1, keepdims=True)
        pv = _dot(pltpu.roll(p_old, KV_HEADS, axis=1).astype(BF16), x_old)
        return (pv + _dot(p_new.astype(BF16), v_new)) / l

    x_sel = jnp.concatenate([r[0].astype(BF16) for r in blk_refs], axis=0)
    lane = _iota((1, n * BLOCK_ROWS), 1)
    mask_sel = (lane % KV_SLOTS == h) & (lane < (n - 1) * BLOCK_ROWS)
    o_sel = attend(x_sel, mask_sel, ksn_ref[0].astype(BF16), vsn_ref[0].astype(BF16))

    wrows = wbuf_ref.shape[1]
    wb = wrows // KV_SLOTS
    lane_w = _iota((1, wrows), 1)
    mask_win = (lane_w % KV_SLOTS == h) & (lane_w // KV_SLOTS >= qi + (wb - WINDOW))
    o_win = attend(wbuf_ref[0].astype(BF16), mask_win, kwn_ref[0].astype(BF16), vwn_ref[0].astype(BF16))

    gate = _kv_gates(gate_ref.at[0, 0], h)
    outs = []
    for g in range(NSA_GROUP):
        outs.append(gate[:, 3 * g:3 * g + 1] * oc_ref[0, 0, :, g * LANES:(g + 1) * LANES]
                    + gate[:, 3 * g + 1:3 * g + 2] * o_sel[g:g + 1]
                    + gate[:, 3 * g + 2:3 * g + 3] * o_win[g:g + 1])
    o_ref[0, 0] = jnp.concatenate(outs, axis=1).astype(o_ref.dtype)


def sample_attn(idx, page_table, cache_sel, sel_off, cache_win, win_off, qr, rows_s, rows_w, gate, o_cmp):
    bsz, dec, _ = rows_s.shape
    n_pages = page_table.shape[1]
    grp = NSA_GROUP * LANES

    def phys(b, h, q, k, idx_r, pt_r):
        blk = idx_r[((b * KV_HEADS + h) * dec + q) * SEL_TOPN + k]
        return (sel_off + pt_r[b * n_pages + blk // BLOCKS_PER_PAGE]) * BLOCKS_PER_PAGE + blk % BLOCKS_PER_PAGE

    def bspec(k):
        return pl.BlockSpec((1, BLOCK_ROWS, LANES),
                            lambda b, h, q, idx_r, pt_r: (phys(b, h, q, k, idx_r, pt_r), 0, 0))

    qmap = lambda b, h, q, idx_r, pt_r: (b, q, 0, h)
    newk = pl.BlockSpec((1, dec, LANES), lambda b, h, q, idx_r, pt_r: (b, 0, h))
    newv = pl.BlockSpec((1, dec, LANES), lambda b, h, q, idx_r, pt_r: (b, 0, KV_HEADS + h))
    wbuf = pl.BlockSpec((1, cache_win.shape[1], LANES), lambda b, h, q, idx_r, pt_r: (win_off + b, 0, 0))
    grid_spec = pltpu.PrefetchScalarGridSpec(
        num_scalar_prefetch=2,
        grid=(bsz, KV_HEADS, dec),
        in_specs=[bspec(k) for k in range(SEL_TOPN)]
        + [pl.BlockSpec((1, 1, 1, grp), qmap), newk, newv, wbuf, newk, newv,
           pl.BlockSpec((1, 1, 1, LANES), lambda b, h, q, idx_r, pt_r: (b, q, 0, 0)),
           pl.BlockSpec((1, 1, 1, grp), qmap)],
        out_specs=pl.BlockSpec((1, 1, 1, grp), qmap),
    )
    q4 = lambda a: a.reshape(bsz, dec, 1, a.shape[-1])
    return pl.pallas_call(
        functools.partial(_sample_attn_kernel, dec=dec),
        grid_spec=grid_spec,
        out_shape=jax.ShapeDtypeStruct((bsz, dec, 1, B_WIDTH), BF16),
        compiler_params=_cparams(("parallel", "parallel", "arbitrary")),
        name="sample_attn",
    )(idx.reshape(-1), page_table.reshape(-1), *([cache_sel] * SEL_TOPN),
      q4(qr), rows_s, rows_s, cache_win, rows_w, rows_w, q4(gate), q4(o_cmp))


def _shift_rows(x, prev, k):
    xr = pltpu.roll(x, k, axis=0)
    pr = pltpu.roll(prev, k, axis=0)
    top = jnp.where(_iota((SUBLANES, 1), 0) < k, pr, xr[:SUBLANES])
    return jnp.concatenate([top, xr[SUBLANES:]], axis=0)


def _taps_kernel(*refs, width, act):
    x_refs, w_ref, b_ref, o_ref = refs[:width], refs[width], refs[width + 1], refs[width + 2]
    y = b_ref[...] + x_refs[0][...] * w_ref[0:1, :]
    for k in range(1, width):
        y = y + x_refs[k][...] * w_ref[k:k + 1, :]
    if act == "silu":
        o_ref[...] = y * jax.nn.sigmoid(y)
    else:
        o_ref[...] = y


def conv_taps(views, w, b, *, act, tile):
    width = len(views)
    m, n = views[0].shape
    col = pl.BlockSpec((m, tile), lambda j: (0, j))
    return pl.pallas_call(
        functools.partial(_taps_kernel, width=width, act=act),
        grid=(n // tile,),
        in_specs=[col] * width + [pl.BlockSpec((width, tile), lambda j: (0, j)),
                                  pl.BlockSpec((1, tile), lambda j: (0, j))],
        out_specs=col,
        out_shape=jax.ShapeDtypeStruct((m, n), F32),
        compiler_params=_cparams(("parallel",)),
        name="conv_taps_" + act,
    )(*views, w, b.reshape(1, -1))


def _glu_kernel(g_ref, u_ref, o_ref):
    g = g_ref[...]
    o_ref[...] = (g * jax.nn.sigmoid(g) * u_ref[...]).astype(o_ref.dtype)


def glu(hc, half, tile):
    m = hc.shape[0]
    nt = half // tile
    return pl.pallas_call(
        _glu_kernel,
        grid=(nt,),
        in_specs=[pl.BlockSpec((m, tile), lambda j: (0, j)), pl.BlockSpec((m, tile), lambda j: (0, nt + j))],
        out_specs=pl.BlockSpec((m, tile), lambda j: (0, j)),
        out_shape=jax.ShapeDtypeStruct((m, half), BF16),
        compiler_params=_cparams(("parallel",)),
        name="glu",
    )(hc, hc)


SSD_GPS = 4


def _ssd_kernel(*refs, n_chunks, fuse_conv):
    c = pl.program_id(2)
    q = SSM_CHUNK
    e_heads = SSM_GROUP_HEADS
    gw = SSM_GROUP_WIDTH
    if fuse_conv:
        (xr_ref, br_ref, cr_ref, z_ref, cwx_ref, cwb_ref, cwc_ref, cbx_ref, cbb_ref, cbc_ref,
         dt_ref, dtt_ref, al_ref, alt_ref, dsk_ref, ng_ref, h0_ref,
         y_ref, ht_ref, tx_ref, tb_ref, tc_ref, st_ref, hx_ref, hb_ref, hc_ref) = refs

        def conv_act(raw_ref, hist_ref, cw_ref, cb_ref, tail_ref):
            x = raw_ref[...]
            prev = jnp.where(c == 0, 0.0, hist_ref[...])
            y = x * cw_ref[0, SSM_CONV - 1:SSM_CONV, :] + cb_ref[0]
            for d in range(1, SSM_CONV):
                y = y + _shift_rows(x, prev, d) * cw_ref[0, SSM_CONV - 1 - d:SSM_CONV - d, :]
            last = x[q - SUBLANES:, :]
            hist_ref[...] = last
            tail_ref[0] = last
            return y * jax.nn.sigmoid(y)

        xs_all = conv_act(xr_ref, hx_ref, cwx_ref, cbx_ref, tx_ref)
        bm_all = conv_act(br_ref, hb_ref, cwb_ref, cbb_ref, tb_ref)
        cm_all = conv_act(cr_ref, hc_ref, cwc_ref, cbc_ref, tc_ref)
    else:
        (xs_ref, bm_ref, cm_ref, z_ref, dt_ref, dtt_ref, al_ref, alt_ref, dsk_ref, ng_ref, h0_ref,
         y_ref, ht_ref, st_ref) = refs
        xs_all, bm_all, cm_all = xs_ref[...], bm_ref[...], cm_ref[...]

    @pl.when(c == 0)
    def _():
        st_ref[...] = h0_ref[0]

    ri = _iota((q, q), 0)
    ci = _iota((q, q), 1)
    tril = ri >= ci
    tril_b = tril.astype(BF16)
    triu_b = (ri <= ci).astype(BF16)
    expand = (_iota((e_heads, gw), 0) == _iota((e_heads, gw), 1) // SSM_HEAD_DIM).astype(BF16)
    lane = _iota((q, 2 * SSM_HEAD_DIM), 1)

    for k in range(SSD_GPS):
        gs = slice(k * gw, (k + 1) * gw)
        ns = slice(k * SSM_STATE, (k + 1) * SSM_STATE)
        dt = dt_ref[k]
        da = dt * (-jnp.exp(al_ref[k]))
        da_t = dtt_ref[k] * (-jnp.exp(alt_ref[k]))
        acs = _dot3_r(tril_b, da)
        acs_t = _dot3_l(da_t, triu_b)
        tot = acs[q - 1:q, :]

        xs = xs_all[:, gs]
        xdt = xs * _dot2_l(dt, expand)
        xdt_b = xdt.astype(BF16)
        bm = bm_all[:, ns]
        cm_b = cm_all[:, ns].astype(BF16)
        cb = _dot_nt(cm_b, bm.astype(BF16))

        y_parts = []
        for pair in range(e_heads // 2):
            cols = slice(pair * 2 * SSM_HEAD_DIM, (pair + 1) * 2 * SSM_HEAD_DIM)
            res = []
            for e in (2 * pair, 2 * pair + 1):
                seg = jnp.where(tril, acs[:, e:e + 1] - acs_t[e:e + 1, :], NEG)
                mix = (cb * jnp.exp(seg)).astype(BF16)
                res.append(_dot(mix, xdt_b[:, cols]))
            y_parts.append(jnp.where(lane < SSM_HEAD_DIM, res[0], res[1]))
        y = jnp.concatenate(y_parts, axis=1)

        state = st_ref[:, gs]
        ex = _dot2_l(jnp.concatenate([jnp.exp(acs), jnp.exp(tot - acs)], axis=0), expand)
        y = y + _dot(cm_b, state.astype(BF16)) * ex[0:q]
        contrib = _dot(bm.T.astype(BF16), (xdt * ex[q:2 * q]).astype(BF16))
        st_ref[:, gs] = state * ex[q - 1:q] + contrib

        y = y + xs * dsk_ref[:, gs]
        z = z_ref[:, gs]
        y = y * (z * jax.nn.sigmoid(z))
        ms = jnp.mean(y * y, axis=-1, keepdims=True)
        y_ref[:, gs] = (y * lax.rsqrt(ms + EPS) * ng_ref[:, gs]).astype(y_ref.dtype)

    @pl.when(c == n_chunks - 1)
    def _():
        ht_ref[0] = st_ref[...]


def ssd(xbc, xbc_col0, zsrc, dt, a_log, d_skip, norm_g, h0, bsz, seq, conv=None):
    nc = seq // SSM_CHUNK
    gps = SSD_GPS
    gw = SSM_GROUP_WIDTH * gps
    nw = SSM_STATE * gps
    e = SSM_GROUP_HEADS
    m = bsz * seq
    dt_g = dt.reshape(m, SSM_GROUPS, e).transpose(1, 0, 2)
    dt_t = dt_g.transpose(0, 2, 1)
    al = a_log.astype(F32).reshape(SSM_GROUPS, 1, e)
    al_t = a_log.astype(F32).reshape(SSM_GROUPS, e, 1)
    dsk = jnp.repeat(d_skip.astype(F32), SSM_HEAD_DIM).reshape(1, SSM_INNER)
    bc = SSM_GROUPS * SSM_STATE
    x_off = xbc_col0 // gw
    b_off = (xbc_col0 + SSM_INNER) // nw
    c_off = (xbc_col0 + SSM_INNER + bc) // nw
    rowmap = lambda col: (lambda b, g, c: (b * nc + c, col(g)))
    xbc_specs = [pl.BlockSpec((SSM_CHUNK, gw), rowmap(lambda g: x_off + g)),
                 pl.BlockSpec((SSM_CHUNK, nw), rowmap(lambda g: b_off + g)),
                 pl.BlockSpec((SSM_CHUNK, nw), rowmap(lambda g: c_off + g)),
                 pl.BlockSpec((SSM_CHUNK, gw), rowmap(lambda g: g))]
    rest_specs = [pl.BlockSpec((gps, SSM_CHUNK, e), lambda b, g, c: (g, b * nc + c, 0)),
                  pl.BlockSpec((gps, e, SSM_CHUNK), lambda b, g, c: (g, 0, b * nc + c)),
                  pl.BlockSpec((gps, 1, e), lambda b, g, c: (g, 0, 0)),
                  pl.BlockSpec((gps, e, 1), lambda b, g, c: (g, 0, 0)),
                  pl.BlockSpec((1, gw), lambda b, g, c: (0, g)),
                  pl.BlockSpec((1, gw), lambda b, g, c: (0, g)),
                  pl.BlockSpec((1, SSM_STATE, gw), lambda b, g, c: (b, 0, g))]
    rest_args = [dt_g, dt_t, al, al_t, dsk, norm_g.astype(F32).reshape(1, SSM_INNER), h0]
    out_specs = [pl.BlockSpec((SSM_CHUNK, gw), rowmap(lambda g: g)),
                 pl.BlockSpec((1, SSM_STATE, gw), lambda b, g, c: (b, 0, g))]
    out_shape = [jax.ShapeDtypeStruct((m, SSM_INNER), BF16),
                 jax.ShapeDtypeStruct((bsz, SSM_STATE, SSM_INNER), F32)]
    scratch = [pltpu.VMEM((SSM_STATE, gw), F32)]
    conv_specs, conv_args = [], []
    if conv is not None:
        cw, cb, layer = conv
        cb3 = cb.reshape(cb.shape[0], 1, -1)
        offs = ((gw, 0), (nw, SSM_INNER // nw), (nw, (SSM_INNER + bc) // nw))
        conv_specs = [pl.BlockSpec((1, SSM_CONV, w), lambda b, g, c, o=o: (layer, 0, o + g)) for w, o in offs]
        conv_specs += [pl.BlockSpec((1, 1, w), lambda b, g, c, o=o: (layer, 0, o + g)) for w, o in offs]
        conv_args = [cw, cw, cw, cb3, cb3, cb3]
        for w, total in ((gw, SSM_INNER), (nw, bc), (nw, bc)):
            out_specs.append(pl.BlockSpec((1, SUBLANES, w), lambda b, g, c: (b, 0, g)))
            out_shape.append(jax.ShapeDtypeStruct((bsz, SUBLANES, total), F32))
            scratch.append(pltpu.VMEM((SUBLANES, w), F32))
    return pl.pallas_call(
        functools.partial(_ssd_kernel, n_chunks=nc, fuse_conv=conv is not None),
        grid=(bsz, SSM_GROUPS // gps, nc),
        in_specs=xbc_specs + conv_specs + rest_specs,
        out_specs=out_specs,
        out_shape=out_shape,
        scratch_shapes=scratch,
        compiler_params=_cparams(("parallel", "parallel", "arbitrary")),
        name="ssd",
    )(xbc, xbc, xbc, zsrc, *conv_args, *rest_args)


def _state_to_cols(h):
    b = h.shape[0]
    return h.transpose(0, 3, 1, 2).reshape(b, SSM_STATE, SSM_INNER)


def _state_from_cols(s):
    b = s.shape[0]
    return s.reshape(b, SSM_STATE, SSM_HEADS, SSM_HEAD_DIM).transpose(0, 2, 3, 1)


FFN_BM = 1024
FFN_SUB = 256
FFN_TILE = 512
FFN_TILES = D_FF // FFN_TILE


def _conv3(h, prev, cw_ref, cb_ref):
    return (_shift_rows(h, prev, 2) * cw_ref[0, 0:1, :] + _shift_rows(h, prev, 1) * cw_ref[0, 1:2, :]
            + h * cw_ref[0, 2:3, :] + cb_ref[0])


def _ffn_up_kernel(x_ref, xs_ref, wg_ref, wu_ref, cwg_ref, cwu_ref, cbg_ref, cbu_ref,
                   act_ref, tg_ref, tu_ref, sg_ref, su_ref, wgb_ref, wub_ref, cg_ref, cu_ref, *, tiles_per_seq):
    i = pl.program_id(1)

    @pl.when(i == 0)
    def _():
        wgb_ref[...] = wg_ref[0].astype(BF16)
        wub_ref[...] = wu_ref[0].astype(BF16)
        sg_ref[...] = _dot(xs_ref[...], wgb_ref[...])
        su_ref[...] = _dot(xs_ref[...], wub_ref[...])

    @pl.when(i % tiles_per_seq == 0)
    def _():
        cg_ref[...] = jnp.zeros_like(cg_ref)
        cu_ref[...] = jnp.zeros_like(cu_ref)

    for r in range(FFN_BM // FFN_SUB):
        rs = slice(r * FFN_SUB, (r + 1) * FFN_SUB)
        x = x_ref[rs, :]
        hg = _dot(x, wgb_ref[...])
        hu = _dot(x, wub_ref[...])
        g = _conv3(hg, cg_ref[...], cwg_ref, cbg_ref)
        u = _conv3(hu, cu_ref[...], cwu_ref, cbu_ref)
        act_ref[rs, :] = (g * jax.nn.sigmoid(g) * u).astype(act_ref.dtype)
        cg_ref[...] = hg[FFN_SUB - SUBLANES:, :]
        cu_ref[...] = hu[FFN_SUB - SUBLANES:, :]
    tg_ref[0] = cg_ref[...]
    tu_ref[0] = cu_ref[...]


def ffn_up(xp, xs, w_up, conv_w, conv_b, layer, bsz, seq):
    m, k = xp.shape
    ms = xs.shape[0]
    tiles_per_seq = seq // FFN_BM
    nt = FFN_TILES
    wspec = lambda off: pl.BlockSpec((1, k, FFN_TILE), lambda j, i: (layer, 0, off + j))
    cwspec = lambda off: pl.BlockSpec((1, FFN_CONV, FFN_TILE), lambda j, i: (layer, 0, off + j))
    cbspec = lambda off: pl.BlockSpec((1, 1, FFN_TILE), lambda j, i: (layer, 0, off + j))
    tail = pl.BlockSpec((1, SUBLANES, FFN_TILE), lambda j, i: (i // tiles_per_seq, 0, j))
    samp = pl.BlockSpec((ms, FFN_TILE), lambda j, i: (0, j))
    cb3 = conv_b.reshape(conv_b.shape[0], 1, -1)
    carry = pltpu.VMEM((SUBLANES, FFN_TILE), F32)
    return pl.pallas_call(
        functools.partial(_ffn_up_kernel, tiles_per_seq=tiles_per_seq),
        grid=(nt, m // FFN_BM),
        in_specs=[pl.BlockSpec((FFN_BM, k), lambda j, i: (i, 0)), pl.BlockSpec((ms, k), lambda j, i: (0, 0)),
                  wspec(0), wspec(nt), cwspec(0), cwspec(nt), cbspec(0), cbspec(nt)],
        out_specs=[pl.BlockSpec((FFN_BM, FFN_TILE), lambda j, i: (i, j)), tail, tail, samp, samp],
        out_shape=[jax.ShapeDtypeStruct((m, D_FF), BF16),
                   jax.ShapeDtypeStruct((bsz, SUBLANES, D_FF), F32), jax.ShapeDtypeStruct((bsz, SUBLANES, D_FF), F32),
                   jax.ShapeDtypeStruct((ms, D_FF), F32), jax.ShapeDtypeStruct((ms, D_FF), F32)],
        scratch_shapes=[pltpu.VMEM((k, FFN_TILE), BF16), pltpu.VMEM((k, FFN_TILE), BF16), carry, carry],
        compiler_params=_cparams(("parallel", "arbitrary")),
        name="ffn_up",
    )(xp, xs, w_up, w_up, conv_w, conv_w, cb3, cb3)


def _rope_tables(pos):
    half = HEAD_DIM // 2
    inv = ROPE_THETA ** (-jnp.arange(half, dtype=F32) / half)
    ang = pos.astype(F32)[:, None] * inv[None, :]
    cos, sin = jnp.cos(ang), jnp.sin(ang)
    return jnp.concatenate([cos, cos], axis=1), jnp.concatenate([-sin, sin], axis=1)


def _kv_out(rows, bsz):
    return rows.reshape(bsz, -1, 2, KV_HEADS, HEAD_DIM)


def kernel(x_prompt, x_sample, cache_kv_cmp, cache_kv_sel, cache_kv_win, state_ssm_conv, state_ssm, state_ffn_conv, page_table, norm_mix, norm_ffn, w_in_even, w_out_even, gmlp_v_norm, gmlp_ws, gmlp_bs, q_norm, k_norm, cmp_pool, w_in_odd, ssm_conv_w, ssm_conv_b, ssm_dt_bias, ssm_a_log, ssm_d, ssm_norm, w_out_odd, ffn_w_up, ffn_conv_w, ffn_conv_b, ffn_w_down):
    bp, sp, _ = x_prompt.shape
    bs, ss, _ = x_sample.shape
    depth = norm_mix.shape[0]
    n_pool = cache_kv_cmp.shape[1]
    n_pages = page_table.shape[1]
    past_len = n_pages * PAGE_SIZE
    wb = cache_kv_win.shape[2]
    mp, ms = bp * sp, bs * ss

    cos_p, sin_p = _rope_tables(jnp.arange(sp, dtype=jnp.int32))
    cos_p, sin_p = jnp.tile(cos_p, (bp, 1)), jnp.tile(sin_p, (bp, 1))
    cos_s, sin_s = _rope_tables(past_len + jnp.arange(ss, dtype=jnp.int32))
    cos_s, sin_s = jnp.tile(cos_s, (bs, 1)), jnp.tile(sin_s, (bs, 1))

    cache_c = cache_kv_cmp.reshape(-1, PAGE_ROWS, LANES)
    cache_s = cache_kv_sel.reshape(-1, BLOCK_ROWS, LANES)
    cache_w = cache_kv_win.reshape(-1, wb * KV_SLOTS, LANES)

    w_down16 = ffn_w_down.astype(BF16)
    w_out_odd16 = w_out_odd.astype(BF16)
    w_in_even_t = jnp.swapaxes(w_in_even, 1, 2)
    w_in_odd_t = jnp.swapaxes(w_in_odd, 1, 2)

    hp = x_prompt.reshape(mp, D_MODEL)
    hs = x_sample.reshape(ms, D_MODEL)
    outs = {k: [] for k in ("p_cmp", "p_sel", "p_win", "s_cmp", "s_sel", "s_win", "s_v",
                            "p_sconv", "p_sst", "s_sconv", "s_sst", "p_fconv", "s_fconv")}

    for layer in range(depth):
        i = layer // 2
        xp = rms_cast(hp, norm_mix[layer])
        xs = rms_cast(hs, norm_mix[layer])
        if layer % 2 == 0:
            w_gate = w_in_even_t[i, E_MAIN:]
            proj_p, proj_s = matmul_ws([xp], [xs], w_in_even_t, i, E_MAIN, bn=768, w_t=True, name="even_in")
            gl = matmul_narrow(xp, w_gate)
            u, v, qn, qr, rc, rs, rw, gate, ks16, kw16, vts16, vtw16, rc_stored = even_prep(
                proj_p, gl, cos_p, sin_p, gmlp_v_norm[i], q_norm[i], k_norm[i], for_mxu=True)
            a_p = gmlp(u, v, gmlp_ws[i], gmlp_bs[i])
            kcmp = compress(rc, cmp_pool[i])
            b_p = nsa_prompt(qn, qr, kcmp, ks16, vts16, kw16, vtw16, gate, bp, sp)
            outs["p_cmp"].append(_kv_out(rc_stored, bp))
            outs["p_sel"].append(_kv_out(rs, bp))
            outs["p_win"].append(_kv_out(rw, bp)[:, sp - min(WINDOW, sp):])
            gl = matmul_narrow(xs, w_gate)
            u, v, qn, qr, rc, rs, rw, gate = even_prep(
                proj_s, gl, cos_s, sin_s, gmlp_v_norm[i], q_norm[i], k_norm[i], for_mxu=False)
            lpad = ((0, 0), (0, A_CHUNK - ss), (0, 0))
            a_s = gmlp(jnp.pad(u.reshape(bs, ss, -1), lpad).reshape(bs * A_CHUNK, -1),
                       jnp.pad(v.reshape(bs, ss, -1), lpad).reshape(bs * A_CHUNK, -1),
                       gmlp_ws[i], gmlp_bs[i]).reshape(bs, A_CHUNK, -1)[:, :ss].reshape(ms, -1)
            kc_past = pool_past(cache_c, i * n_pool, page_table, cmp_pool[i])
            o_cmp, idx = sample_select(qn.reshape(bs, ss, -1), kc_past)
            b_s = sample_attn(idx, page_table, cache_s, i * n_pool, cache_w, i * bs,
                              qr.reshape(bs, ss, -1), rs.reshape(bs, ss, -1), rw.reshape(bs, ss, -1),
                              gate.reshape(bs, ss, -1), o_cmp).reshape(ms, B_WIDTH)
            outs["s_cmp"].append(_kv_out(rc, bs))
            outs["s_sel"].append(_kv_out(rs, bs))
            outs["s_win"].append(_kv_out(rw, bs))
            outs["s_v"].append(v.reshape(bs, ss, A_WIDTH))
            hp, hs = matmul_ws([a_p, b_p], [a_s, b_s], w_out_even, i, D_MODEL, bn=1024, res=(hp, hs), name="even_out")
        else:
            zx = SSM_INNER + SSM_CONV_DIM
            w_dt = w_in_odd_t[i, zx:]
            proj_p, proj_s = matmul_ws([xp], [xs], w_in_odd_t, i, zx, bn=1024, w_t=True, name="odd_in")
            dt = matmul_narrow(xp, w_dt, ssm_dt_bias[i])[:, :SSM_HEADS]
            h0 = jnp.zeros((bp, SSM_STATE, SSM_INNER), F32)
            y, ht, *tails = ssd(proj_p, SSM_INNER, proj_p, dt, ssm_a_log[i], ssm_d[i], ssm_norm[i], h0, bp, sp,
                                conv=(ssm_conv_w, ssm_conv_b, i))
            tail = jnp.concatenate(tails, axis=2)
            hp = matmul(y, w_out_odd16, layer=i, bn=512, res=hp, name="odd_out")
            outs["p_sconv"].append(tail[:, SUBLANES - (SSM_CONV - 1):])
            outs["p_sst"].append(_state_from_cols(ht))
            dt = matmul_narrow(xs, w_dt, ssm_dt_bias[i])[:, :SSM_HEADS]
            xin = jnp.concatenate([state_ssm_conv[i], proj_s[:, SSM_INNER:].reshape(bs, ss, -1)], axis=1)
            views = [xin[:, k:k + ss].reshape(ms, -1) for k in range(SSM_CONV)]
            xbc = conv_taps(views, ssm_conv_w[i], ssm_conv_b[i], act="silu", tile=1024)
            cpad = ((0, 0), (0, SSM_CHUNK - ss), (0, 0))
            padrows = lambda a: jnp.pad(a.reshape(bs, ss, -1), cpad).reshape(bs * SSM_CHUNK, -1)
            y, ht = ssd(padrows(xbc), 0, padrows(proj_s[:, :SSM_INNER]), padrows(dt), ssm_a_log[i], ssm_d[i], ssm_norm[i],
                        _state_to_cols(state_ssm[i].astype(F32)), bs, SSM_CHUNK)
            y = y.reshape(bs, SSM_CHUNK, -1)[:, :ss].reshape(ms, -1)
            hs = matmul(y, w_out_odd16, layer=i, bn=1024, res=hs, name="odd_out_s")
            outs["s_sconv"].append(xin[:, ss:])
            outs["s_sst"].append(_state_from_cols(ht))
        xp = rms_cast(hp, norm_ffn[layer])
        xs = rms_cast(hs, norm_ffn[layer])
        act, tail_g, tail_u, hu_g, hu_u = ffn_up(xp, xs, ffn_w_up, ffn_conv_w, ffn_conv_b, layer, bp, sp)
        hp = matmul(act, w_down16, layer=layer, bn=512, res=hp, name="ffn_down")
        outs["p_fconv"].append(jnp.concatenate([tail_g, tail_u], axis=2)[:, SUBLANES - (FFN_CONV - 1):])
        hu = jnp.concatenate([hu_g, hu_u], axis=1)
        xin = jnp.concatenate([state_ffn_conv[layer], hu.reshape(bs, ss, -1)], axis=1)
        views = [xin[:, k:k + ss].reshape(ms, -1) for k in range(FFN_CONV)]
        hc = conv_taps(views, ffn_conv_w[layer], ffn_conv_b[layer], act="none", tile=1024)
        hs = matmul(glu(hc, D_FF, 512), w_down16, layer=layer, bn=512, res=hs, name="ffn_down_s")
        outs["s_fconv"].append(xin[:, ss:])

    st = lambda k: jnp.stack(outs[k])
    return (hp.reshape(bp, sp, D_MODEL), hs.reshape(bs, ss, D_MODEL), st("p_cmp"), st("p_sel"), st("p_win"),
            st("p_sconv"), st("p_sst"), st("p_fconv"), st("s_cmp"), st("s_sel"), st("s_win"), st("s_v"),
            st("s_sconv"), st("s_sst"), st("s_fconv"))
```

```python
import functools

import jax
import jax.numpy as jnp
from jax import lax
from jax.experimental import pallas as pl
from jax.experimental.pallas import tpu as pltpu

F32 = jnp.float32
BF16 = jnp.bfloat16

D_MODEL = 2048
A_WIDTH = D_MODEL // 2
A_GROUPS = 8
A_CHUNK = 128
NSA_HEADS = 8
HEAD_DIM = 128
KV_HEADS = 2
NSA_GROUP = NSA_HEADS // KV_HEADS
B_WIDTH = NSA_HEADS * HEAD_DIM
KV_COLS = 2 * KV_HEADS * HEAD_DIM
KV_SLOTS = 2 * KV_HEADS
CMP_BLOCK = 64
SEL_TOPN = 16
WINDOW = 512
PAGE_SIZE = 128
ROPE_THETA = 10000.0
ATTN_SCALE = HEAD_DIM ** -0.5
LOG2E = 1.4426950408889634
SSM_INNER = 2 * D_MODEL
SSM_HEAD_DIM = 64
SSM_HEADS = SSM_INNER // SSM_HEAD_DIM
SSM_GROUPS = 8
SSM_GROUP_HEADS = SSM_HEADS // SSM_GROUPS
SSM_GROUP_WIDTH = SSM_INNER // SSM_GROUPS
SSM_STATE = 128
SSM_CONV = 4
SSM_CHUNK = 128
SSM_CONV_DIM = SSM_INNER + 2 * SSM_GROUPS * SSM_STATE
D_FF = 5632
FFN_CONV = 3
EPS = 1e-6
NEG = -1e30
FORCE = 1e4
TINY = 1e-30

VMEM_LIMIT_BYTES = 56 * 1024 * 1024
LANES = 128
SUBLANES = 8

E_MAIN = 2 * A_WIDTH + B_WIDTH + 3 * KV_COLS
N_GATES = 3 * NSA_HEADS
GATES_PER_KV = 3 * NSA_GROUP


def _cparams(sem):
    return pltpu.CompilerParams(dimension_semantics=sem, vmem_limit_bytes=VMEM_LIMIT_BYTES)


def _dot(a, b):
    return jnp.dot(a, b, preferred_element_type=F32)


def _dot_nt(a, b):
    return lax.dot_general(a, b, (((1,), (1,)), ((), ())), preferred_element_type=F32)


def _iota(shape, dim):
    return lax.broadcasted_iota(jnp.int32, shape, dim)


def _split3(x):
    hi = x.astype(BF16)
    r1 = x - hi.astype(F32)
    mid = r1.astype(BF16)
    lo = (r1 - mid.astype(F32)).astype(BF16)
    return hi, mid, lo


def _dot3_l(x, m):
    hi, mid, lo = _split3(x)
    return _dot(hi, m) + _dot(mid, m) + _dot(lo, m)


def _dot3_r(m, x):
    hi, mid, lo = _split3(x)
    return _dot(m, hi) + _dot(m, mid) + _dot(m, lo)


def _dot2_l(x, m):
    hi = x.astype(BF16)
    return _dot(hi, m) + _dot((x - hi.astype(F32)).astype(BF16), m)


def _rms_kernel(x_ref, g_ref, o_ref):
    x = x_ref[...]
    ms = jnp.mean(x * x, axis=-1, keepdims=True)
    o_ref[...] = (x * lax.rsqrt(ms + EPS) * g_ref[...]).astype(o_ref.dtype)


def rms_cast(x, g):
    m, d = x.shape
    tr = min(m, 512)
    return pl.pallas_call(
        _rms_kernel,
        grid=(m // tr,),
        in_specs=[pl.BlockSpec((tr, d), lambda i: (i, 0)), pl.BlockSpec((1, d), lambda i: (0, 0))],
        out_specs=pl.BlockSpec((tr, d), lambda i: (i, 0)),
        out_shape=jax.ShapeDtypeStruct((m, d), BF16),
        compiler_params=_cparams(("parallel",)),
        name="rms_cast",
    )(x, g.reshape(1, d))


def _mm_kernel(x_ref, w_ref, o_ref):
    o_ref[...] = _dot(x_ref[...], w_ref[0]).astype(o_ref.dtype)


def _mm_res_kernel(x_ref, w_ref, r_ref, o_ref):
    o_ref[...] = (r_ref[...] + _dot(x_ref[...], w_ref[0])).astype(o_ref.dtype)


def matmul(x, w, *, bn, layer=0, res=None, out_dtype=F32, name="mm"):
    m, k = x.shape
    if w.ndim == 2:
        w = w[None]
    n = w.shape[2]
    bm = min(m, 1024)
    in_specs = [pl.BlockSpec((bm, k), lambda i, j: (i, 0)), pl.BlockSpec((1, k, bn), lambda i, j: (layer, 0, j))]
    args = [x, w]
    kern = _mm_kernel
    if res is not None:
        in_specs.append(pl.BlockSpec((bm, bn), lambda i, j: (i, j)))
        args.append(res)
        kern = _mm_res_kernel
    return pl.pallas_call(
        kern,
        grid=(m // bm, n // bn),
        in_specs=in_specs,
        out_specs=pl.BlockSpec((bm, bn), lambda i, j: (i, j)),
        out_shape=jax.ShapeDtypeStruct((m, n), out_dtype),
        compiler_params=_cparams(("parallel", "parallel")),
        name=name,
    )(*args)


WS_BM = 1024


def _mm_ws_kernel(*refs, n_parts, has_res, w_t):
    xp, xs, w_ref = refs[:n_parts], refs[n_parts:2 * n_parts], refs[2 * n_parts]
    pos = 2 * n_parts + 1
    if has_res:
        r_ref, rs_ref = refs[pos], refs[pos + 1]
        pos += 2
    o_ref, os_ref, wb_ref = refs[pos], refs[pos + 1], refs[pos + 2]

    def mm(parts):
        acc, k0 = None, 0
        for p in parts:
            kp = p.shape[1]
            t = _dot(p[...], wb_ref[k0:k0 + kp, :])
            acc = t if acc is None else acc + t
            k0 += kp
        return acc

    @pl.when(pl.program_id(1) == 0)
    def _():
        wb_ref[...] = (w_ref[0].T if w_t else w_ref[0]).astype(BF16)
        ys = mm(xs)
        if has_res:
            ys = rs_ref[...] + ys
        os_ref[...] = ys.astype(os_ref.dtype)

    y = mm(xp)
    if has_res:
        y = r_ref[...] + y
    o_ref[...] = y.astype(o_ref.dtype)


def matmul_ws(xp_parts, xs_parts, w, layer, n_cols, *, bn, w_t=False, res=None, name="mm_ws"):
    mp = xp_parts[0].shape[0]
    ms = xs_parts[0].shape[0]
    k = w.shape[2] if w_t else w.shape[1]
    n_parts = len(xp_parts)
    in_specs = [pl.BlockSpec((WS_BM, p.shape[1]), lambda j, i: (i, 0)) for p in xp_parts]
    in_specs += [pl.BlockSpec((ms, p.shape[1]), lambda j, i: (0, 0)) for p in xs_parts]
    in_specs.append(pl.BlockSpec((1, bn, k), lambda j, i: (layer, j, 0)) if w_t
                    else pl.BlockSpec((1, k, bn), lambda j, i: (layer, 0, j)))
    args = list(xp_parts) + list(xs_parts) + [w]
    if res is not None:
        in_specs += [pl.BlockSpec((WS_BM, bn), lambda j, i: (i, j)), pl.BlockSpec((ms, bn), lambda j, i: (0, j))]
        args += list(res)
    return pl.pallas_call(
        functools.partial(_mm_ws_kernel, n_parts=n_parts, has_res=res is not None, w_t=w_t),
        grid=(n_cols // bn, mp // WS_BM),
        in_specs=in_specs,
        out_specs=[pl.BlockSpec((WS_BM, bn), lambda j, i: (i, j)), pl.BlockSpec((ms, bn), lambda j, i: (0, j))],
        out_shape=[jax.ShapeDtypeStruct((mp, n_cols), F32), jax.ShapeDtypeStruct((ms, n_cols), F32)],
        scratch_shapes=[pltpu.VMEM((k, bn), BF16)],
        compiler_params=_cparams(("parallel", "arbitrary")),
        name=name,
    )(*args)


def _mm_narrow_kernel(x_ref, wt_ref, b_ref, o_ref, *, softplus):
    y = _dot_nt(x_ref[...], wt_ref[...].astype(BF16))
    if softplus:
        y = jax.nn.softplus(y + b_ref[...])
    o_ref[...] = y


def matmul_narrow(x, wt, bias=None):
    m, k = x.shape
    n = wt.shape[0]
    bm = min(m, 1024)
    b = jnp.zeros((LANES,), F32) if bias is None else jnp.pad(bias.astype(F32), (0, LANES - n))
    return pl.pallas_call(
        functools.partial(_mm_narrow_kernel, softplus=bias is not None),
        grid=(m // bm,),
        in_specs=[pl.BlockSpec((bm, k), lambda i: (i, 0)), pl.BlockSpec((LANES, k), lambda i: (0, 0)),
                  pl.BlockSpec((1, LANES), lambda i: (0, 0))],
        out_specs=pl.BlockSpec((bm, LANES), lambda i: (i, 0)),
        out_shape=jax.ShapeDtypeStruct((m, LANES), F32),
        compiler_params=_cparams(("parallel",)),
        name="mm_narrow",
    )(x, jnp.pad(wt, ((0, LANES - n), (0, 0))), b.reshape(1, LANES))


def _head_rms(x, gain):
    ms = jnp.mean(x * x, axis=-1, keepdims=True)
    return x * lax.rsqrt(ms + EPS) * gain


def _rope(x, cosf, sinf):
    return x * cosf + pltpu.roll(x, HEAD_DIM // 2, axis=1) * sinf


def _even_prep_kernel(p_ref, gl_ref, cos_ref, sin_ref, vg_ref, qn_ref, kn_ref,
                      u_ref, v_ref, q_ref, qr_ref, rc_ref, rs_ref, rw_ref, gate_ref, *mxu_refs):
    cosf = cos_ref[...]
    sinf = sin_ref[...]
    u_ref[...] = jax.nn.gelu(p_ref[:, 0:A_WIDTH]).astype(u_ref.dtype)
    for g in range(A_GROUPS):
        sl = slice(g * LANES, (g + 1) * LANES)
        vg = jax.nn.gelu(p_ref[:, A_WIDTH + g * LANES:A_WIDTH + (g + 1) * LANES])
        v_ref[:, sl] = _head_rms(vg, vg_ref[:, sl])
    qgain = qn_ref[...]
    for h in range(NSA_HEADS):
        sl = slice(h * LANES, (h + 1) * LANES)
        q = _head_rms(p_ref[:, 2 * A_WIDTH + h * LANES:2 * A_WIDTH + (h + 1) * LANES], qgain)
        q_ref[:, sl] = q.astype(q_ref.dtype)
        qr_ref[:, sl] = _rope(q, cosf, sinf).astype(qr_ref.dtype)
    base = 2 * A_WIDTH + B_WIDTH
    for which, row_ref in enumerate((rc_ref, rs_ref, rw_ref)):
        gain = kn_ref[which:which + 1, :]
        off = base + which * KV_COLS
        k16_ref, vt_ref = (mxu_refs[which - 1], mxu_refs[which + 1]) if (mxu_refs and which > 0) else (None, None)
        stored_ref = (mxu_refs[4] if which == 0 else row_ref) if mxu_refs else None
        tr = p_ref.shape[0]
        for h in range(KV_HEADS):
            sl = slice(h * LANES, (h + 1) * LANES)
            k = _head_rms(p_ref[:, off + h * LANES:off + (h + 1) * LANES], gain)
            if which > 0:
                k = _rope(k, cosf, sinf)
            vals = p_ref[:, off + (KV_HEADS + h) * LANES:off + (KV_HEADS + h + 1) * LANES]
            if stored_ref is not None:
                stored_ref[pl.ds(h, tr, stride=KV_SLOTS), :] = k
                stored_ref[pl.ds(KV_HEADS + h, tr, stride=KV_SLOTS), :] = vals
            if stored_ref is not row_ref:
                row_ref[:, sl] = k
                row_ref[:, (KV_HEADS + h) * LANES:(KV_HEADS + h + 1) * LANES] = vals
            if k16_ref is not None:
                k16_ref[:, sl] = k.astype(BF16)
                vt_ref[0, h] = vals.T.astype(BF16)
    gate_ref[...] = jax.nn.sigmoid(gl_ref[...])


def even_prep(proj, gl, cosf, sinf, v_gain, q_gain, k_gain, *, for_mxu):
    m = proj.shape[0]
    tr = min(m, SEL_KC)
    row = lambda w: pl.BlockSpec((tr, w), lambda i: (i, 0))
    full = lambda r, w: pl.BlockSpec((r, w), lambda i: (0, 0))
    out_shapes = [
        jax.ShapeDtypeStruct((m, A_WIDTH), BF16),
        jax.ShapeDtypeStruct((m, A_WIDTH), F32),
        jax.ShapeDtypeStruct((m, B_WIDTH), BF16),
        jax.ShapeDtypeStruct((m, B_WIDTH), BF16),
        jax.ShapeDtypeStruct((m, KV_COLS), F32),
        jax.ShapeDtypeStruct((m, KV_COLS), F32),
        jax.ShapeDtypeStruct((m, KV_COLS), F32),
        jax.ShapeDtypeStruct((m, LANES), F32),
    ]
    out_specs = [row(s.shape[1]) for s in out_shapes]
    if for_mxu:
        assert tr == SEL_KC
        stored = jax.ShapeDtypeStruct((m * KV_SLOTS, LANES), F32)
        stored_spec = pl.BlockSpec((tr * KV_SLOTS, LANES), lambda i: (i, 0))
        out_shapes[5:7] = [stored, stored]
        out_specs[5:7] = [stored_spec, stored_spec]
        out_shapes += [jax.ShapeDtypeStruct((m, KV_HEADS * LANES), BF16)] * 2
        out_specs += [row(KV_HEADS * LANES)] * 2
        out_shapes += [jax.ShapeDtypeStruct((m // tr, KV_HEADS, HEAD_DIM, tr), BF16)] * 2
        out_specs += [pl.BlockSpec((1, KV_HEADS, HEAD_DIM, tr), lambda i: (i, 0, 0, 0))] * 2
        out_shapes.append(stored)
        out_specs.append(stored_spec)
    return pl.pallas_call(
        _even_prep_kernel,
        grid=(m // tr,),
        in_specs=[row(E_MAIN), row(LANES), row(LANES), row(LANES),
                  full(1, A_WIDTH), full(1, LANES), full(3, LANES)],
        out_specs=out_specs,
        out_shape=out_shapes,
        compiler_params=_cparams(("parallel",)),
        name="even_prep",
    )(proj, gl, cosf, sinf, v_gain.reshape(1, A_WIDTH), q_gain.reshape(1, LANES), k_gain)


def _gmlp_kernel(u_ref, v_ref, ws_ref, bst_ref, o_ref):
    tril = _iota((A_CHUNK, A_CHUNK), 0) >= _iota((A_CHUNK, A_CHUNK), 1)
    for g in range(A_GROUPS):
        sl = slice(g * LANES, (g + 1) * LANES)
        wm = jnp.where(tril, ws_ref[g], 0.0).astype(BF16)
        s = _dot(wm, v_ref[:, sl].astype(BF16)) + bst_ref[:, g:g + 1]
        o_ref[:, sl] = (u_ref[:, sl].astype(F32) * s).astype(o_ref.dtype)


def gmlp(u, v, ws, bs):
    m = u.shape[0]
    row = pl.BlockSpec((A_CHUNK, A_WIDTH), lambda i: (i, 0))
    return pl.pallas_call(
        _gmlp_kernel,
        grid=(m // A_CHUNK,),
        in_specs=[row, row,
                  pl.BlockSpec((A_GROUPS, A_CHUNK, A_CHUNK), lambda i: (0, 0, 0)),
                  pl.BlockSpec((A_CHUNK, A_GROUPS), lambda i: (0, 0))],
        out_specs=row,
        out_shape=jax.ShapeDtypeStruct((m, A_WIDTH), BF16),
        compiler_params=_cparams(("parallel",)),
        name="gmlp",
    )(u, v, ws, bs.T)


def _compress_kernel(x_ref, pool_ref, o_ref):
    pool = pool_ref[...]
    e = jnp.exp(pool - jnp.max(pool, axis=0, keepdims=True))
    w = e / jnp.sum(e, axis=0, keepdims=True)
    x = x_ref[...]
    nb = x.shape[0] // CMP_BLOCK
    o_ref[...] = jnp.sum(x.reshape(nb, CMP_BLOCK, KV_COLS) * w[None], axis=1)


def compress(rows, pool):
    m = rows.shape[0]
    tr = min(m, 1024)
    pool_cols = jnp.concatenate([jnp.repeat(pool.T, LANES, axis=1)] * 2, axis=1)
    return pl.pallas_call(
        _compress_kernel,
        grid=(m // tr,),
        in_specs=[pl.BlockSpec((tr, KV_COLS), lambda i: (i, 0)),
                  pl.BlockSpec((CMP_BLOCK, KV_COLS), lambda i: (0, 0))],
        out_specs=pl.BlockSpec((tr // CMP_BLOCK, KV_COLS), lambda i: (i, 0)),
        out_shape=jax.ShapeDtypeStruct((m // CMP_BLOCK, KV_COLS), F32),
        compiler_params=_cparams(("parallel",)),
        name="compress",
    )(rows, pool_cols)


NSA_TQ = 256
SEL_KC = 512
EXP2_SCALE = ATTN_SCALE * LOG2E


def _kv_gates(gate_ref, h):
    g = gate_ref[...]
    return jnp.where(h == 0, g[:, 0:GATES_PER_KV], g[:, GATES_PER_KV:2 * GATES_PER_KV])


def _pad_rows(x, rows):
    return jnp.concatenate([x, jnp.zeros((rows - x.shape[0], x.shape[1]), x.dtype)], axis=0)


def _nsa_prompt_kernel(qn_ref, qr_ref, kc_ref, vc_ref, ks_ref, vts_ref, kw_ref, vtw_ref, gate_ref, o_ref,
                       acc_ref, osel_ref, *, seq):
    h = pl.program_id(1)
    t = pl.program_id(2)
    nb = seq // CMP_BLOCK
    tq = NSA_TQ
    qpos_t = t * tq + _iota((1, tq), 1)
    blk_t = _iota((nb, 1), 0)

    heads = lambda ref: jnp.concatenate([ref[:, g * LANES:(g + 1) * LANES] for g in range(NSA_GROUP)], axis=0)
    wide = lambda x: jnp.concatenate([x] * NSA_GROUP, axis=1)
    qn4 = heads(qn_ref)
    qr4 = heads(qr_ref)

    kc = kc_ref[0].astype(BF16)
    vc_t = _pad_rows(vc_ref[0], LANES).T.astype(BF16)
    valid_ct = ((blk_t + 1) * CMP_BLOCK - 1) <= t * tq + _iota((1, NSA_GROUP * tq), 1) % tq
    st = jnp.where(valid_ct, _dot_nt(kc, qn4) * ATTN_SCALE, NEG)
    et = jnp.where(valid_ct, jnp.exp(st - jnp.max(st, axis=0, keepdims=True)), 0.0)
    pt = et / jnp.maximum(jnp.sum(et, axis=0, keepdims=True), TINY)
    o_cmp = _dot(vc_t, _pad_rows(pt, LANES).astype(BF16))
    imp_t = pt[:, 0:tq]
    for g in range(1, NSA_GROUP):
        imp_t = imp_t + pt[:, g * tq:(g + 1) * tq]

    cur_t = qpos_t // CMP_BLOCK
    forced = (blk_t == 0) | (blk_t == cur_t)
    score = jnp.where(forced, FORCE, jnp.where(blk_t > cur_t, NEG, imp_t))
    rank = jnp.zeros((nb, tq), jnp.int32)
    for j in range(nb):
        row = score[j:j + 1, :]
        beats = (row > score) | ((row == score) & (blk_t > j))
        rank = rank + beats.astype(jnp.int32)
    sel = _pad_rows((rank < min(SEL_TOPN, nb)).astype(F32), LANES).astype(BF16)

    def attend(lo, hi, k_ref, vt_ref, bias_fn):
        acc_ref[...] = jnp.zeros(acc_ref.shape, F32)

        def body(c, carry):
            m, l = carry
            start = pl.multiple_of(c * SEL_KC, SEL_KC)
            s = _dot_nt(k_ref[pl.ds(start, SEL_KC), :], qr4) + wide(bias_fn(c))
            m_new = jnp.maximum(m, jnp.max(s, axis=0, keepdims=True))
            p = jnp.exp2((s - m_new) * EXP2_SCALE)
            alpha = jnp.exp2((m - m_new) * EXP2_SCALE)
            acc_ref[...] = alpha * acc_ref[...] + _dot(vt_ref[c, 0], p.astype(BF16))
            return m_new, alpha * l + jnp.sum(p, axis=0, keepdims=True)

        init = (jnp.full((1, NSA_GROUP * tq), NEG, F32), jnp.zeros((1, NSA_GROUP * tq), F32))
        return lax.fori_loop(lo, hi, body, init)[1]

    def kpos_col(c):
        return c * SEL_KC + _iota((SEL_KC, 1), 0)

    def sel_bias(c):
        expand = ((c * SEL_KC + _iota((SEL_KC, LANES), 0)) // CMP_BLOCK == _iota((SEL_KC, LANES), 1)).astype(BF16)
        chosen = _dot(expand, sel) > 0.5
        return jnp.where(chosen & (kpos_col(c) <= qpos_t), 0.0, NEG)

    def win_bias(c):
        kpos = kpos_col(c)
        return jnp.where((kpos <= qpos_t) & (kpos >= qpos_t - WINDOW), 0.0, NEG)

    hi = (t * tq + tq - 1) // SEL_KC + 1
    l_sel = attend(0, hi, ks_ref, vts_ref, sel_bias)
    osel_ref[...] = acc_ref[...] / l_sel
    l_win = attend(jnp.maximum(t * tq - WINDOW, 0) // SEL_KC, hi, kw_ref, vtw_ref, win_bias)
    o_win = acc_ref[...] / l_win

    g_all = gate_ref[...]
    gate_t = jnp.where(h == 0, g_all, pltpu.roll(g_all, LANES - GATES_PER_KV, axis=1)).T
    for g in range(NSA_GROUP):
        cs = slice(g * tq, (g + 1) * tq)
        out_t = (gate_t[3 * g:3 * g + 1] * o_cmp[:, cs] + gate_t[3 * g + 1:3 * g + 2] * osel_ref[:, cs]
                 + gate_t[3 * g + 2:3 * g + 3] * o_win[:, cs])
        o_ref[:, g * LANES:(g + 1) * LANES] = out_t.T.astype(o_ref.dtype)


def nsa_prompt(qn, qr, kcmp, ks16, vts16, kw16, vtw16, gate, bsz, seq):
    nt = seq // NSA_TQ
    nb = seq // CMP_BLOCK
    nc = seq // SEL_KC
    assert nb <= LANES and seq % SEL_KC == 0
    qspec = pl.BlockSpec((NSA_TQ, NSA_GROUP * LANES), lambda b, h, t: (b * nt + t, h))
    cmp_k = pl.BlockSpec((1, nb, LANES), lambda b, h, t: (b, 0, h))
    cmp_v = pl.BlockSpec((1, nb, LANES), lambda b, h, t: (b, 0, KV_HEADS + h))
    key = pl.BlockSpec((seq, LANES), lambda b, h, t: (b, h))
    val_t = pl.BlockSpec((nc, 1, HEAD_DIM, SEL_KC), lambda b, h, t: (b, h, 0, 0))
    kc3 = kcmp.reshape(bsz, nb, KV_COLS)
    return pl.pallas_call(
        functools.partial(_nsa_prompt_kernel, seq=seq),
        grid=(bsz, KV_HEADS, nt),
        in_specs=[qspec, qspec, cmp_k, cmp_v, key, val_t, key, val_t,
                  pl.BlockSpec((NSA_TQ, LANES), lambda b, h, t: (b * nt + t, 0))],
        out_specs=qspec,
        out_shape=jax.ShapeDtypeStruct((bsz * seq, B_WIDTH), BF16),
        scratch_shapes=[pltpu.VMEM((HEAD_DIM, NSA_GROUP * NSA_TQ), F32),
                        pltpu.VMEM((HEAD_DIM, NSA_GROUP * NSA_TQ), F32)],
        compiler_params=_cparams(("parallel", "parallel", "arbitrary")),
        name="nsa_prompt",
    )(qn, qr, kc3, kc3, ks16, vts16, kw16, vtw16, gate)


PAGES_PER_STEP = 16
PAGE_ROWS = PAGE_SIZE * KV_SLOTS
BLOCK_ROWS = CMP_BLOCK * KV_SLOTS
BLOCKS_PER_PAGE = PAGE_SIZE // CMP_BLOCK


def _pool_past_kernel(pt_ref, *refs, pps):
    page_refs, pool_ref, o_ref = refs[:pps], refs[pps], refs[pps + 1]
    tiles = BLOCK_ROWS // SUBLANES
    pool = pool_ref[...].reshape(tiles, SUBLANES, LANES)
    fold = lambda a, op: op(a, pltpu.roll(a, KV_SLOTS, axis=0))
    mx = fold(jnp.max(pool, axis=0), jnp.maximum)
    e = jnp.exp(pool - mx[None])
    w = e / fold(jnp.sum(e, axis=0), jnp.add)[None]
    first = _iota((SUBLANES, LANES), 0) < KV_SLOTS
    for k, ref in enumerate(page_refs):
        x = ref[0].reshape(BLOCKS_PER_PAGE, tiles, SUBLANES, LANES)
        sums = [fold(jnp.sum(x[b] * w, axis=0), jnp.add) for b in range(BLOCKS_PER_PAGE)]
        o_ref[0, k * SUBLANES:(k + 1) * SUBLANES, :] = jnp.where(first, sums[0], sums[1])


def pool_past(cache, layer_off, page_table, pool):
    bsz, n_pages = page_table.shape
    pps = min(PAGES_PER_STEP, n_pages)
    steps = n_pages // pps
    assert BLOCKS_PER_PAGE * KV_SLOTS == SUBLANES
    pool_rows = jnp.broadcast_to(jnp.tile(pool.T, (1, 2)).reshape(BLOCK_ROWS, 1), (BLOCK_ROWS, LANES))

    def page_spec(k):
        return pl.BlockSpec((1, PAGE_ROWS, LANES),
                            lambda b, s, pt: (layer_off + pt[b * n_pages + s * pps + k], 0, 0))

    grid_spec = pltpu.PrefetchScalarGridSpec(
        num_scalar_prefetch=1,
        grid=(bsz, steps),
        in_specs=[page_spec(k) for k in range(pps)]
        + [pl.BlockSpec((BLOCK_ROWS, LANES), lambda b, s, pt: (0, 0))],
        out_specs=pl.BlockSpec((1, pps * SUBLANES, LANES), lambda b, s, pt: (b, s, 0)),
    )
    return pl.pallas_call(
        functools.partial(_pool_past_kernel, pps=pps),
        grid_spec=grid_spec,
        out_shape=jax.ShapeDtypeStruct((bsz, n_pages * SUBLANES, LANES), F32),
        compiler_params=_cparams(("parallel", "arbitrary")),
        name="pool_past",
    )(page_table.reshape(-1), *([cache] * pps), pool_rows)


def _sample_select_kernel(q_ref, kc_ref, oc_ref, idx_ref, *, dec):
    rows = kc_ref.shape[1]
    x = kc_ref[0].astype(BF16)
    lane = _iota((dec, rows), 1)
    slot = _iota((dec, SEL_TOPN), 1)
    for h in range(KV_HEADS):
        is_key = lane % KV_SLOTS == h
        imp = jnp.zeros((dec, rows), F32)
        for g in range(NSA_GROUP):
            hd = h * NSA_GROUP + g
            q = q_ref[0, :, hd * LANES:(hd + 1) * LANES]
            s = jnp.where(is_key, _dot_nt(q, x) * ATTN_SCALE, NEG)
            e = jnp.where(is_key, jnp.exp(s - jnp.max(s, axis=-1, keepdims=True)), 0.0)
            p = e / jnp.maximum(jnp.sum(e, axis=-1, keepdims=True), TINY)
            imp = imp + p
            oc_ref[0, :, hd * LANES:(hd + 1) * LANES] = _dot(pltpu.roll(p, KV_HEADS, axis=1).astype(BF16), x)
        score = jnp.where(is_key & (lane >= KV_SLOTS), imp, -jnp.inf)
        picked = jnp.zeros((dec, SEL_TOPN), jnp.int32)
        for r in range(SEL_TOPN - 2):
            best = jnp.max(score, axis=-1, keepdims=True)
            arg = jnp.min(jnp.where(score == best, lane, rows), axis=-1, keepdims=True)
            picked = jnp.where(slot == r + 1, arg // KV_SLOTS, picked)
            score = jnp.where(lane == arg, -jnp.inf, score)
        idx_ref[0, h * dec:(h + 1) * dec, :] = picked


def sample_select(qn, kcmp_past):
    bsz, dec, _ = qn.shape
    rows = kcmp_past.shape[1]
    return pl.pallas_call(
        functools.partial(_sample_select_kernel, dec=dec),
        grid=(bsz,),
        in_specs=[pl.BlockSpec((1, dec, B_WIDTH), lambda b: (b, 0, 0)),
                  pl.BlockSpec((1, rows, LANES), lambda b: (b, 0, 0))],
        out_specs=[pl.BlockSpec((1, dec, B_WIDTH), lambda b: (b, 0, 0)),
                   pl.BlockSpec((1, KV_HEADS * dec, SEL_TOPN), lambda b: (b, 0, 0))],
        out_shape=[jax.ShapeDtypeStruct((bsz, dec, B_WIDTH), F32),
                   jax.ShapeDtypeStruct((bsz, KV_HEADS * dec, SEL_TOPN), jnp.int32)],
        compiler_params=_cparams(("parallel",)),
        name="sample_select",
    )(qn, kcmp_past)


def _rows_to_tile(rows):
    r = _iota((SUBLANES, rows[0].shape[1]), 0)
    out = jnp.zeros((SUBLANES, rows[0].shape[1]), rows[0].dtype)
    for i, x in enumerate(rows):
        out = jnp.where(r == i, jnp.broadcast_to(x, out.shape), out)
    return out


def _sample_attn_kernel(idx_ref, pt_ref, *refs, dec):
    n = SEL_TOPN
    blk_refs = refs[:n]
    (qr_ref, ksn_ref, vsn_ref, wbuf_ref, kwn_ref, vwn_ref, gate_ref, oc_ref, o_ref) = refs[n:]
    h = pl.program_id(1)
    qi = pl.program_id(2)
    q8 = _rows_to_tile([qr_ref[0, 0, :, g * LANES:(g + 1) * LANES].astype(F32)
                        for g in range(NSA_GROUP)]).astype(BF16)
    newpos = _iota((1, dec), 1)

    def attend(x_old, mask_old, k_new, v_new):
        s_old = jnp.where(mask_old, _dot_nt(q8, x_old) * ATTN_SCALE, NEG)
        mask_new = newpos <= qi
        s_new = jnp.where(mask_new, _dot_nt(q8, k_new) * ATTN_SCALE, NEG)
        m = jnp.maximum(jnp.max(s_old, axis=-1, keepdims=True), jnp.max(s_new, axis=-1, keepdims=True))
        p_old = jnp.where(mask_old, jnp.exp(s_old - m), 0.0)
        p_new = jnp.where(mask_new, jnp.exp(s_new - m), 0.0)
        l = jnp.sum(p_old, axis=-1, keepdims=True) + jnp.sum(p_new, axis=-1, keepdims=True)
        pv = _dot(pltpu.roll(p_old, KV_HEADS, axis=1).astype(BF16), x_old)
        return (pv + _dot(p_new.astype(BF16), v_new)) / l

    x_sel = jnp.concatenate([r[0].astype(BF16) for r in blk_refs], axis=0)
    lane = _iota((1, n * BLOCK_ROWS), 1)
    mask_sel = (lane % KV_SLOTS == h) & (lane < (n - 1) * BLOCK_ROWS)
    o_sel = attend(x_sel, mask_sel, ksn_ref[0].astype(BF16), vsn_ref[0].astype(BF16))

    wrows = wbuf_ref.shape[1]
    wb = wrows // KV_SLOTS
    lane_w = _iota((1, wrows), 1)
    mask_win = (lane_w % KV_SLOTS == h) & (lane_w // KV_SLOTS >= qi + (wb - WINDOW))
    o_win = attend(wbuf_ref[0].astype(BF16), mask_win, kwn_ref[0].astype(BF16), vwn_ref[0].astype(BF16))

    gate = _kv_gates(gate_ref.at[0, 0], h)
    outs = []
    for g in range(NSA_GROUP):
        outs.append(gate[:, 3 * g:3 * g + 1] * oc_ref[0, 0, :, g * LANES:(g + 1) * LANES]
                    + gate[:, 3 * g + 1:3 * g + 2] * o_sel[g:g + 1]
                    + gate[:, 3 * g + 2:3 * g + 3] * o_win[g:g + 1])
    o_ref[0, 0] = jnp.concatenate(outs, axis=1).astype(o_ref.dtype)


def sample_attn(idx, page_table, cache_sel, sel_off, cache_win, win_off, qr, rows_s, rows_w, gate, o_cmp):
    bsz, dec, _ = rows_s.shape
    n_pages = page_table.shape[1]
    grp = NSA_GROUP * LANES

    def phys(b, h, q, k, idx_r, pt_r):
        blk = idx_r[((b * KV_HEADS + h) * dec + q) * SEL_TOPN + k]
        return (sel_off + pt_r[b * n_pages + blk // BLOCKS_PER_PAGE]) * BLOCKS_PER_PAGE + blk % BLOCKS_PER_PAGE

    def bspec(k):
        return pl.BlockSpec((1, BLOCK_ROWS, LANES),
                            lambda b, h, q, idx_r, pt_r: (phys(b, h, q, k, idx_r, pt_r), 0, 0))

    qmap = lambda b, h, q, idx_r, pt_r: (b, q, 0, h)
    newk = pl.BlockSpec((1, dec, LANES), lambda b, h, q, idx_r, pt_r: (b, 0, h))
    newv = pl.BlockSpec((1, dec, LANES), lambda b, h, q, idx_r, pt_r: (b, 0, KV_HEADS + h))
    wbuf = pl.BlockSpec((1, cache_win.shape[1], LANES), lambda b, h, q, idx_r, pt_r: (win_off + b, 0, 0))
    grid_spec = pltpu.PrefetchScalarGridSpec(
        num_scalar_prefetch=2,
        grid=(bsz, KV_HEADS, dec),
        in_specs=[bspec(k) for k in range(SEL_TOPN)]
        + [pl.BlockSpec((1, 1, 1, grp), qmap), newk, newv, wbuf, newk, newv,
           pl.BlockSpec((1, 1, 1, LANES), lambda b, h, q, idx_r, pt_r: (b, q, 0, 0)),
           pl.BlockSpec((1, 1, 1, grp), qmap)],
        out_specs=pl.BlockSpec((1, 1, 1, grp), qmap),
    )
    q4 = lambda a: a.reshape(bsz, dec, 1, a.shape[-1])
    return pl.pallas_call(
        functools.partial(_sample_attn_kernel, dec=dec),
        grid_spec=grid_spec,
        out_shape=jax.ShapeDtypeStruct((bsz, dec, 1, B_WIDTH), BF16),
        compiler_params=_cparams(("parallel", "parallel", "arbitrary")),
        name="sample_attn",
    )(idx.reshape(-1), page_table.reshape(-1), *([cache_sel] * SEL_TOPN),
      q4(qr), rows_s, rows_s, cache_win, rows_w, rows_w, q4(gate), q4(o_cmp))


def _shift_rows(x, prev, k):
    xr = pltpu.roll(x, k, axis=0)
    pr = pltpu.roll(prev, k, axis=0)
    top = jnp.where(_iota((SUBLANES, 1), 0) < k, pr, xr[:SUBLANES])
    return jnp.concatenate([top, xr[SUBLANES:]], axis=0)


def _taps_kernel(*refs, width, act):
    x_refs, w_ref, b_ref, o_ref = refs[:width], refs[width], refs[width + 1], refs[width + 2]
    y = b_ref[...] + x_refs[0][...] * w_ref[0:1, :]
    for k in range(1, width):
        y = y + x_refs[k][...] * w_ref[k:k + 1, :]
    if act == "silu":
        o_ref[...] = y * jax.nn.sigmoid(y)
    else:
        o_ref[...] = y


def conv_taps(views, w, b, *, act, tile):
    width = len(views)
    m, n = views[0].shape
    col = pl.BlockSpec((m, tile), lambda j: (0, j))
    return pl.pallas_call(
        functools.partial(_taps_kernel, width=width, act=act),
        grid=(n // tile,),
        in_specs=[col] * width + [pl.BlockSpec((width, tile), lambda j: (0, j)),
                                  pl.BlockSpec((1, tile), lambda j: (0, j))],
        out_specs=col,
        out_shape=jax.ShapeDtypeStruct((m, n), F32),
        compiler_params=_cparams(("parallel",)),
        name="conv_taps_" + act,
    )(*views, w, b.reshape(1, -1))


def _glu_kernel(g_ref, u_ref, o_ref):
    g = g_ref[...]
    o_ref[...] = (g * jax.nn.sigmoid(g) * u_ref[...]).astype(o_ref.dtype)


def glu(hc, half, tile):
    m = hc.shape[0]
    nt = half // tile
    return pl.pallas_call(
        _glu_kernel,
        grid=(nt,),
        in_specs=[pl.BlockSpec((m, tile), lambda j: (0, j)), pl.BlockSpec((m, tile), lambda j: (0, nt + j))],
        out_specs=pl.BlockSpec((m, tile), lambda j: (0, j)),
        out_shape=jax.ShapeDtypeStruct((m, half), BF16),
        compiler_params=_cparams(("parallel",)),
        name="glu",
    )(hc, hc)


SSD_GPS = SSM_GROUPS


def _ssd_kernel(*refs, n_chunks, fuse_conv):
    c = pl.program_id(2)
    q = SSM_CHUNK
    e_heads = SSM_GROUP_HEADS
    gw = SSM_GROUP_WIDTH
    if fuse_conv:
        (xr_ref, br_ref, cr_ref, z_ref, cwx_ref, cwb_ref, cwc_ref, cbx_ref, cbb_ref, cbc_ref,
         dt_ref, dtt_ref, al_ref, alt_ref, dsk_ref, ng_ref, h0_ref,
         y_ref, ht_ref, tx_ref, tb_ref, tc_ref, st_ref, hx_ref, hb_ref, hc_ref) = refs

        def conv_act(raw_ref, hist_ref, cw_ref, cb_ref, tail_ref):
            x = raw_ref[...]
            prev = jnp.where(c == 0, 0.0, hist_ref[...])
            y = x * cw_ref[0, SSM_CONV - 1:SSM_CONV, :] + cb_ref[0]
            for d in range(1, SSM_CONV):
                y = y + _shift_rows(x, prev, d) * cw_ref[0, SSM_CONV - 1 - d:SSM_CONV - d, :]
            last = x[q - SUBLANES:, :]
            hist_ref[...] = last
            tail_ref[0] = last
            return y * jax.nn.sigmoid(y)

        xs_all = conv_act(xr_ref, hx_ref, cwx_ref, cbx_ref, tx_ref)
        bm_all = conv_act(br_ref, hb_ref, cwb_ref, cbb_ref, tb_ref)
        cm_all = conv_act(cr_ref, hc_ref, cwc_ref, cbc_ref, tc_ref)
    else:
        (xs_ref, bm_ref, cm_ref, z_ref, dt_ref, dtt_ref, al_ref, alt_ref, dsk_ref, ng_ref, h0_ref,
         y_ref, ht_ref, st_ref) = refs
        xs_all, bm_all, cm_all = xs_ref[...], bm_ref[...], cm_ref[...]

    @pl.when(c == 0)
    def _():
        st_ref[...] = h0_ref[0]

    ri = _iota((q, q), 0)
    ci = _iota((q, q), 1)
    tril = ri >= ci
    tril_b = tril.astype(BF16)
    triu_b = (ri <= ci).astype(BF16)
    expand = (_iota((e_heads, gw), 0) == _iota((e_heads, gw), 1) // SSM_HEAD_DIM).astype(BF16)
    lane = _iota((q, 2 * SSM_HEAD_DIM), 1)

    for k in range(SSD_GPS):
        gs = slice(k * gw, (k + 1) * gw)
        ns = slice(k * SSM_STATE, (k + 1) * SSM_STATE)
        hs = slice(k * e_heads, (k + 1) * e_heads)
        dt = dt_ref[:, hs]
        da = dt * (-jnp.exp(al_ref[:, hs]))
        da_t = dtt_ref[hs, :] * (-jnp.exp(alt_ref[hs, :]))
        acs = _dot3_r(tril_b, da)
        acs_t = _dot3_l(da_t, triu_b)
        tot = acs[q - 1:q, :]

        xs = xs_all[:, gs]
        xdt = xs * _dot2_l(dt, expand)
        xdt_b = xdt.astype(BF16)
        bm = bm_all[:, ns]
        cm_b = cm_all[:, ns].astype(BF16)
        cb = _dot_nt(cm_b, bm.astype(BF16))

        y_parts = []
        for pair in range(e_heads // 2):
            cols = slice(pair * 2 * SSM_HEAD_DIM, (pair + 1) * 2 * SSM_HEAD_DIM)
            res = []
            for e in (2 * pair, 2 * pair + 1):
                seg = jnp.where(tril, acs[:, e:e + 1] - acs_t[e:e + 1, :], NEG)
                mix = (cb * jnp.exp(seg)).astype(BF16)
                res.append(_dot(mix, xdt_b[:, cols]))
            y_parts.append(jnp.where(lane < SSM_HEAD_DIM, res[0], res[1]))
        y = jnp.concatenate(y_parts, axis=1)

        state = st_ref[:, gs]
        ex = _dot2_l(jnp.concatenate([jnp.exp(acs), jnp.exp(tot - acs)], axis=0), expand)
        y = y + _dot(cm_b, state.astype(BF16)) * ex[0:q]
        contrib = _dot(bm.T.astype(BF16), (xdt * ex[q:2 * q]).astype(BF16))
        st_ref[:, gs] = state * ex[q - 1:q] + contrib

        y = y + xs * dsk_ref[:, gs]
        z = z_ref[:, gs]
        y = y * (z * jax.nn.sigmoid(z))
        ms = jnp.mean(y * y, axis=-1, keepdims=True)
        y_ref[:, gs] = (y * lax.rsqrt(ms + EPS) * ng_ref[:, gs]).astype(y_ref.dtype)

    @pl.when(c == n_chunks - 1)
    def _():
        ht_ref[0] = st_ref[...]


def ssd(xbc, xbc_col0, zsrc, dt, a_log, d_skip, norm_g, h0, bsz, seq, conv=None):
    nc = seq // SSM_CHUNK
    gps = SSD_GPS
    gw = SSM_GROUP_WIDTH * gps
    nw = SSM_STATE * gps
    e = SSM_GROUP_HEADS
    m = bsz * seq
    assert gps == SSM_GROUPS
    dt_t = dt.T
    al = a_log.astype(F32).reshape(1, SSM_HEADS)
    al_t = a_log.astype(F32).reshape(SSM_HEADS, 1)
    dsk = jnp.repeat(d_skip.astype(F32), SSM_HEAD_DIM).reshape(1, SSM_INNER)
    bc = SSM_GROUPS * SSM_STATE
    x_off = xbc_col0 // gw
    b_off = (xbc_col0 + SSM_INNER) // nw
    c_off = (xbc_col0 + SSM_INNER + bc) // nw
    rowmap = lambda col: (lambda b, g, c: (b * nc + c, col(g)))
    xbc_specs = [pl.BlockSpec((SSM_CHUNK, gw), rowmap(lambda g: x_off + g)),
                 pl.BlockSpec((SSM_CHUNK, nw), rowmap(lambda g: b_off + g)),
                 pl.BlockSpec((SSM_CHUNK, nw), rowmap(lambda g: c_off + g)),
                 pl.BlockSpec((SSM_CHUNK, gw), rowmap(lambda g: g))]
    rest_specs = [pl.BlockSpec((SSM_CHUNK, SSM_HEADS), lambda b, g, c: (b * nc + c, 0)),
                  pl.BlockSpec((SSM_HEADS, SSM_CHUNK), lambda b, g, c: (0, b * nc + c)),
                  pl.BlockSpec((1, SSM_HEADS), lambda b, g, c: (0, 0)),
                  pl.BlockSpec((SSM_HEADS, 1), lambda b, g, c: (0, 0)),
                  pl.BlockSpec((1, gw), lambda b, g, c: (0, g)),
                  pl.BlockSpec((1, gw), lambda b, g, c: (0, g)),
                  pl.BlockSpec((1, SSM_STATE, gw), lambda b, g, c: (b, 0, g))]
    rest_args = [dt, dt_t, al, al_t, dsk, norm_g.astype(F32).reshape(1, SSM_INNER), h0]
    out_specs = [pl.BlockSpec((SSM_CHUNK, gw), rowmap(lambda g: g)),
                 pl.BlockSpec((1, SSM_STATE, gw), lambda b, g, c: (b, 0, g))]
    out_shape = [jax.ShapeDtypeStruct((m, SSM_INNER), BF16),
                 jax.ShapeDtypeStruct((bsz, SSM_STATE, SSM_INNER), F32)]
    scratch = [pltpu.VMEM((SSM_STATE, gw), F32)]
    conv_specs, conv_args = [], []
    if conv is not None:
        cw, cb, layer = conv
        cb3 = cb.reshape(cb.shape[0], 1, -1)
        offs = ((gw, 0), (nw, SSM_INNER // nw), (nw, (SSM_INNER + bc) // nw))
        conv_specs = [pl.BlockSpec((1, SSM_CONV, w), lambda b, g, c, o=o: (layer, 0, o + g)) for w, o in offs]
        conv_specs += [pl.BlockSpec((1, 1, w), lambda b, g, c, o=o: (layer, 0, o + g)) for w, o in offs]
        conv_args = [cw, cw, cw, cb3, cb3, cb3]
        for w, total in ((gw, SSM_INNER), (nw, bc), (nw, bc)):
            out_specs.append(pl.BlockSpec((1, SUBLANES, w), lambda b, g, c: (b, 0, g)))
            out_shape.append(jax.ShapeDtypeStruct((bsz, SUBLANES, total), F32))
            scratch.append(pltpu.VMEM((SUBLANES, w), F32))
    return pl.pallas_call(
        functools.partial(_ssd_kernel, n_chunks=nc, fuse_conv=conv is not None),
        grid=(bsz, SSM_GROUPS // gps, nc),
        in_specs=xbc_specs + conv_specs + rest_specs,
        out_specs=out_specs,
        out_shape=out_shape,
        scratch_shapes=scratch,
        compiler_params=_cparams(("parallel", "parallel", "arbitrary")),
        name="ssd",
    )(xbc, xbc, xbc, zsrc, *conv_args, *rest_args)


def _state_to_cols(h):
    b = h.shape[0]
    return h.transpose(0, 3, 1, 2).reshape(b, SSM_STATE, SSM_INNER)


def _state_from_cols(s):
    b = s.shape[0]
    return s.reshape(b, SSM_STATE, SSM_HEADS, SSM_HEAD_DIM).transpose(0, 2, 3, 1)


FFN_BM = 1024
FFN_SUB = 256
FFN_TILE = 512
FFN_TILES = D_FF // FFN_TILE


def _conv3(h, prev, cw_ref, cb_ref):
    return (_shift_rows(h, prev, 2) * cw_ref[0, 0:1, :] + _shift_rows(h, prev, 1) * cw_ref[0, 1:2, :]
            + h * cw_ref[0, 2:3, :] + cb_ref[0])


def _ffn_up_kernel(x_ref, xs_ref, wg_ref, wu_ref, cwg_ref, cwu_ref, cbg_ref, cbu_ref,
                   act_ref, tg_ref, tu_ref, sg_ref, su_ref, wgb_ref, wub_ref, cg_ref, cu_ref, *, tiles_per_seq):
    i = pl.program_id(1)

    @pl.when(i == 0)
    def _():
        wgb_ref[...] = wg_ref[0].astype(BF16)
        wub_ref[...] = wu_ref[0].astype(BF16)
        sg_ref[...] = _dot(xs_ref[...], wgb_ref[...])
        su_ref[...] = _dot(xs_ref[...], wub_ref[...])

    @pl.when(i % tiles_per_seq == 0)
    def _():
        cg_ref[...] = jnp.zeros_like(cg_ref)
        cu_ref[...] = jnp.zeros_like(cu_ref)

    for r in range(FFN_BM // FFN_SUB):
        rs = slice(r * FFN_SUB, (r + 1) * FFN_SUB)
        x = x_ref[rs, :]
        hg = _dot(x, wgb_ref[...])
        hu = _dot(x, wub_ref[...])
        g = _conv3(hg, cg_ref[...], cwg_ref, cbg_ref)
        u = _conv3(hu, cu_ref[...], cwu_ref, cbu_ref)
        act_ref[rs, :] = (g * jax.nn.sigmoid(g) * u).astype(act_ref.dtype)
        cg_ref[...] = hg[FFN_SUB - SUBLANES:, :]
        cu_ref[...] = hu[FFN_SUB - SUBLANES:, :]
    tg_ref[0] = cg_ref[...]
    tu_ref[0] = cu_ref[...]


def ffn_up(xp, xs, w_up, conv_w, conv_b, layer, bsz, seq):
    m, k = xp.shape
    ms = xs.shape[0]
    tiles_per_seq = seq // FFN_BM
    nt = FFN_TILES
    wspec = lambda off: pl.BlockSpec((1, k, FFN_TILE), lambda j, i: (layer, 0, off + j))
    cwspec = lambda off: pl.BlockSpec((1, FFN_CONV, FFN_TILE), lambda j, i: (layer, 0, off + j))
    cbspec = lambda off: pl.BlockSpec((1, 1, FFN_TILE), lambda j, i: (layer, 0, off + j))
    tail = pl.BlockSpec((1, SUBLANES, FFN_TILE), lambda j, i: (i // tiles_per_seq, 0, j))
    samp = pl.BlockSpec((ms, FFN_TILE), lambda j, i: (0, j))
    cb3 = conv_b.reshape(conv_b.shape[0], 1, -1)
    carry = pltpu.VMEM((SUBLANES, FFN_TILE), F32)
    return pl.pallas_call(
        functools.partial(_ffn_up_kernel, tiles_per_seq=tiles_per_seq),
        grid=(nt, m // FFN_BM),
        in_specs=[pl.BlockSpec((FFN_BM, k), lambda j, i: (i, 0)), pl.BlockSpec((ms, k), lambda j, i: (0, 0)),
                  wspec(0), wspec(nt), cwspec(0), cwspec(nt), cbspec(0), cbspec(nt)],
        out_specs=[pl.BlockSpec((FFN_BM, FFN_TILE), lambda j, i: (i, j)), tail, tail, samp, samp],
        out_shape=[jax.ShapeDtypeStruct((m, D_FF), BF16),
                   jax.ShapeDtypeStruct((bsz, SUBLANES, D_FF), F32), jax.ShapeDtypeStruct((bsz, SUBLANES, D_FF), F32),
                   jax.ShapeDtypeStruct((ms, D_FF), F32), jax.ShapeDtypeStruct((ms, D_FF), F32)],
        scratch_shapes=[pltpu.VMEM((k, FFN_TILE), BF16), pltpu.VMEM((k, FFN_TILE), BF16), carry, carry],
        compiler_params=_cparams(("parallel", "arbitrary")),
        name="ffn_up",
    )(xp, xs, w_up, w_up, conv_w, conv_w, cb3, cb3)


def _rope_tables(pos):
    half = HEAD_DIM // 2
    inv = ROPE_THETA ** (-jnp.arange(half, dtype=F32) / half)
    ang = pos.astype(F32)[:, None] * inv[None, :]
    cos, sin = jnp.cos(ang), jnp.sin(ang)
    return jnp.concatenate([cos, cos], axis=1), jnp.concatenate([-sin, sin], axis=1)


def _kv_out(rows, bsz):
    return rows.reshape(bsz, -1, 2, KV_HEADS, HEAD_DIM)


def kernel(x_prompt, x_sample, cache_kv_cmp, cache_kv_sel, cache_kv_win, state_ssm_conv, state_ssm, state_ffn_conv, page_table, norm_mix, norm_ffn, w_in_even, w_out_even, gmlp_v_norm, gmlp_ws, gmlp_bs, q_norm, k_norm, cmp_pool, w_in_odd, ssm_conv_w, ssm_conv_b, ssm_dt_bias, ssm_a_log, ssm_d, ssm_norm, w_out_odd, ffn_w_up, ffn_conv_w, ffn_conv_b, ffn_w_down):
    bp, sp, _ = x_prompt.shape
    bs, ss, _ = x_sample.shape
    depth = norm_mix.shape[0]
    n_pool = cache_kv_cmp.shape[1]
    n_pages = page_table.shape[1]
    past_len = n_pages * PAGE_SIZE
    wb = cache_kv_win.shape[2]
    mp, ms = bp * sp, bs * ss

    cos_p, sin_p = _rope_tables(jnp.arange(sp, dtype=jnp.int32))
    cos_p, sin_p = jnp.tile(cos_p, (bp, 1)), jnp.tile(sin_p, (bp, 1))
    cos_s, sin_s = _rope_tables(past_len + jnp.arange(ss, dtype=jnp.int32))
    cos_s, sin_s = jnp.tile(cos_s, (bs, 1)), jnp.tile(sin_s, (bs, 1))

    cache_c = cache_kv_cmp.reshape(-1, PAGE_ROWS, LANES)
    cache_s = cache_kv_sel.reshape(-1, BLOCK_ROWS, LANES)
    cache_w = cache_kv_win.reshape(-1, wb * KV_SLOTS, LANES)

    w_down16 = ffn_w_down.astype(BF16)
    w_out_odd16 = w_out_odd.astype(BF16)
    w_in_even_t = jnp.swapaxes(w_in_even, 1, 2)
    w_in_odd_t = jnp.swapaxes(w_in_odd, 1, 2)

    hp = x_prompt.reshape(mp, D_MODEL)
    hs = x_sample.reshape(ms, D_MODEL)
    outs = {k: [] for k in ("p_cmp", "p_sel", "p_win", "s_cmp", "s_sel", "s_win", "s_v",
                            "p_sconv", "p_sst", "s_sconv", "s_sst", "p_fconv", "s_fconv")}

    for layer in range(depth):
        i = layer // 2
        xp = rms_cast(hp, norm_mix[layer])
        xs = rms_cast(hs, norm_mix[layer])
        if layer % 2 == 0:
            w_gate = w_in_even_t[i, E_MAIN:]
            proj_p, proj_s = matmul_ws([xp], [xs], w_in_even_t, i, E_MAIN, bn=768, w_t=True, name="even_in")
            gl = matmul_narrow(xp, w_gate)
            u, v, qn, qr, rc, rs, rw, gate, ks16, kw16, vts16, vtw16, rc_stored = even_prep(
                proj_p, gl, cos_p, sin_p, gmlp_v_norm[i], q_norm[i], k_norm[i], for_mxu=True)
            a_p = gmlp(u, v, gmlp_ws[i], gmlp_bs[i])
            kcmp = compress(rc, cmp_pool[i])
            b_p = nsa_prompt(qn, qr, kcmp, ks16, vts16, kw16, vtw16, gate, bp, sp)
            outs["p_cmp"].append(_kv_out(rc_stored, bp))
            outs["p_sel"].append(_kv_out(rs, bp))
            outs["p_win"].append(_kv_out(rw, bp)[:, sp - min(WINDOW, sp):])
            gl = matmul_narrow(xs, w_gate)
            u, v, qn, qr, rc, rs, rw, gate = even_prep(
                proj_s, gl, cos_s, sin_s, gmlp_v_norm[i], q_norm[i], k_norm[i], for_mxu=False)
            lpad = ((0, 0), (0, A_CHUNK - ss), (0, 0))
            a_s = gmlp(jnp.pad(u.reshape(bs, ss, -1), lpad).reshape(bs * A_CHUNK, -1),
                       jnp.pad(v.reshape(bs, ss, -1), lpad).reshape(bs * A_CHUNK, -1),
                       gmlp_ws[i], gmlp_bs[i]).reshape(bs, A_CHUNK, -1)[:, :ss].reshape(ms, -1)
            kc_past = pool_past(cache_c, i * n_pool, page_table, cmp_pool[i])
            o_cmp, idx = sample_select(qn.reshape(bs, ss, -1), kc_past)
            b_s = sample_attn(idx, page_table, cache_s, i * n_pool, cache_w, i * bs,
                              qr.reshape(bs, ss, -1), rs.reshape(bs, ss, -1), rw.reshape(bs, ss, -1),
                              gate.reshape(bs, ss, -1), o_cmp).reshape(ms, B_WIDTH)
            outs["s_cmp"].append(_kv_out(rc, bs))
            outs["s_sel"].append(_kv_out(rs, bs))
            outs["s_win"].append(_kv_out(rw, bs))
            outs["s_v"].append(v.reshape(bs, ss, A_WIDTH))
            hp, hs = matmul_ws([a_p, b_p], [a_s, b_s], w_out_even, i, D_MODEL, bn=1024, res=(hp, hs), name="even_out")
        else:
            zx = SSM_INNER + SSM_CONV_DIM
            w_dt = w_in_odd_t[i, zx:]
            proj_p, proj_s = matmul_ws([xp], [xs], w_in_odd_t, i, zx, bn=1024, w_t=True, name="odd_in")
            dt = matmul_narrow(xp, w_dt, ssm_dt_bias[i])[:, :SSM_HEADS]
            h0 = jnp.zeros((bp, SSM_STATE, SSM_INNER), F32)
            y, ht, *tails = ssd(proj_p, SSM_INNER, proj_p, dt, ssm_a_log[i], ssm_d[i], ssm_norm[i], h0, bp, sp,
                                conv=(ssm_conv_w, ssm_conv_b, i))
            tail = jnp.concatenate(tails, axis=2)
            hp = matmul(y, w_out_odd16, layer=i, bn=512, res=hp, name="odd_out")
            outs["p_sconv"].append(tail[:, SUBLANES - (SSM_CONV - 1):])
            outs["p_sst"].append(_state_from_cols(ht))
            dt = matmul_narrow(xs, w_dt, ssm_dt_bias[i])[:, :SSM_HEADS]
            xin = jnp.concatenate([state_ssm_conv[i], proj_s[:, SSM_INNER:].reshape(bs, ss, -1)], axis=1)
            views = [xin[:, k:k + ss].reshape(ms, -1) for k in range(SSM_CONV)]
            xbc = conv_taps(views, ssm_conv_w[i], ssm_conv_b[i], act="silu", tile=1024)
            cpad = ((0, 0), (0, SSM_CHUNK - ss), (0, 0))
            padrows = lambda a: jnp.pad(a.reshape(bs, ss, -1), cpad).reshape(bs * SSM_CHUNK, -1)
            y, ht = ssd(padrows(xbc), 0, padrows(proj_s[:, :SSM_INNER]), padrows(dt), ssm_a_log[i], ssm_d[i], ssm_norm[i],
                        _state_to_cols(state_ssm[i].astype(F32)), bs, SSM_CHUNK)
            y = y.reshape(bs, SSM_CHUNK, -1)[:, :ss].reshape(ms, -1)
            hs = matmul(y, w_out_odd16, layer=i, bn=1024, res=hs, name="odd_out_s")
            outs["s_sconv"].append(xin[:, ss:])
            outs["s_sst"].append(_state_from_cols(ht))
        xp = rms_cast(hp, norm_ffn[layer])
        xs = rms_cast(hs, norm_ffn[layer])
        act, tail_g, tail_u, hu_g, hu_u = ffn_up(xp, xs, ffn_w_up, ffn_conv_w, ffn_conv_b, layer, bp, sp)
        hp = matmul(act, w_down16, layer=layer, bn=512, res=hp, name="ffn_down")
        outs["p_fconv"].append(jnp.concatenate([tail_g, tail_u], axis=2)[:, SUBLANES - (FFN_CONV - 1):])
        hu = jnp.concatenate([hu_g, hu_u], axis=1)
        xin = jnp.concatenate([state_ffn_conv[layer], hu.reshape(bs, ss, -1)], axis=1)
        views = [xin[:, k:k + ss].reshape(ms, -1) for k in range(FFN_CONV)]
        hc = conv_taps(views, ffn_conv_w[layer], ffn_conv_b[layer], act="none", tile=1024)
        hs = matmul(glu(hc, D_FF, 512), w_down16, layer=layer, bn=512, res=hs, name="ffn_down_s")
        outs["s_fconv"].append(xin[:, ss:])

    st = lambda k: jnp.stack(outs[k])
    return (hp.reshape(bp, sp, D_MODEL), hs.reshape(bs, ss, D_MODEL), st("p_cmp"), st("p_sel"), st("p_win"),
            st("p_sconv"), st("p_sst"), st("p_fconv"), st("s_cmp"), st("s_sel"), st("s_win"), st("s_v"),
            st("s_sconv"), st("s_sst"), st("s_fconv"))
```

```python
import functools

import jax
import jax.numpy as jnp
from jax import lax
from jax.experimental import pallas as pl
from jax.experimental.pallas import tpu as pltpu

F32 = jnp.float32
BF16 = jnp.bfloat16

D_MODEL = 2048
A_WIDTH = D_MODEL // 2
A_GROUPS = 8
A_CHUNK = 128
NSA_HEADS = 8
HEAD_DIM = 128
KV_HEADS = 2
NSA_GROUP = NSA_HEADS // KV_HEADS
B_WIDTH = NSA_HEADS * HEAD_DIM
KV_COLS = 2 * KV_HEADS * HEAD_DIM
KV_SLOTS = 2 * KV_HEADS
CMP_BLOCK = 64
SEL_TOPN = 16
WINDOW = 512
PAGE_SIZE = 128
ROPE_THETA = 10000.0
ATTN_SCALE = HEAD_DIM ** -0.5
LOG2E = 1.4426950408889634
SSM_INNER = 2 * D_MODEL
SSM_HEAD_DIM = 64
SSM_HEADS = SSM_INNER // SSM_HEAD_DIM
SSM_GROUPS = 8
SSM_GROUP_HEADS = SSM_HEADS // SSM_GROUPS
SSM_GROUP_WIDTH = SSM_INNER // SSM_GROUPS
SSM_STATE = 128
SSM_CONV = 4
SSM_CHUNK = 128
SSM_CONV_DIM = SSM_INNER + 2 * SSM_GROUPS * SSM_STATE
D_FF = 5632
FFN_CONV = 3
EPS = 1e-6
NEG = -1e30
FORCE = 1e4
TINY = 1e-30

VMEM_LIMIT_BYTES = 56 * 1024 * 1024
LANES = 128
SUBLANES = 8

E_MAIN = 2 * A_WIDTH + B_WIDTH + 3 * KV_COLS
N_GATES = 3 * NSA_HEADS
GATES_PER_KV = 3 * NSA_GROUP


def _cparams(sem):
    return pltpu.CompilerParams(dimension_semantics=sem, vmem_limit_bytes=VMEM_LIMIT_BYTES)


def _dot(a, b):
    return jnp.dot(a, b, preferred_element_type=F32)


def _dot_nt(a, b):
    return lax.dot_general(a, b, (((1,), (1,)), ((), ())), preferred_element_type=F32)


def _iota(shape, dim):
    return lax.broadcasted_iota(jnp.int32, shape, dim)


def _split3(x):
    hi = x.astype(BF16)
    r1 = x - hi.astype(F32)
    mid = r1.astype(BF16)
    lo = (r1 - mid.astype(F32)).astype(BF16)
    return hi, mid, lo


def _dot3_l(x, m):
    hi, mid, lo = _split3(x)
    return _dot(hi, m) + _dot(mid, m) + _dot(lo, m)


def _dot3_r(m, x):
    hi, mid, lo = _split3(x)
    return _dot(m, hi) + _dot(m, mid) + _dot(m, lo)


def _dot2_l(x, m):
    hi = x.astype(BF16)
    return _dot(hi, m) + _dot((x - hi.astype(F32)).astype(BF16), m)


def _rms_kernel(x_ref, g_ref, o_ref):
    x = x_ref[...]
    ms = jnp.mean(x * x, axis=-1, keepdims=True)
    o_ref[...] = (x * lax.rsqrt(ms + EPS) * g_ref[...]).astype(o_ref.dtype)


def rms_cast(x, g):
    m, d = x.shape
    tr = min(m, 512)
    return pl.pallas_call(
        _rms_kernel,
        grid=(m // tr,),
        in_specs=[pl.BlockSpec((tr, d), lambda i: (i, 0)), pl.BlockSpec((1, d), lambda i: (0, 0))],
        out_specs=pl.BlockSpec((tr, d), lambda i: (i, 0)),
        out_shape=jax.ShapeDtypeStruct((m, d), BF16),
        compiler_params=_cparams(("parallel",)),
        name="rms_cast",
    )(x, g.reshape(1, d))


def _mm_kernel(x_ref, w_ref, o_ref):
    o_ref[...] = _dot(x_ref[...], w_ref[0]).astype(o_ref.dtype)


def _mm_res_kernel(x_ref, w_ref, r_ref, o_ref):
    o_ref[...] = (r_ref[...] + _dot(x_ref[...], w_ref[0])).astype(o_ref.dtype)


def matmul(x, w, *, bn, layer=0, res=None, out_dtype=F32, name="mm"):
    m, k = x.shape
    if w.ndim == 2:
        w = w[None]
    n = w.shape[2]
    bm = min(m, 1024)
    in_specs = [pl.BlockSpec((bm, k), lambda i, j: (i, 0)), pl.BlockSpec((1, k, bn), lambda i, j: (layer, 0, j))]
    args = [x, w]
    kern = _mm_kernel
    if res is not None:
        in_specs.append(pl.BlockSpec((bm, bn), lambda i, j: (i, j)))
        args.append(res)
        kern = _mm_res_kernel
    return pl.pallas_call(
        kern,
        grid=(m // bm, n // bn),
        in_specs=in_specs,
        out_specs=pl.BlockSpec((bm, bn), lambda i, j: (i, j)),
        out_shape=jax.ShapeDtypeStruct((m, n), out_dtype),
        compiler_params=_cparams(("parallel", "parallel")),
        name=name,
    )(*args)


WS_BM = 1024


def _mm_ws_kernel(*refs, n_parts, has_res, w_t):
    xp, xs, w_ref = refs[:n_parts], refs[n_parts:2 * n_parts], refs[2 * n_parts]
    pos = 2 * n_parts + 1
    if has_res:
        r_ref, rs_ref = refs[pos], refs[pos + 1]
        pos += 2
    o_ref, os_ref, wb_ref = refs[pos], refs[pos + 1], refs[pos + 2]

    def mm(parts):
        acc, k0 = None, 0
        for p in parts:
            kp = p.shape[1]
            t = _dot(p[...], wb_ref[k0:k0 + kp, :])
            acc = t if acc is None else acc + t
            k0 += kp
        return acc

    @pl.when(pl.program_id(1) == 0)
    def _():
        wb_ref[...] = (w_ref[0].T if w_t else w_ref[0]).astype(BF16)
        ys = mm(xs)
        if has_res:
            ys = rs_ref[...] + ys
        os_ref[...] = ys.astype(os_ref.dtype)

    y = mm(xp)
    if has_res:
        y = r_ref[...] + y
    o_ref[...] = y.astype(o_ref.dtype)


def matmul_ws(xp_parts, xs_parts, w, layer, n_cols, *, bn, w_t=False, res=None, name="mm_ws"):
    mp = xp_parts[0].shape[0]
    ms = xs_parts[0].shape[0]
    k = w.shape[2] if w_t else w.shape[1]
    n_parts = len(xp_parts)
    in_specs = [pl.BlockSpec((WS_BM, p.shape[1]), lambda j, i: (i, 0)) for p in xp_parts]
    in_specs += [pl.BlockSpec((ms, p.shape[1]), lambda j, i: (0, 0)) for p in xs_parts]
    in_specs.append(pl.BlockSpec((1, bn, k), lambda j, i: (layer, j, 0)) if w_t
                    else pl.BlockSpec((1, k, bn), lambda j, i: (layer, 0, j)))
    args = list(xp_parts) + list(xs_parts) + [w]
    if res is not None:
        in_specs += [pl.BlockSpec((WS_BM, bn), lambda j, i: (i, j)), pl.BlockSpec((ms, bn), lambda j, i: (0, j))]
        args += list(res)
    return pl.pallas_call(
        functools.partial(_mm_ws_kernel, n_parts=n_parts, has_res=res is not None, w_t=w_t),
        grid=(n_cols // bn, mp // WS_BM),
        in_specs=in_specs,
        out_specs=[pl.BlockSpec((WS_BM, bn), lambda j, i: (i, j)), pl.BlockSpec((ms, bn), lambda j, i: (0, j))],
        out_shape=[jax.ShapeDtypeStruct((mp, n_cols), F32), jax.ShapeDtypeStruct((ms, n_cols), F32)],
        scratch_shapes=[pltpu.VMEM((k, bn), BF16)],
        compiler_params=_cparams(("parallel", "arbitrary")),
        name=name,
    )(*args)


def _mm_narrow_kernel(x_ref, wt_ref, b_ref, o_ref, *, softplus):
    y = _dot_nt(x_ref[...], wt_ref[...].astype(BF16))
    if softplus:
        y = jax.nn.softplus(y + b_ref[...])
    o_ref[...] = y


def matmul_narrow(x, wt, bias=None):
    m, k = x.shape
    n = wt.shape[0]
    bm = min(m, 1024)
    b = jnp.zeros((LANES,), F32) if bias is None else jnp.pad(bias.astype(F32), (0, LANES - n))
    return pl.pallas_call(
        functools.partial(_mm_narrow_kernel, softplus=bias is not None),
        grid=(m // bm,),
        in_specs=[pl.BlockSpec((bm, k), lambda i: (i, 0)), pl.BlockSpec((LANES, k), lambda i: (0, 0)),
                  pl.BlockSpec((1, LANES), lambda i: (0, 0))],
        out_specs=pl.BlockSpec((bm, LANES), lambda i: (i, 0)),
        out_shape=jax.ShapeDtypeStruct((m, LANES), F32),
        compiler_params=_cparams(("parallel",)),
        name="mm_narrow",
    )(x, jnp.pad(wt, ((0, LANES - n), (0, 0))), b.reshape(1, LANES))


def _head_rms(x, gain):
    ms = jnp.mean(x * x, axis=-1, keepdims=True)
    return x * lax.rsqrt(ms + EPS) * gain


def _rope(x, cosf, sinf):
    return x * cosf + pltpu.roll(x, HEAD_DIM // 2, axis=1) * sinf


def _even_prep_kernel(p_ref, gl_ref, cos_ref, sin_ref, vg_ref, qn_ref, kn_ref,
                      u_ref, v_ref, q_ref, qr_ref, rc_ref, rs_ref, rw_ref, gate_ref, *mxu_refs):
    cosf = cos_ref[...]
    sinf = sin_ref[...]
    u_ref[...] = jax.nn.gelu(p_ref[:, 0:A_WIDTH]).astype(u_ref.dtype)
    for g in range(A_GROUPS):
        sl = slice(g * LANES, (g + 1) * LANES)
        vg = jax.nn.gelu(p_ref[:, A_WIDTH + g * LANES:A_WIDTH + (g + 1) * LANES])
        v_ref[:, sl] = _head_rms(vg, vg_ref[:, sl])
    qgain = qn_ref[...]
    for h in range(NSA_HEADS):
        sl = slice(h * LANES, (h + 1) * LANES)
        q = _head_rms(p_ref[:, 2 * A_WIDTH + h * LANES:2 * A_WIDTH + (h + 1) * LANES], qgain)
        q_ref[:, sl] = q.astype(q_ref.dtype)
        q_rot = _rope(q, cosf, sinf)
        if mxu_refs:
            q_rot = q_rot * EXP2_SCALE
        qr_ref[:, sl] = q_rot.astype(qr_ref.dtype)
    base = 2 * A_WIDTH + B_WIDTH
    for which, row_ref in enumerate((rc_ref, rs_ref, rw_ref)):
        gain = kn_ref[which:which + 1, :]
        off = base + which * KV_COLS
        k16_ref, vt_ref = (mxu_refs[which - 1], mxu_refs[which + 1]) if (mxu_refs and which > 0) else (None, None)
        stored_ref = (mxu_refs[4] if which == 0 else row_ref) if mxu_refs else None
        tr = p_ref.shape[0]
        for h in range(KV_HEADS):
            sl = slice(h * LANES, (h + 1) * LANES)
            k = _head_rms(p_ref[:, off + h * LANES:off + (h + 1) * LANES], gain)
            if which > 0:
                k = _rope(k, cosf, sinf)
            vals = p_ref[:, off + (KV_HEADS + h) * LANES:off + (KV_HEADS + h + 1) * LANES]
            if stored_ref is not None:
                stored_ref[pl.ds(h, tr, stride=KV_SLOTS), :] = k
                stored_ref[pl.ds(KV_HEADS + h, tr, stride=KV_SLOTS), :] = vals
            if stored_ref is not row_ref:
                row_ref[:, sl] = k
                row_ref[:, (KV_HEADS + h) * LANES:(KV_HEADS + h + 1) * LANES] = vals
            if k16_ref is not None:
                k16_ref[:, sl] = k.astype(BF16)
                vt_ref[0, h] = vals.T.astype(BF16)
    gate_ref[...] = jax.nn.sigmoid(gl_ref[...])


def even_prep(proj, gl, cosf, sinf, v_gain, q_gain, k_gain, *, for_mxu):
    m = proj.shape[0]
    tr = min(m, SEL_KC)
    row = lambda w: pl.BlockSpec((tr, w), lambda i: (i, 0))
    full = lambda r, w: pl.BlockSpec((r, w), lambda i: (0, 0))
    out_shapes = [
        jax.ShapeDtypeStruct((m, A_WIDTH), BF16),
        jax.ShapeDtypeStruct((m, A_WIDTH), F32),
        jax.ShapeDtypeStruct((m, B_WIDTH), BF16),
        jax.ShapeDtypeStruct((m, B_WIDTH), BF16),
        jax.ShapeDtypeStruct((m, KV_COLS), F32),
        jax.ShapeDtypeStruct((m, KV_COLS), F32),
        jax.ShapeDtypeStruct((m, KV_COLS), F32),
        jax.ShapeDtypeStruct((m, LANES), F32),
    ]
    out_specs = [row(s.shape[1]) for s in out_shapes]
    if for_mxu:
        assert tr == SEL_KC
        stored = jax.ShapeDtypeStruct((m * KV_SLOTS, LANES), F32)
        stored_spec = pl.BlockSpec((tr * KV_SLOTS, LANES), lambda i: (i, 0))
        out_shapes[5:7] = [stored, stored]
        out_specs[5:7] = [stored_spec, stored_spec]
        out_shapes += [jax.ShapeDtypeStruct((m, KV_HEADS * LANES), BF16)] * 2
        out_specs += [row(KV_HEADS * LANES)] * 2
        out_shapes += [jax.ShapeDtypeStruct((m // tr, KV_HEADS, HEAD_DIM, tr), BF16)] * 2
        out_specs += [pl.BlockSpec((1, KV_HEADS, HEAD_DIM, tr), lambda i: (i, 0, 0, 0))] * 2
        out_shapes.append(stored)
        out_specs.append(stored_spec)
    return pl.pallas_call(
        _even_prep_kernel,
        grid=(m // tr,),
        in_specs=[row(E_MAIN), row(LANES), row(LANES), row(LANES),
                  full(1, A_WIDTH), full(1, LANES), full(3, LANES)],
        out_specs=out_specs,
        out_shape=out_shapes,
        compiler_params=_cparams(("parallel",)),
        name="even_prep",
    )(proj, gl, cosf, sinf, v_gain.reshape(1, A_WIDTH), q_gain.reshape(1, LANES), k_gain)


GMLP_CHUNKS = 4


def _gmlp_kernel(u_ref, v_ref, ws_ref, bst_ref, o_ref):
    tril = _iota((A_CHUNK, A_CHUNK), 0) >= _iota((A_CHUNK, A_CHUNK), 1)
    for g in range(A_GROUPS):
        sl = slice(g * LANES, (g + 1) * LANES)
        wm = jnp.where(tril, ws_ref[g], 0.0).astype(BF16)
        bias = bst_ref[:, g:g + 1]
        for c in range(u_ref.shape[0] // A_CHUNK):
            rs = slice(c * A_CHUNK, (c + 1) * A_CHUNK)
            s = _dot(wm, v_ref[rs, sl].astype(BF16)) + bias
            o_ref[rs, sl] = (u_ref[rs, sl].astype(F32) * s).astype(o_ref.dtype)


def gmlp(u, v, ws, bs):
    m = u.shape[0]
    rows = A_CHUNK * GMLP_CHUNKS if m % (A_CHUNK * GMLP_CHUNKS) == 0 else A_CHUNK
    row = pl.BlockSpec((rows, A_WIDTH), lambda i: (i, 0))
    return pl.pallas_call(
        _gmlp_kernel,
        grid=(m // rows,),
        in_specs=[row, row,
                  pl.BlockSpec((A_GROUPS, A_CHUNK, A_CHUNK), lambda i: (0, 0, 0)),
                  pl.BlockSpec((A_CHUNK, A_GROUPS), lambda i: (0, 0))],
        out_specs=row,
        out_shape=jax.ShapeDtypeStruct((m, A_WIDTH), BF16),
        compiler_params=_cparams(("parallel",)),
        name="gmlp",
    )(u, v, ws, bs.T)


def _compress_kernel(x_ref, pool_ref, o_ref):
    pool = pool_ref[...]
    e = jnp.exp(pool - jnp.max(pool, axis=0, keepdims=True))
    w = e / jnp.sum(e, axis=0, keepdims=True)
    x = x_ref[...]
    nb = x.shape[0] // CMP_BLOCK
    o_ref[...] = jnp.sum(x.reshape(nb, CMP_BLOCK, KV_COLS) * w[None], axis=1)


def compress(rows, pool):
    m = rows.shape[0]
    tr = min(m, 1024)
    pool_cols = jnp.concatenate([jnp.repeat(pool.T, LANES, axis=1)] * 2, axis=1)
    return pl.pallas_call(
        _compress_kernel,
        grid=(m // tr,),
        in_specs=[pl.BlockSpec((tr, KV_COLS), lambda i: (i, 0)),
                  pl.BlockSpec((CMP_BLOCK, KV_COLS), lambda i: (0, 0))],
        out_specs=pl.BlockSpec((tr // CMP_BLOCK, KV_COLS), lambda i: (i, 0)),
        out_shape=jax.ShapeDtypeStruct((m // CMP_BLOCK, KV_COLS), F32),
        compiler_params=_cparams(("parallel",)),
        name="compress",
    )(rows, pool_cols)


NSA_TQ = 512
SEL_KC = 512
EXP2_SCALE = ATTN_SCALE * LOG2E


def _kv_gates(gate_ref, h):
    g = gate_ref[...]
    return jnp.where(h == 0, g[:, 0:GATES_PER_KV], g[:, GATES_PER_KV:2 * GATES_PER_KV])


def _pad_rows(x, rows):
    return jnp.concatenate([x, jnp.zeros((rows - x.shape[0], x.shape[1]), x.dtype)], axis=0)


def _nsa_prompt_kernel(qn_ref, qr_ref, kc_ref, vc_ref, ks_ref, vts_ref, kw_ref, vtw_ref, gate_ref, o_ref,
                       acc_ref, osel_ref, *, seq):
    h = pl.program_id(1)
    t = pl.program_id(2)
    nb = seq // CMP_BLOCK
    tq = NSA_TQ
    qpos_t = t * tq + _iota((1, tq), 1)
    blk_t = _iota((nb, 1), 0)

    heads = lambda ref: jnp.concatenate([ref[:, g * LANES:(g + 1) * LANES] for g in range(NSA_GROUP)], axis=0)
    wide = lambda x: jnp.concatenate([x] * NSA_GROUP, axis=1)
    qn4 = heads(qn_ref)
    qr4 = heads(qr_ref)

    kc = kc_ref[0].astype(BF16)
    vc_t = _pad_rows(vc_ref[0], LANES).T.astype(BF16)
    valid_ct = ((blk_t + 1) * CMP_BLOCK - 1) <= t * tq + _iota((1, NSA_GROUP * tq), 1) % tq
    st = jnp.where(valid_ct, _dot_nt(kc, qn4) * ATTN_SCALE, NEG)
    et = jnp.where(valid_ct, jnp.exp(st - jnp.max(st, axis=0, keepdims=True)), 0.0)
    pt = et / jnp.maximum(jnp.sum(et, axis=0, keepdims=True), TINY)
    o_cmp = _dot(vc_t, _pad_rows(pt, LANES).astype(BF16))
    imp_t = pt[:, 0:tq]
    for g in range(1, NSA_GROUP):
        imp_t = imp_t + pt[:, g * tq:(g + 1) * tq]

    cur_t = qpos_t // CMP_BLOCK
    forced = (blk_t == 0) | (blk_t == cur_t)
    score = jnp.where(forced, FORCE, jnp.where(blk_t > cur_t, NEG, imp_t))
    rank = jnp.zeros((nb, tq), jnp.int32)
    for j in range(nb):
        row = score[j:j + 1, :]
        beats = (row > score) | ((row == score) & (blk_t > j))
        rank = rank + beats.astype(jnp.int32)
    sel = _pad_rows((rank < min(SEL_TOPN, nb)).astype(F32), LANES).astype(BF16)

    def attend(lo, hi, k_ref, vt_ref, bias_fn):
        acc_ref[...] = jnp.zeros(acc_ref.shape, F32)

        def body(c, carry):
            m, l = carry
            start = pl.multiple_of(c * SEL_KC, SEL_KC)
            s = _dot_nt(k_ref[pl.ds(start, SEL_KC), :], qr4) + wide(bias_fn(c))
            m_new = jnp.maximum(m, jnp.max(s, axis=0, keepdims=True))
            p = jnp.exp2(s - m_new)
            alpha = jnp.exp2(m - m_new)
            acc_ref[...] = alpha * acc_ref[...] + _dot(vt_ref[c, 0], p.astype(BF16))
            return m_new, alpha * l + jnp.sum(p, axis=0, keepdims=True)

        init = (jnp.full((1, NSA_GROUP * tq), NEG, F32), jnp.zeros((1, NSA_GROUP * tq), F32))
        return lax.fori_loop(lo, hi, body, init)[1]

    def kpos_col(c):
        return c * SEL_KC + _iota((SEL_KC, 1), 0)

    def sel_bias(c):
        expand = ((c * SEL_KC + _iota((SEL_KC, LANES), 0)) // CMP_BLOCK == _iota((SEL_KC, LANES), 1)).astype(BF16)
        chosen = _dot(expand, sel) > 0.5
        return jnp.where(chosen & (kpos_col(c) <= qpos_t), 0.0, NEG)

    def win_bias(c):
        kpos = kpos_col(c)
        return jnp.where((kpos <= qpos_t) & (kpos >= qpos_t - WINDOW), 0.0, NEG)

    hi = (t * tq + tq - 1) // SEL_KC + 1
    l_sel = attend(0, hi, ks_ref, vts_ref, sel_bias)
    osel_ref[...] = acc_ref[...] / l_sel
    l_win = attend(jnp.maximum(t * tq - WINDOW, 0) // SEL_KC, hi, kw_ref, vtw_ref, win_bias)
    o_win = acc_ref[...] / l_win

    g_all = gate_ref[...]
    gate_t = jnp.where(h == 0, g_all, pltpu.roll(g_all, LANES - GATES_PER_KV, axis=1)).T
    for g in range(NSA_GROUP):
        cs = slice(g * tq, (g + 1) * tq)
        out_t = (gate_t[3 * g:3 * g + 1] * o_cmp[:, cs] + gate_t[3 * g + 1:3 * g + 2] * osel_ref[:, cs]
                 + gate_t[3 * g + 2:3 * g + 3] * o_win[:, cs])
        o_ref[:, g * LANES:(g + 1) * LANES] = out_t.T.astype(o_ref.dtype)


def nsa_prompt(qn, qr, kcmp, ks16, vts16, kw16, vtw16, gate, bsz, seq):
    nt = seq // NSA_TQ
    nb = seq // CMP_BLOCK
    nc = seq // SEL_KC
    assert nb <= LANES and seq % SEL_KC == 0
    qspec = pl.BlockSpec((NSA_TQ, NSA_GROUP * LANES), lambda b, h, t: (b * nt + t, h))
    cmp_k = pl.BlockSpec((1, nb, LANES), lambda b, h, t: (b, 0, h))
    cmp_v = pl.BlockSpec((1, nb, LANES), lambda b, h, t: (b, 0, KV_HEADS + h))
    key = pl.BlockSpec((seq, LANES), lambda b, h, t: (b, h))
    val_t = pl.BlockSpec((nc, 1, HEAD_DIM, SEL_KC), lambda b, h, t: (b, h, 0, 0))
    kc3 = kcmp.reshape(bsz, nb, KV_COLS)
    return pl.pallas_call(
        functools.partial(_nsa_prompt_kernel, seq=seq),
        grid=(bsz, KV_HEADS, nt),
        in_specs=[qspec, qspec, cmp_k, cmp_v, key, val_t, key, val_t,
                  pl.BlockSpec((NSA_TQ, LANES), lambda b, h, t: (b * nt + t, 0))],
        out_specs=qspec,
        out_shape=jax.ShapeDtypeStruct((bsz * seq, B_WIDTH), BF16),
        scratch_shapes=[pltpu.VMEM((HEAD_DIM, NSA_GROUP * NSA_TQ), F32),
                        pltpu.VMEM((HEAD_DIM, NSA_GROUP * NSA_TQ), F32)],
        compiler_params=_cparams(("parallel", "parallel", "arbitrary")),
        name="nsa_prompt",
    )(qn, qr, kc3, kc3, ks16, vts16, kw16, vtw16, gate)


PAGES_PER_STEP = 16
PAGE_ROWS = PAGE_SIZE * KV_SLOTS
BLOCK_ROWS = CMP_BLOCK * KV_SLOTS
BLOCKS_PER_PAGE = PAGE_SIZE // CMP_BLOCK


def _pool_past_kernel(pt_ref, *refs, pps):
    page_refs, pool_ref, o_ref = refs[:pps], refs[pps], refs[pps + 1]
    tiles = BLOCK_ROWS // SUBLANES
    pool = pool_ref[...].reshape(tiles, SUBLANES, LANES)
    fold = lambda a, op: op(a, pltpu.roll(a, KV_SLOTS, axis=0))
    mx = fold(jnp.max(pool, axis=0), jnp.maximum)
    e = jnp.exp(pool - mx[None])
    w = e / fold(jnp.sum(e, axis=0), jnp.add)[None]
    first = _iota((SUBLANES, LANES), 0) < KV_SLOTS
    for k, ref in enumerate(page_refs):
        x = ref[0].reshape(BLOCKS_PER_PAGE, tiles, SUBLANES, LANES)
        sums = [fold(jnp.sum(x[b] * w, axis=0), jnp.add) for b in range(BLOCKS_PER_PAGE)]
        o_ref[0, k * SUBLANES:(k + 1) * SUBLANES, :] = jnp.where(first, sums[0], sums[1])


def pool_past(cache, layer_off, page_table, pool):
    bsz, n_pages = page_table.shape
    pps = min(PAGES_PER_STEP, n_pages)
    steps = n_pages // pps
    assert BLOCKS_PER_PAGE * KV_SLOTS == SUBLANES
    pool_rows = jnp.broadcast_to(jnp.tile(pool.T, (1, 2)).reshape(BLOCK_ROWS, 1), (BLOCK_ROWS, LANES))

    def page_spec(k):
        return pl.BlockSpec((1, PAGE_ROWS, LANES),
                            lambda b, s, pt: (layer_off + pt[b * n_pages + s * pps + k], 0, 0))

    grid_spec = pltpu.PrefetchScalarGridSpec(
        num_scalar_prefetch=1,
        grid=(bsz, steps),
        in_specs=[page_spec(k) for k in range(pps)]
        + [pl.BlockSpec((BLOCK_ROWS, LANES), lambda b, s, pt: (0, 0))],
        out_specs=pl.BlockSpec((1, pps * SUBLANES, LANES), lambda b, s, pt: (b, s, 0)),
    )
    return pl.pallas_call(
        functools.partial(_pool_past_kernel, pps=pps),
        grid_spec=grid_spec,
        out_shape=jax.ShapeDtypeStruct((bsz, n_pages * SUBLANES, LANES), F32),
        compiler_params=_cparams(("parallel", "arbitrary")),
        name="pool_past",
    )(page_table.reshape(-1), *([cache] * pps), pool_rows)


def _sample_select_kernel(q_ref, kc_ref, oc_ref, idx_ref, *, dec):
    rows = kc_ref.shape[1]
    x = kc_ref[0].astype(BF16)
    lane = _iota((dec, rows), 1)
    slot = _iota((dec, SEL_TOPN), 1)
    for h in range(KV_HEADS):
        is_key = lane % KV_SLOTS == h
        imp = jnp.zeros((dec, rows), F32)
        for g in range(NSA_GROUP):
            hd = h * NSA_GROUP + g
            q = q_ref[0, :, hd * LANES:(hd + 1) * LANES]
            s = jnp.where(is_key, _dot_nt(q, x) * ATTN_SCALE, NEG)
            e = jnp.where(is_key, jnp.exp(s - jnp.max(s, axis=-1, keepdims=True)), 0.0)
            p = e / jnp.maximum(jnp.sum(e, axis=-1, keepdims=True), TINY)
            imp = imp + p
            oc_ref[0, :, hd * LANES:(hd + 1) * LANES] = _dot(pltpu.roll(p, KV_HEADS, axis=1).astype(BF16), x)
        score = jnp.where(is_key & (lane >= KV_SLOTS), imp, -jnp.inf)
        picked = jnp.zeros((dec, SEL_TOPN), jnp.int32)
        for r in range(SEL_TOPN - 2):
            best = jnp.max(score, axis=-1, keepdims=True)
            arg = jnp.min(jnp.where(score == best, lane, rows), axis=-1, keepdims=True)
            picked = jnp.where(slot == r + 1, arg // KV_SLOTS, picked)
            score = jnp.where(lane == arg, -jnp.inf, score)
        idx_ref[0, h * dec:(h + 1) * dec, :] = picked


def sample_select(qn, kcmp_past):
    bsz, dec, _ = qn.shape
    rows = kcmp_past.shape[1]
    return pl.pallas_call(
        functools.partial(_sample_select_kernel, dec=dec),
        grid=(bsz,),
        in_specs=[pl.BlockSpec((1, dec, B_WIDTH), lambda b: (b, 0, 0)),
                  pl.BlockSpec((1, rows, LANES), lambda b: (b, 0, 0))],
        out_specs=[pl.BlockSpec((1, dec, B_WIDTH), lambda b: (b, 0, 0)),
                   pl.BlockSpec((1, KV_HEADS * dec, SEL_TOPN), lambda b: (b, 0, 0))],
        out_shape=[jax.ShapeDtypeStruct((bsz, dec, B_WIDTH), F32),
                   jax.ShapeDtypeStruct((bsz, KV_HEADS * dec, SEL_TOPN), jnp.int32)],
        compiler_params=_cparams(("parallel",)),
        name="sample_select",
    )(qn, kcmp_past)


def _rows_to_tile(rows):
    r = _iota((SUBLANES, rows[0].shape[1]), 0)
    out = jnp.zeros((SUBLANES, rows[0].shape[1]), rows[0].dtype)
    for i, x in enumerate(rows):
        out = jnp.where(r == i, jnp.broadcast_to(x, out.shape), out)
    return out


def _sample_attn_kernel(idx_ref, pt_ref, *refs, dec):
    n = SEL_TOPN
    blk_refs = refs[:n]
    (qr_ref, ksn_ref, vsn_ref, wbuf_ref, kwn_ref, vwn_ref, gate_ref, oc_ref, o_ref) = refs[n:]
    h = pl.program_id(1)
    qi = pl.program_id(2)
    q8 = _rows_to_tile([qr_ref[0, 0, :, g * LANES:(g + 1) * LANES].astype(F32)
                        for g in range(NSA_GROUP)]).astype(BF16)
    newpos = _iota((1, dec), 1)

    def attend(x_old, mask_old, k_new, v_new):
        s_old = jnp.where(mask_old, _dot_nt(q8, x_old) * ATTN_SCALE, NEG)
        mask_new = newpos <= qi
        s_new = jnp.where(mask_new, _dot_nt(q8, k_new) * ATTN_SCALE, NEG)
        m = jnp.maximum(jnp.max(s_old, axis=-1, keepdims=True), jnp.max(s_new, axis=-1, keepdims=True))
        p_old = jnp.where(mask_old, jnp.exp(s_old - m), 0.0)
        p_new = jnp.where(mask_new, jnp.exp(s_new - m), 0.0)
        l = jnp.sum(p_old, axis=-1, keepdims=True) + jnp.sum(p_new, axis=-1, keepdims=True)
        pv = _dot(pltpu.roll(p_old, KV_HEADS, axis=1).astype(BF16), x_old)
        return (pv + _dot(p_new.astype(BF16), v_new)) / l

    x_sel = jnp.concatenate([r[0].astype(BF16) for r in blk_refs], axis=0)
    lane = _iota((1, n * BLOCK_ROWS), 1)
    mask_sel = (lane % KV_SLOTS == h) & (lane < (n - 1) * BLOCK_ROWS)
    o_sel = attend(x_sel, mask_sel, ksn_ref[0].astype(BF16), vsn_ref[0].astype(BF16))

    wrows = wbuf_ref.shape[1]
    wb = wrows // KV_SLOTS
    lane_w = _iota((1, wrows), 1)
    mask_win = (lane_w % KV_SLOTS == h) & (lane_w // KV_SLOTS >= qi + (wb - WINDOW))
    o_win = attend(wbuf_ref[0].astype(BF16), mask_win, kwn_ref[0].astype(BF16), vwn_ref[0].astype(BF16))

    gate = _kv_gates(gate_ref.at[0, 0], h)
    outs = []
    for g in range(NSA_GROUP):
        outs.append(gate[:, 3 * g:3 * g + 1] * oc_ref[0, 0, :, g * LANES:(g + 1) * LANES]
                    + gate[:, 3 * g + 1:3 * g + 2] * o_sel[g:g + 1]
                    + gate[:, 3 * g + 2:3 * g + 3] * o_win[g:g + 1])
    o_ref[0, 0] = jnp.concatenate(outs, axis=1).astype(o_ref.dtype)


def sample_attn(idx, page_table, cache_sel, sel_off, cache_win, win_off, qr, rows_s, rows_w, gate, o_cmp):
    bsz, dec, _ = rows_s.shape
    n_pages = page_table.shape[1]
    grp = NSA_GROUP * LANES

    def phys(b, h, q, k, idx_r, pt_r):
        blk = idx_r[((b * KV_HEADS + h) * dec + q) * SEL_TOPN + k]
        return (sel_off + pt_r[b * n_pages + blk // BLOCKS_PER_PAGE]) * BLOCKS_PER_PAGE + blk % BLOCKS_PER_PAGE

    def bspec(k):
        return pl.BlockSpec((1, BLOCK_ROWS, LANES),
                            lambda b, h, q, idx_r, pt_r: (phys(b, h, q, k, idx_r, pt_r), 0, 0))

    qmap = lambda b, h, q, idx_r, pt_r: (b, q, 0, h)
    newk = pl.BlockSpec((1, dec, LANES), lambda b, h, q, idx_r, pt_r: (b, 0, h))
    newv = pl.BlockSpec((1, dec, LANES), lambda b, h, q, idx_r, pt_r: (b, 0, KV_HEADS + h))
    wbuf = pl.BlockSpec((1, cache_win.shape[1], LANES), lambda b, h, q, idx_r, pt_r: (win_off + b, 0, 0))
    grid_spec = pltpu.PrefetchScalarGridSpec(
        num_scalar_prefetch=2,
        grid=(bsz, KV_HEADS, dec),
        in_specs=[bspec(k) for k in range(SEL_TOPN)]
        + [pl.BlockSpec((1, 1, 1, grp), qmap), newk, newv, wbuf, newk, newv,
           pl.BlockSpec((1, 1, 1, LANES), lambda b, h, q, idx_r, pt_r: (b, q, 0, 0)),
           pl.BlockSpec((1, 1, 1, grp), qmap)],
        out_specs=pl.BlockSpec((1, 1, 1, grp), qmap),
    )
    q4 = lambda a: a.reshape(bsz, dec, 1, a.shape[-1])
    return pl.pallas_call(
        functools.partial(_sample_attn_kernel, dec=dec),
        grid_spec=grid_spec,
        out_shape=jax.ShapeDtypeStruct((bsz, dec, 1, B_WIDTH), BF16),
        compiler_params=_cparams(("parallel", "parallel", "arbitrary")),
        name="sample_attn",
    )(idx.reshape(-1), page_table.reshape(-1), *([cache_sel] * SEL_TOPN),
      q4(qr), rows_s, rows_s, cache_win, rows_w, rows_w, q4(gate), q4(o_cmp))


def _shift_rows(x, prev, k):
    xr = pltpu.roll(x, k, axis=0)
    pr = pltpu.roll(prev, k, axis=0)
    top = jnp.where(_iota((SUBLANES, 1), 0) < k, pr, xr[:SUBLANES])
    return jnp.concatenate([top, xr[SUBLANES:]], axis=0)


def _taps_kernel(*refs, width, act):
    x_refs, w_ref, b_ref, o_ref = refs[:width], refs[width], refs[width + 1], refs[width + 2]
    y = b_ref[...] + x_refs[0][...] * w_ref[0:1, :]
    for k in range(1, width):
        y = y + x_refs[k][...] * w_ref[k:k + 1, :]
    if act == "silu":
        o_ref[...] = y * jax.nn.sigmoid(y)
    else:
        o_ref[...] = y


def conv_taps(views, w, b, *, act, tile):
    width = len(views)
    m, n = views[0].shape
    col = pl.BlockSpec((m, tile), lambda j: (0, j))
    return pl.pallas_call(
        functools.partial(_taps_kernel, width=width, act=act),
        grid=(n // tile,),
        in_specs=[col] * width + [pl.BlockSpec((width, tile), lambda j: (0, j)),
                                  pl.BlockSpec((1, tile), lambda j: (0, j))],
        out_specs=col,
        out_shape=jax.ShapeDtypeStruct((m, n), F32),
        compiler_params=_cparams(("parallel",)),
        name="conv_taps_" + act,
    )(*views, w, b.reshape(1, -1))


def _glu_kernel(g_ref, u_ref, o_ref):
    g = g_ref[...]
    o_ref[...] = (g * jax.nn.sigmoid(g) * u_ref[...]).astype(o_ref.dtype)


def glu(hc, half, tile):
    m = hc.shape[0]
    nt = half // tile
    return pl.pallas_call(
        _glu_kernel,
        grid=(nt,),
        in_specs=[pl.BlockSpec((m, tile), lambda j: (0, j)), pl.BlockSpec((m, tile), lambda j: (0, nt + j))],
        out_specs=pl.BlockSpec((m, tile), lambda j: (0, j)),
        out_shape=jax.ShapeDtypeStruct((m, half), BF16),
        compiler_params=_cparams(("parallel",)),
        name="glu",
    )(hc, hc)


SSD_GPS = SSM_GROUPS


def _ssd_kernel(*refs, n_chunks, fuse_conv):
    c = pl.program_id(2)
    q = SSM_CHUNK
    e_heads = SSM_GROUP_HEADS
    gw = SSM_GROUP_WIDTH
    if fuse_conv:
        (xr_ref, br_ref, cr_ref, z_ref, cwx_ref, cwb_ref, cwc_ref, cbx_ref, cbb_ref, cbc_ref,
         dt_ref, dtt_ref, al_ref, alt_ref, dsk_ref, ng_ref, h0_ref,
         y_ref, ht_ref, tx_ref, tb_ref, tc_ref, st_ref, hx_ref, hb_ref, hc_ref) = refs

        def conv_act(raw_ref, hist_ref, cw_ref, cb_ref, tail_ref):
            x = raw_ref[...]
            prev = jnp.where(c == 0, 0.0, hist_ref[...])
            y = x * cw_ref[0, SSM_CONV - 1:SSM_CONV, :] + cb_ref[0]
            for d in range(1, SSM_CONV):
                y = y + _shift_rows(x, prev, d) * cw_ref[0, SSM_CONV - 1 - d:SSM_CONV - d, :]
            last = x[q - SUBLANES:, :]
            hist_ref[...] = last
            tail_ref[0] = last
            return y * jax.nn.sigmoid(y)

        xs_all = conv_act(xr_ref, hx_ref, cwx_ref, cbx_ref, tx_ref)
        bm_all = conv_act(br_ref, hb_ref, cwb_ref, cbb_ref, tb_ref)
        cm_all = conv_act(cr_ref, hc_ref, cwc_ref, cbc_ref, tc_ref)
    else:
        (xs_ref, bm_ref, cm_ref, z_ref, dt_ref, dtt_ref, al_ref, alt_ref, dsk_ref, ng_ref, h0_ref,
         y_ref, ht_ref, st_ref) = refs
        xs_all, bm_all, cm_all = xs_ref[...], bm_ref[...], cm_ref[...]

    @pl.when(c == 0)
    def _():
        st_ref[...] = h0_ref[0]

    ri = _iota((q, q), 0)
    ci = _iota((q, q), 1)
    tril = ri >= ci
    tril_b = tril.astype(BF16)
    triu_b = (ri <= ci).astype(BF16)
    expand = (_iota((e_heads, gw), 0) == _iota((e_heads, gw), 1) // SSM_HEAD_DIM).astype(BF16)
    lane = _iota((q, 2 * SSM_HEAD_DIM), 1)

    for k in range(SSD_GPS):
        gs = slice(k * gw, (k + 1) * gw)
        ns = slice(k * SSM_STATE, (k + 1) * SSM_STATE)
        hs = slice(k * e_heads, (k + 1) * e_heads)
        dt = dt_ref[:, hs]
        da = dt * (-jnp.exp(al_ref[:, hs]))
        da_t = dtt_ref[hs, :] * (-jnp.exp(alt_ref[hs, :]))
        acs = _dot3_r(tril_b, da)
        acs_t = _dot3_l(da_t, triu_b)
        tot = acs[q - 1:q, :]

        xs = xs_all[:, gs]
        xdt = xs * _dot2_l(dt, expand)
        xdt_b = xdt.astype(BF16)
        bm = bm_all[:, ns]
        cm_b = cm_all[:, ns].astype(BF16)
        cb = _dot_nt(cm_b, bm.astype(BF16))

        y_parts = []
        for pair in range(e_heads // 2):
            cols = slice(pair * 2 * SSM_HEAD_DIM, (pair + 1) * 2 * SSM_HEAD_DIM)
            res = []
            for e in (2 * pair, 2 * pair + 1):
                seg = jnp.where(tril, acs[:, e:e + 1] - acs_t[e:e + 1, :], NEG)
                mix = (cb * jnp.exp(seg)).astype(BF16)
                res.append(_dot(mix, xdt_b[:, cols]))
            y_parts.append(jnp.where(lane < SSM_HEAD_DIM, res[0], res[1]))
        y = jnp.concatenate(y_parts, axis=1)

        state = st_ref[:, gs]
        ex = _dot2_l(jnp.concatenate([jnp.exp(acs), jnp.exp(tot - acs)], axis=0), expand)
        y = y + _dot(cm_b, state.astype(BF16)) * ex[0:q]
        contrib = _dot(bm.T.astype(BF16), (xdt * ex[q:2 * q]).astype(BF16))
        st_ref[:, gs] = state * ex[q - 1:q] + contrib

        y = y + xs * dsk_ref[:, gs]
        z = z_ref[:, gs]
        y = y * (z * jax.nn.sigmoid(z))
        ms = jnp.mean(y * y, axis=-1, keepdims=True)
        y_ref[:, gs] = (y * lax.rsqrt(ms + EPS) * ng_ref[:, gs]).astype(y_ref.dtype)

    @pl.when(c == n_chunks - 1)
    def _():
        ht_ref[0] = st_ref[...]


def ssd(xbc, xbc_col0, zsrc, dt, a_log, d_skip, norm_g, h0, bsz, seq, conv=None):
    nc = seq // SSM_CHUNK
    gps = SSD_GPS
    gw = SSM_GROUP_WIDTH * gps
    nw = SSM_STATE * gps
    e = SSM_GROUP_HEADS
    m = bsz * seq
    assert gps == SSM_GROUPS
    dt_t = dt.T
    al = a_log.astype(F32).reshape(1, SSM_HEADS)
    al_t = a_log.astype(F32).reshape(SSM_HEADS, 1)
    dsk = jnp.repeat(d_skip.astype(F32), SSM_HEAD_DIM).reshape(1, SSM_INNER)
    bc = SSM_GROUPS * SSM_STATE
    x_off = xbc_col0 // gw
    b_off = (xbc_col0 + SSM_INNER) // nw
    c_off = (xbc_col0 + SSM_INNER + bc) // nw
    rowmap = lambda col: (lambda b, g, c: (b * nc + c, col(g)))
    xbc_specs = [pl.BlockSpec((SSM_CHUNK, gw), rowmap(lambda g: x_off + g)),
                 pl.BlockSpec((SSM_CHUNK, nw), rowmap(lambda g: b_off + g)),
                 pl.BlockSpec((SSM_CHUNK, nw), rowmap(lambda g: c_off + g)),
                 pl.BlockSpec((SSM_CHUNK, gw), rowmap(lambda g: g))]
    rest_specs = [pl.BlockSpec((SSM_CHUNK, SSM_HEADS), lambda b, g, c: (b * nc + c, 0)),
                  pl.BlockSpec((SSM_HEADS, SSM_CHUNK), lambda b, g, c: (0, b * nc + c)),
                  pl.BlockSpec((1, SSM_HEADS), lambda b, g, c: (0, 0)),
                  pl.BlockSpec((SSM_HEADS, 1), lambda b, g, c: (0, 0)),
                  pl.BlockSpec((1, gw), lambda b, g, c: (0, g)),
                  pl.BlockSpec((1, gw), lambda b, g, c: (0, g)),
                  pl.BlockSpec((1, SSM_STATE, gw), lambda b, g, c: (b, 0, g))]
    rest_args = [dt, dt_t, al, al_t, dsk, norm_g.astype(F32).reshape(1, SSM_INNER), h0]
    out_specs = [pl.BlockSpec((SSM_CHUNK, gw), rowmap(lambda g: g)),
                 pl.BlockSpec((1, SSM_STATE, gw), lambda b, g, c: (b, 0, g))]
    out_shape = [jax.ShapeDtypeStruct((m, SSM_INNER), BF16),
                 jax.ShapeDtypeStruct((bsz, SSM_STATE, SSM_INNER), F32)]
    scratch = [pltpu.VMEM((SSM_STATE, gw), F32)]
    conv_specs, conv_args = [], []
    if conv is not None:
        cw, cb, layer = conv
        cb3 = cb.reshape(cb.shape[0], 1, -1)
        offs = ((gw, 0), (nw, SSM_INNER // nw), (nw, (SSM_INNER + bc) // nw))
        conv_specs = [pl.BlockSpec((1, SSM_CONV, w), lambda b, g, c, o=o: (layer, 0, o + g)) for w, o in offs]
        conv_specs += [pl.BlockSpec((1, 1, w), lambda b, g, c, o=o: (layer, 0, o + g)) for w, o in offs]
        conv_args = [cw, cw, cw, cb3, cb3, cb3]
        for w, total in ((gw, SSM_INNER), (nw, bc), (nw, bc)):
            out_specs.append(pl.BlockSpec((1, SUBLANES, w), lambda b, g, c: (b, 0, g)))
            out_shape.append(jax.ShapeDtypeStruct((bsz, SUBLANES, total), F32))
            scratch.append(pltpu.VMEM((SUBLANES, w), F32))
    return pl.pallas_call(
        functools.partial(_ssd_kernel, n_chunks=nc, fuse_conv=conv is not None),
        grid=(bsz, SSM_GROUPS // gps, nc),
        in_specs=xbc_specs + conv_specs + rest_specs,
        out_specs=out_specs,
        out_shape=out_shape,
        scratch_shapes=scratch,
        compiler_params=_cparams(("parallel", "parallel", "arbitrary")),
        name="ssd",
    )(xbc, xbc, xbc, zsrc, *conv_args, *rest_args)


def _state_to_cols(h):
    b = h.shape[0]
    return h.transpose(0, 3, 1, 2).reshape(b, SSM_STATE, SSM_INNER)


def _state_from_cols(s):
    b = s.shape[0]
    return s.reshape(b, SSM_STATE, SSM_HEADS, SSM_HEAD_DIM).transpose(0, 2, 3, 1)


FFN_BM = 1024
FFN_SUB = 256
FFN_TILE = 512
FFN_TILES = D_FF // FFN_TILE


def _conv3(h, prev, cw_ref, cb_ref):
    return (_shift_rows(h, prev, 2) * cw_ref[0, 0:1, :] + _shift_rows(h, prev, 1) * cw_ref[0, 1:2, :]
            + h * cw_ref[0, 2:3, :] + cb_ref[0])


def _ffn_up_kernel(x_ref, xs_ref, wg_ref, wu_ref, cwg_ref, cwu_ref, cbg_ref, cbu_ref,
                   act_ref, tg_ref, tu_ref, sg_ref, su_ref, wgb_ref, wub_ref, cg_ref, cu_ref, *, tiles_per_seq):
    i = pl.program_id(1)

    @pl.when(i == 0)
    def _():
        wgb_ref[...] = wg_ref[0].astype(BF16)
        wub_ref[...] = wu_ref[0].astype(BF16)
        sg_ref[...] = _dot(xs_ref[...], wgb_ref[...])
        su_ref[...] = _dot(xs_ref[...], wub_ref[...])

    @pl.when(i % tiles_per_seq == 0)
    def _():
        cg_ref[...] = jnp.zeros_like(cg_ref)
        cu_ref[...] = jnp.zeros_like(cu_ref)

    for r in range(FFN_BM // FFN_SUB):
        rs = slice(r * FFN_SUB, (r + 1) * FFN_SUB)
        x = x_ref[rs, :]
        hg = _dot(x, wgb_ref[...])
        hu = _dot(x, wub_ref[...])
        g = _conv3(hg, cg_ref[...], cwg_ref, cbg_ref)
        u = _conv3(hu, cu_ref[...], cwu_ref, cbu_ref)
        act_ref[rs, :] = (g * jax.nn.sigmoid(g) * u).astype(act_ref.dtype)
        cg_ref[...] = hg[FFN_SUB - SUBLANES:, :]
        cu_ref[...] = hu[FFN_SUB - SUBLANES:, :]
    tg_ref[0] = cg_ref[...]
    tu_ref[0] = cu_ref[...]


def ffn_up(xp, xs, w_up, conv_w, conv_b, layer, bsz, seq):
    m, k = xp.shape
    ms = xs.shape[0]
    tiles_per_seq = seq // FFN_BM
    nt = FFN_TILES
    wspec = lambda off: pl.BlockSpec((1, k, FFN_TILE), lambda j, i: (layer, 0, off + j))
    cwspec = lambda off: pl.BlockSpec((1, FFN_CONV, FFN_TILE), lambda j, i: (layer, 0, off + j))
    cbspec = lambda off: pl.BlockSpec((1, 1, FFN_TILE), lambda j, i: (layer, 0, off + j))
    tail = pl.BlockSpec((1, SUBLANES, FFN_TILE), lambda j, i: (i // tiles_per_seq, 0, j))
    samp = pl.BlockSpec((ms, FFN_TILE), lambda j, i: (0, j))
    cb3 = conv_b.reshape(conv_b.shape[0], 1, -1)
    carry = pltpu.VMEM((SUBLANES, FFN_TILE), F32)
    return pl.pallas_call(
        functools.partial(_ffn_up_kernel, tiles_per_seq=tiles_per_seq),
        grid=(nt, m // FFN_BM),
        in_specs=[pl.BlockSpec((FFN_BM, k), lambda j, i: (i, 0)), pl.BlockSpec((ms, k), lambda j, i: (0, 0)),
                  wspec(0), wspec(nt), cwspec(0), cwspec(nt), cbspec(0), cbspec(nt)],
        out_specs=[pl.BlockSpec((FFN_BM, FFN_TILE), lambda j, i: (i, j)), tail, tail, samp, samp],
        out_shape=[jax.ShapeDtypeStruct((m, D_FF), BF16),
                   jax.ShapeDtypeStruct((bsz, SUBLANES, D_FF), F32), jax.ShapeDtypeStruct((bsz, SUBLANES, D_FF), F32),
                   jax.ShapeDtypeStruct((ms, D_FF), F32), jax.ShapeDtypeStruct((ms, D_FF), F32)],
        scratch_shapes=[pltpu.VMEM((k, FFN_TILE), BF16), pltpu.VMEM((k, FFN_TILE), BF16), carry, carry],
        compiler_params=_cparams(("parallel", "arbitrary")),
        name="ffn_up",
    )(xp, xs, w_up, w_up, conv_w, conv_w, cb3, cb3)


def _rope_tables(pos):
    half = HEAD_DIM // 2
    inv = ROPE_THETA ** (-jnp.arange(half, dtype=F32) / half)
    ang = pos.astype(F32)[:, None] * inv[None, :]
    cos, sin = jnp.cos(ang), jnp.sin(ang)
    return jnp.concatenate([cos, cos], axis=1), jnp.concatenate([-sin, sin], axis=1)


def _kv_out(rows, bsz):
    return rows.reshape(bsz, -1, 2, KV_HEADS, HEAD_DIM)


def kernel(x_prompt, x_sample, cache_kv_cmp, cache_kv_sel, cache_kv_win, state_ssm_conv, state_ssm, state_ffn_conv, page_table, norm_mix, norm_ffn, w_in_even, w_out_even, gmlp_v_norm, gmlp_ws, gmlp_bs, q_norm, k_norm, cmp_pool, w_in_odd, ssm_conv_w, ssm_conv_b, ssm_dt_bias, ssm_a_log, ssm_d, ssm_norm, w_out_odd, ffn_w_up, ffn_conv_w, ffn_conv_b, ffn_w_down):
    bp, sp, _ = x_prompt.shape
    bs, ss, _ = x_sample.shape
    depth = norm_mix.shape[0]
    n_pool = cache_kv_cmp.shape[1]
    n_pages = page_table.shape[1]
    past_len = n_pages * PAGE_SIZE
    wb = cache_kv_win.shape[2]
    mp, ms = bp * sp, bs * ss

    cos_p, sin_p = _rope_tables(jnp.arange(sp, dtype=jnp.int32))
    cos_p, sin_p = jnp.tile(cos_p, (bp, 1)), jnp.tile(sin_p, (bp, 1))
    cos_s, sin_s = _rope_tables(past_len + jnp.arange(ss, dtype=jnp.int32))
    cos_s, sin_s = jnp.tile(cos_s, (bs, 1)), jnp.tile(sin_s, (bs, 1))

    cache_c = cache_kv_cmp.reshape(-1, PAGE_ROWS, LANES)
    cache_s = cache_kv_sel.reshape(-1, BLOCK_ROWS, LANES)
    cache_w = cache_kv_win.reshape(-1, wb * KV_SLOTS, LANES)

    w_down16 = ffn_w_down.astype(BF16)
    w_out_odd16 = w_out_odd.astype(BF16)
    w_in_even_t = jnp.swapaxes(w_in_even, 1, 2)
    w_in_odd_t = jnp.swapaxes(w_in_odd, 1, 2)

    hp = x_prompt.reshape(mp, D_MODEL)
    hs = x_sample.reshape(ms, D_MODEL)
    outs = {k: [] for k in ("p_cmp", "p_sel", "p_win", "s_cmp", "s_sel", "s_win", "s_v",
                            "p_sconv", "p_sst", "s_sconv", "s_sst", "p_fconv", "s_fconv")}

    for layer in range(depth):
        i = layer // 2
        xp = rms_cast(hp, norm_mix[layer])
        xs = rms_cast(hs, norm_mix[layer])
        if layer % 2 == 0:
            w_gate = w_in_even_t[i, E_MAIN:]
            proj_p, proj_s = matmul_ws([xp], [xs], w_in_even_t, i, E_MAIN, bn=768, w_t=True, name="even_in")
            gl = matmul_narrow(xp, w_gate)
            u, v, qn, qr, rc, rs, rw, gate, ks16, kw16, vts16, vtw16, rc_stored = even_prep(
                proj_p, gl, cos_p, sin_p, gmlp_v_norm[i], q_norm[i], k_norm[i], for_mxu=True)
            a_p = gmlp(u, v, gmlp_ws[i], gmlp_bs[i])
            kcmp = compress(rc, cmp_pool[i])
            b_p = nsa_prompt(qn, qr, kcmp, ks16, vts16, kw16, vtw16, gate, bp, sp)
            outs["p_cmp"].append(_kv_out(rc_stored, bp))
            outs["p_sel"].append(_kv_out(rs, bp))
            outs["p_win"].append(_kv_out(rw, bp)[:, sp - min(WINDOW, sp):])
            gl = matmul_narrow(xs, w_gate)
            u, v, qn, qr, rc, rs, rw, gate = even_prep(
                proj_s, gl, cos_s, sin_s, gmlp_v_norm[i], q_norm[i], k_norm[i], for_mxu=False)
            lpad = ((0, 0), (0, A_CHUNK - ss), (0, 0))
            a_s = gmlp(jnp.pad(u.reshape(bs, ss, -1), lpad).reshape(bs * A_CHUNK, -1),
                       jnp.pad(v.reshape(bs, ss, -1), lpad).reshape(bs * A_CHUNK, -1),
                       gmlp_ws[i], gmlp_bs[i]).reshape(bs, A_CHUNK, -1)[:, :ss].reshape(ms, -1)
            kc_past = pool_past(cache_c, i * n_pool, page_table, cmp_pool[i])
            o_cmp, idx = sample_select(qn.reshape(bs, ss, -1), kc_past)
            b_s = sample_attn(idx, page_table, cache_s, i * n_pool, cache_w, i * bs,
                              qr.reshape(bs, ss, -1), rs.reshape(bs, ss, -1), rw.reshape(bs, ss, -1),
                              gate.reshape(bs, ss, -1), o_cmp).reshape(ms, B_WIDTH)
            outs["s_cmp"].append(_kv_out(rc, bs))
            outs["s_sel"].append(_kv_out(rs, bs))
            outs["s_win"].append(_kv_out(rw, bs))
            outs["s_v"].append(v.reshape(bs, ss, A_WIDTH))
            hp, hs = matmul_ws([a_p, b_p], [a_s, b_s], w_out_even, i, D_MODEL, bn=1024, res=(hp, hs), name="even_out")
        else:
            zx = SSM_INNER + SSM_CONV_DIM
            w_dt = w_in_odd_t[i, zx:]
            proj_p, proj_s = matmul_ws([xp], [xs], w_in_odd_t, i, zx, bn=1024, w_t=True, name="odd_in")
            dt = matmul_narrow(xp, w_dt, ssm_dt_bias[i])[:, :SSM_HEADS]
            h0 = jnp.zeros((bp, SSM_STATE, SSM_INNER), F32)
            y, ht, *tails = ssd(proj_p, SSM_INNER, proj_p, dt, ssm_a_log[i], ssm_d[i], ssm_norm[i], h0, bp, sp,
                                conv=(ssm_conv_w, ssm_conv_b, i))
            tail = jnp.concatenate(tails, axis=2)
            hp = matmul(y, w_out_odd16, layer=i, bn=512, res=hp, name="odd_out")
            outs["p_sconv"].append(tail[:, SUBLANES - (SSM_CONV - 1):])
            outs["p_sst"].append(_state_from_cols(ht))
            dt = matmul_narrow(xs, w_dt, ssm_dt_bias[i])[:, :SSM_HEADS]
            xin = jnp.concatenate([state_ssm_conv[i], proj_s[:, SSM_INNER:].reshape(bs, ss, -1)], axis=1)
            views = [xin[:, k:k + ss].reshape(ms, -1) for k in range(SSM_CONV)]
            xbc = conv_taps(views, ssm_conv_w[i], ssm_conv_b[i], act="silu", tile=1024)
            cpad = ((0, 0), (0, SSM_CHUNK - ss), (0, 0))
            padrows = lambda a: jnp.pad(a.reshape(bs, ss, -1), cpad).reshape(bs * SSM_CHUNK, -1)
            y, ht = ssd(padrows(xbc), 0, padrows(proj_s[:, :SSM_INNER]), padrows(dt), ssm_a_log[i], ssm_d[i], ssm_norm[i],
                        _state_to_cols(state_ssm[i].astype(F32)), bs, SSM_CHUNK)
            y = y.reshape(bs, SSM_CHUNK, -1)[:, :ss].reshape(ms, -1)
            hs = matmul(y, w_out_odd16, layer=i, bn=1024, res=hs, name="odd_out_s")
            outs["s_sconv"].append(xin[:, ss:])
            outs["s_sst"].append(_state_from_cols(ht))
        xp = rms_cast(hp, norm_ffn[layer])
        xs = rms_cast(hs, norm_ffn[layer])
        act, tail_g, tail_u, hu_g, hu_u = ffn_up(xp, xs, ffn_w_up, ffn_conv_w, ffn_conv_b, layer, bp, sp)
        hp = matmul(act, w_down16, layer=layer, bn=512, res=hp, name="ffn_down")
        outs["p_fconv"].append(jnp.concatenate([tail_g, tail_u], axis=2)[:, SUBLANES - (FFN_CONV - 1):])
        hu = jnp.concatenate([hu_g, hu_u], axis=1)
        xin = jnp.concatenate([state_ffn_conv[layer], hu.reshape(bs, ss, -1)], axis=1)
        views = [xin[:, k:k + ss].reshape(ms, -1) for k in range(FFN_CONV)]
        hc = conv_taps(views, ffn_conv_w[layer], ffn_conv_b[layer], act="none", tile=1024)
        hs = matmul(glu(hc, D_FF, 512), w_down16, layer=layer, bn=512, res=hs, name="ffn_down_s")
        outs["s_fconv"].append(xin[:, ss:])

    st = lambda k: jnp.stack(outs[k])
    return (hp.reshape(bp, sp, D_MODEL), hs.reshape(bs, ss, D_MODEL), st("p_cmp"), st("p_sel"), st("p_win"),
            st("p_sconv"), st("p_sst"), st("p_fconv"), st("s_cmp"), st("s_sel"), st("s_win"), st("s_v"),
            st("s_sconv"), st("s_sst"), st("s_fconv"))
```

```python
import functools

import jax
import jax.numpy as jnp
from jax import lax
from jax.experimental import pallas as pl
from jax.experimental.pallas import tpu as pltpu

F32 = jnp.float32
BF16 = jnp.bfloat16

D_MODEL = 2048
A_WIDTH = D_MODEL // 2
A_GROUPS = 8
A_CHUNK = 128
NSA_HEADS = 8
HEAD_DIM = 128
KV_HEADS = 2
NSA_GROUP = NSA_HEADS // KV_HEADS
B_WIDTH = NSA_HEADS * HEAD_DIM
KV_COLS = 2 * KV_HEADS * HEAD_DIM
KV_SLOTS = 2 * KV_HEADS
CMP_BLOCK = 64
SEL_TOPN = 16
WINDOW = 512
PAGE_SIZE = 128
ROPE_THETA = 10000.0
ATTN_SCALE = HEAD_DIM ** -0.5
LOG2E = 1.4426950408889634
SSM_INNER = 2 * D_MODEL
SSM_HEAD_DIM = 64
SSM_HEADS = SSM_INNER // SSM_HEAD_DIM
SSM_GROUPS = 8
SSM_GROUP_HEADS = SSM_HEADS // SSM_GROUPS
SSM_GROUP_WIDTH = SSM_INNER // SSM_GROUPS
SSM_STATE = 128
SSM_CONV = 4
SSM_CHUNK = 128
SSM_CONV_DIM = SSM_INNER + 2 * SSM_GROUPS * SSM_STATE
D_FF = 5632
FFN_CONV = 3
EPS = 1e-6
NEG = -1e30
FORCE = 1e4
TINY = 1e-30

VMEM_LIMIT_BYTES = 56 * 1024 * 1024
LANES = 128
SUBLANES = 8

E_MAIN = 2 * A_WIDTH + B_WIDTH + 3 * KV_COLS
N_GATES = 3 * NSA_HEADS
GATES_PER_KV = 3 * NSA_GROUP


def _cparams(sem):
    return pltpu.CompilerParams(dimension_semantics=sem, vmem_limit_bytes=VMEM_LIMIT_BYTES)


def _dot(a, b):
    return jnp.dot(a, b, preferred_element_type=F32)


def _dot_nt(a, b):
    return lax.dot_general(a, b, (((1,), (1,)), ((), ())), preferred_element_type=F32)


def _iota(shape, dim):
    return lax.broadcasted_iota(jnp.int32, shape, dim)


def _split3(x):
    hi = x.astype(BF16)
    r1 = x - hi.astype(F32)
    mid = r1.astype(BF16)
    lo = (r1 - mid.astype(F32)).astype(BF16)
    return hi, mid, lo


def _dot3_l(x, m):
    hi, mid, lo = _split3(x)
    return _dot(hi, m) + _dot(mid, m) + _dot(lo, m)


def _dot3_r(m, x):
    hi, mid, lo = _split3(x)
    return _dot(m, hi) + _dot(m, mid) + _dot(m, lo)


def _dot2_l(x, m):
    hi = x.astype(BF16)
    return _dot(hi, m) + _dot((x - hi.astype(F32)).astype(BF16), m)


def _rms_kernel(x_ref, g_ref, o_ref):
    x = x_ref[...]
    ms = jnp.mean(x * x, axis=-1, keepdims=True)
    o_ref[...] = (x * lax.rsqrt(ms + EPS) * g_ref[...]).astype(o_ref.dtype)


def rms_cast(x, g):
    m, d = x.shape
    tr = min(m, 512)
    return pl.pallas_call(
        _rms_kernel,
        grid=(m // tr,),
        in_specs=[pl.BlockSpec((tr, d), lambda i: (i, 0)), pl.BlockSpec((1, d), lambda i: (0, 0))],
        out_specs=pl.BlockSpec((tr, d), lambda i: (i, 0)),
        out_shape=jax.ShapeDtypeStruct((m, d), BF16),
        compiler_params=_cparams(("parallel",)),
        name="rms_cast",
    )(x, g.reshape(1, d))


def _mm_kernel(x_ref, w_ref, o_ref):
    o_ref[...] = _dot(x_ref[...], w_ref[0]).astype(o_ref.dtype)


def _mm_res_kernel(x_ref, w_ref, r_ref, o_ref):
    o_ref[...] = (r_ref[...] + _dot(x_ref[...], w_ref[0])).astype(o_ref.dtype)


def matmul(x, w, *, bn, layer=0, res=None, out_dtype=F32, name="mm"):
    m, k = x.shape
    if w.ndim == 2:
        w = w[None]
    n = w.shape[2]
    bm = min(m, 1024)
    in_specs = [pl.BlockSpec((bm, k), lambda i, j: (i, 0)), pl.BlockSpec((1, k, bn), lambda i, j: (layer, 0, j))]
    args = [x, w]
    kern = _mm_kernel
    if res is not None:
        in_specs.append(pl.BlockSpec((bm, bn), lambda i, j: (i, j)))
        args.append(res)
        kern = _mm_res_kernel
    return pl.pallas_call(
        kern,
        grid=(m // bm, n // bn),
        in_specs=in_specs,
        out_specs=pl.BlockSpec((bm, bn), lambda i, j: (i, j)),
        out_shape=jax.ShapeDtypeStruct((m, n), out_dtype),
        compiler_params=_cparams(("parallel", "parallel")),
        name=name,
    )(*args)


WS_BM = 1024


def _mm_ws_kernel(*refs, n_parts, has_res, w_t):
    xp, xs, w_ref = refs[:n_parts], refs[n_parts:2 * n_parts], refs[2 * n_parts]
    pos = 2 * n_parts + 1
    if has_res:
        r_ref, rs_ref = refs[pos], refs[pos + 1]
        pos += 2
    o_ref, os_ref, wb_ref = refs[pos], refs[pos + 1], refs[pos + 2]

    def mm(parts):
        acc, k0 = None, 0
        for p in parts:
            kp = p.shape[1]
            t = _dot(p[...], wb_ref[k0:k0 + kp, :])
            acc = t if acc is None else acc + t
            k0 += kp
        return acc

    @pl.when(pl.program_id(1) == 0)
    def _():
        wb_ref[...] = (w_ref[0].T if w_t else w_ref[0]).astype(BF16)
        ys = mm(xs)
        if has_res:
            ys = rs_ref[...] + ys
        os_ref[...] = ys.astype(os_ref.dtype)

    y = mm(xp)
    if has_res:
        y = r_ref[...] + y
    o_ref[...] = y.astype(o_ref.dtype)


def matmul_ws(xp_parts, xs_parts, w, layer, n_cols, *, bn, w_t=False, res=None, name="mm_ws"):
    mp = xp_parts[0].shape[0]
    ms = xs_parts[0].shape[0]
    k = w.shape[2] if w_t else w.shape[1]
    n_parts = len(xp_parts)
    in_specs = [pl.BlockSpec((WS_BM, p.shape[1]), lambda j, i: (i, 0)) for p in xp_parts]
    in_specs += [pl.BlockSpec((ms, p.shape[1]), lambda j, i: (0, 0)) for p in xs_parts]
    in_specs.append(pl.BlockSpec((1, bn, k), lambda j, i: (layer, j, 0)) if w_t
                    else pl.BlockSpec((1, k, bn), lambda j, i: (layer, 0, j)))
    args = list(xp_parts) + list(xs_parts) + [w]
    if res is not None:
        in_specs += [pl.BlockSpec((WS_BM, bn), lambda j, i: (i, j)), pl.BlockSpec((ms, bn), lambda j, i: (0, j))]
        args += list(res)
    return pl.pallas_call(
        functools.partial(_mm_ws_kernel, n_parts=n_parts, has_res=res is not None, w_t=w_t),
        grid=(n_cols // bn, mp // WS_BM),
        in_specs=in_specs,
        out_specs=[pl.BlockSpec((WS_BM, bn), lambda j, i: (i, j)), pl.BlockSpec((ms, bn), lambda j, i: (0, j))],
        out_shape=[jax.ShapeDtypeStruct((mp, n_cols), F32), jax.ShapeDtypeStruct((ms, n_cols), F32)],
        scratch_shapes=[pltpu.VMEM((k, bn), BF16)],
        compiler_params=_cparams(("parallel", "arbitrary")),
        name=name,
    )(*args)


def _mm_narrow_kernel(x_ref, wt_ref, b_ref, o_ref, *, softplus):
    y = _dot_nt(x_ref[...], wt_ref[...].astype(BF16))
    if softplus:
        y = jax.nn.softplus(y + b_ref[...])
    o_ref[...] = y


def matmul_narrow(x, wt, bias=None):
    m, k = x.shape
    n = wt.shape[0]
    bm = min(m, 1024)
    b = jnp.zeros((LANES,), F32) if bias is None else jnp.pad(bias.astype(F32), (0, LANES - n))
    return pl.pallas_call(
        functools.partial(_mm_narrow_kernel, softplus=bias is not None),
        grid=(m // bm,),
        in_specs=[pl.BlockSpec((bm, k), lambda i: (i, 0)), pl.BlockSpec((LANES, k), lambda i: (0, 0)),
                  pl.BlockSpec((1, LANES), lambda i: (0, 0))],
        out_specs=pl.BlockSpec((bm, LANES), lambda i: (i, 0)),
        out_shape=jax.ShapeDtypeStruct((m, LANES), F32),
        compiler_params=_cparams(("parallel",)),
        name="mm_narrow",
    )(x, jnp.pad(wt, ((0, LANES - n), (0, 0))), b.reshape(1, LANES))


def _head_rms(x, gain):
    ms = jnp.mean(x * x, axis=-1, keepdims=True)
    return x * lax.rsqrt(ms + EPS) * gain


def _rope(x, cosf, sinf):
    return x * cosf + pltpu.roll(x, HEAD_DIM // 2, axis=1) * sinf


def _even_prep_kernel(p_ref, gl_ref, cos_ref, sin_ref, vg_ref, qn_ref, kn_ref,
                      u_ref, v_ref, q_ref, qr_ref, rc_ref, rs_ref, rw_ref, gate_ref, *mxu_refs):
    cosf = cos_ref[...]
    sinf = sin_ref[...]
    u_ref[...] = jax.nn.gelu(p_ref[:, 0:A_WIDTH]).astype(u_ref.dtype)
    for g in range(A_GROUPS):
        sl = slice(g * LANES, (g + 1) * LANES)
        vg = jax.nn.gelu(p_ref[:, A_WIDTH + g * LANES:A_WIDTH + (g + 1) * LANES])
        v_ref[:, sl] = _head_rms(vg, vg_ref[:, sl])
    qgain = qn_ref[...]
    for h in range(NSA_HEADS):
        sl = slice(h * LANES, (h + 1) * LANES)
        q = _head_rms(p_ref[:, 2 * A_WIDTH + h * LANES:2 * A_WIDTH + (h + 1) * LANES], qgain)
        q_ref[:, sl] = q.astype(q_ref.dtype)
        q_rot = _rope(q, cosf, sinf)
        if mxu_refs:
            q_rot = q_rot * EXP2_SCALE
        qr_ref[:, sl] = q_rot.astype(qr_ref.dtype)
    base = 2 * A_WIDTH + B_WIDTH
    for which, row_ref in enumerate((rc_ref, rs_ref, rw_ref)):
        gain = kn_ref[which:which + 1, :]
        off = base + which * KV_COLS
        k16_ref, vt_ref = (mxu_refs[which - 1], mxu_refs[which + 1]) if (mxu_refs and which > 0) else (None, None)
        stored_ref = (mxu_refs[4] if which == 0 else row_ref) if mxu_refs else None
        tr = p_ref.shape[0]
        for h in range(KV_HEADS):
            sl = slice(h * LANES, (h + 1) * LANES)
            k = _head_rms(p_ref[:, off + h * LANES:off + (h + 1) * LANES], gain)
            if which > 0:
                k = _rope(k, cosf, sinf)
            vals = p_ref[:, off + (KV_HEADS + h) * LANES:off + (KV_HEADS + h + 1) * LANES]
            if stored_ref is not None:
                stored_ref[pl.ds(h, tr, stride=KV_SLOTS), :] = k
                stored_ref[pl.ds(KV_HEADS + h, tr, stride=KV_SLOTS), :] = vals
            if stored_ref is not row_ref:
                row_ref[:, sl] = k
                row_ref[:, (KV_HEADS + h) * LANES:(KV_HEADS + h + 1) * LANES] = vals
            if k16_ref is not None:
                k16_ref[:, sl] = k.astype(BF16)
                vt_ref[0, h] = vals.T.astype(BF16)
    gate_ref[...] = jax.nn.sigmoid(gl_ref[...])


def even_prep(proj, gl, cosf, sinf, v_gain, q_gain, k_gain, *, for_mxu):
    m = proj.shape[0]
    tr = min(m, SEL_KC)
    row = lambda w: pl.BlockSpec((tr, w), lambda i: (i, 0))
    full = lambda r, w: pl.BlockSpec((r, w), lambda i: (0, 0))
    out_shapes = [
        jax.ShapeDtypeStruct((m, A_WIDTH), BF16),
        jax.ShapeDtypeStruct((m, A_WIDTH), F32),
        jax.ShapeDtypeStruct((m, B_WIDTH), BF16),
        jax.ShapeDtypeStruct((m, B_WIDTH), BF16),
        jax.ShapeDtypeStruct((m, KV_COLS), F32),
        jax.ShapeDtypeStruct((m, KV_COLS), F32),
        jax.ShapeDtypeStruct((m, KV_COLS), F32),
        jax.ShapeDtypeStruct((m, LANES), F32),
    ]
    out_specs = [row(s.shape[1]) for s in out_shapes]
    if for_mxu:
        assert tr == SEL_KC
        stored = jax.ShapeDtypeStruct((m * KV_SLOTS, LANES), F32)
        stored_spec = pl.BlockSpec((tr * KV_SLOTS, LANES), lambda i: (i, 0))
        out_shapes[5:7] = [stored, stored]
        out_specs[5:7] = [stored_spec, stored_spec]
        out_shapes += [jax.ShapeDtypeStruct((m, KV_HEADS * LANES), BF16)] * 2
        out_specs += [row(KV_HEADS * LANES)] * 2
        out_shapes += [jax.ShapeDtypeStruct((m // tr, KV_HEADS, HEAD_DIM, tr), BF16)] * 2
        out_specs += [pl.BlockSpec((1, KV_HEADS, HEAD_DIM, tr), lambda i: (i, 0, 0, 0))] * 2
        out_shapes.append(stored)
        out_specs.append(stored_spec)
    return pl.pallas_call(
        _even_prep_kernel,
        grid=(m // tr,),
        in_specs=[row(E_MAIN), row(LANES), row(LANES), row(LANES),
                  full(1, A_WIDTH), full(1, LANES), full(3, LANES)],
        out_specs=out_specs,
        out_shape=out_shapes,
        compiler_params=_cparams(("parallel",)),
        name="even_prep",
    )(proj, gl, cosf, sinf, v_gain.reshape(1, A_WIDTH), q_gain.reshape(1, LANES), k_gain)


GMLP_CHUNKS = 4


def _gmlp_kernel(u_ref, v_ref, ws_ref, bst_ref, o_ref):
    tril = _iota((A_CHUNK, A_CHUNK), 0) >= _iota((A_CHUNK, A_CHUNK), 1)
    for g in range(A_GROUPS):
        sl = slice(g * LANES, (g + 1) * LANES)
        wm = jnp.where(tril, ws_ref[g], 0.0).astype(BF16)
        bias = bst_ref[:, g:g + 1]
        for c in range(u_ref.shape[0] // A_CHUNK):
            rs = slice(c * A_CHUNK, (c + 1) * A_CHUNK)
            s = _dot(wm, v_ref[rs, sl].astype(BF16)) + bias
            o_ref[rs, sl] = (u_ref[rs, sl].astype(F32) * s).astype(o_ref.dtype)


def gmlp(u, v, ws, bs):
    m = u.shape[0]
    rows = A_CHUNK * GMLP_CHUNKS if m % (A_CHUNK * GMLP_CHUNKS) == 0 else A_CHUNK
    row = pl.BlockSpec((rows, A_WIDTH), lambda i: (i, 0))
    return pl.pallas_call(
        _gmlp_kernel,
        grid=(m // rows,),
        in_specs=[row, row,
                  pl.BlockSpec((A_GROUPS, A_CHUNK, A_CHUNK), lambda i: (0, 0, 0)),
                  pl.BlockSpec((A_CHUNK, A_GROUPS), lambda i: (0, 0))],
        out_specs=row,
        out_shape=jax.ShapeDtypeStruct((m, A_WIDTH), BF16),
        compiler_params=_cparams(("parallel",)),
        name="gmlp",
    )(u, v, ws, bs.T)


def _compress_kernel(x_ref, pool_ref, o_ref):
    pool = pool_ref[...]
    e = jnp.exp(pool - jnp.max(pool, axis=0, keepdims=True))
    w = e / jnp.sum(e, axis=0, keepdims=True)
    x = x_ref[...]
    nb = x.shape[0] // CMP_BLOCK
    o_ref[...] = jnp.sum(x.reshape(nb, CMP_BLOCK, KV_COLS) * w[None], axis=1)


def compress(rows, pool):
    m = rows.shape[0]
    tr = min(m, 1024)
    pool_cols = jnp.concatenate([jnp.repeat(pool.T, LANES, axis=1)] * 2, axis=1)
    return pl.pallas_call(
        _compress_kernel,
        grid=(m // tr,),
        in_specs=[pl.BlockSpec((tr, KV_COLS), lambda i: (i, 0)),
                  pl.BlockSpec((CMP_BLOCK, KV_COLS), lambda i: (0, 0))],
        out_specs=pl.BlockSpec((tr // CMP_BLOCK, KV_COLS), lambda i: (i, 0)),
        out_shape=jax.ShapeDtypeStruct((m // CMP_BLOCK, KV_COLS), F32),
        compiler_params=_cparams(("parallel",)),
        name="compress",
    )(rows, pool_cols)


NSA_TQ = 512
SEL_KC = 512
EXP2_SCALE = ATTN_SCALE * LOG2E


def _kv_gates(gate_ref, h):
    g = gate_ref[...]
    return jnp.where(h == 0, g[:, 0:GATES_PER_KV], g[:, GATES_PER_KV:2 * GATES_PER_KV])


def _pad_rows(x, rows):
    return jnp.concatenate([x, jnp.zeros((rows - x.shape[0], x.shape[1]), x.dtype)], axis=0)


def _nsa_prompt_kernel(qn_ref, qr_ref, kc_ref, vc_ref, ks_ref, vts_ref, kw_ref, vtw_ref, gate_ref, o_ref,
                       acc_ref, osel_ref, *, seq):
    h = pl.program_id(1)
    t = pl.program_id(2)
    nb = seq // CMP_BLOCK
    tq = NSA_TQ
    qpos_t = t * tq + _iota((1, tq), 1)
    blk_t = _iota((nb, 1), 0)

    heads = lambda ref: jnp.concatenate([ref[:, g * LANES:(g + 1) * LANES] for g in range(NSA_GROUP)], axis=0)
    wide = lambda x: jnp.concatenate([x] * NSA_GROUP, axis=1)
    qn4 = heads(qn_ref)
    qr4 = heads(qr_ref)

    kc = kc_ref[0].astype(BF16)
    vc_t = _pad_rows(vc_ref[0], LANES).T.astype(BF16)
    valid_ct = ((blk_t + 1) * CMP_BLOCK - 1) <= t * tq + _iota((1, NSA_GROUP * tq), 1) % tq
    st = jnp.where(valid_ct, _dot_nt(kc, qn4) * ATTN_SCALE, NEG)
    et = jnp.where(valid_ct, jnp.exp(st - jnp.max(st, axis=0, keepdims=True)), 0.0)
    pt = et / jnp.maximum(jnp.sum(et, axis=0, keepdims=True), TINY)
    o_cmp = _dot(vc_t, _pad_rows(pt, LANES).astype(BF16))
    imp_t = pt[:, 0:tq]
    for g in range(1, NSA_GROUP):
        imp_t = imp_t + pt[:, g * tq:(g + 1) * tq]

    cur_t = qpos_t // CMP_BLOCK
    forced = (blk_t == 0) | (blk_t == cur_t)
    score = jnp.where(forced, FORCE, jnp.where(blk_t > cur_t, NEG, imp_t))
    rank = jnp.zeros((nb, tq), jnp.int32)
    for j in range(nb):
        row = score[j:j + 1, :]
        beats = (row > score) | ((row == score) & (blk_t > j))
        rank = rank + beats.astype(jnp.int32)
    sel = _pad_rows((rank < min(SEL_TOPN, nb)).astype(F32), LANES).astype(BF16)

    def attend(lo, hi, k_ref, vt_ref, bias_fn):
        acc_ref[...] = jnp.zeros(acc_ref.shape, F32)

        def body(c, carry):
            m, l = carry
            start = pl.multiple_of(c * SEL_KC, SEL_KC)
            s = _dot_nt(k_ref[pl.ds(start, SEL_KC), :], qr4) + wide(bias_fn(c))
            m_new = jnp.maximum(m, jnp.max(s, axis=0, keepdims=True))
            p = jnp.exp2(s - m_new)
            alpha = jnp.exp2(m - m_new)
            acc_ref[...] = alpha * acc_ref[...] + _dot(vt_ref[c, 0], p.astype(BF16))
            return m_new, alpha * l + jnp.sum(p, axis=0, keepdims=True)

        init = (jnp.full((1, NSA_GROUP * tq), NEG, F32), jnp.zeros((1, NSA_GROUP * tq), F32))
        return lax.fori_loop(lo, hi, body, init)[1]

    def kpos_col(c):
        return c * SEL_KC + _iota((SEL_KC, 1), 0)

    def sel_bias(c):
        expand = ((c * SEL_KC + _iota((SEL_KC, LANES), 0)) // CMP_BLOCK == _iota((SEL_KC, LANES), 1)).astype(BF16)
        chosen = _dot(expand, sel) > 0.5
        return jnp.where(chosen & (kpos_col(c) <= qpos_t), 0.0, NEG)

    def win_bias(c):
        kpos = kpos_col(c)
        return jnp.where((kpos <= qpos_t) & (kpos >= qpos_t - WINDOW), 0.0, NEG)

    hi = (t * tq + tq - 1) // SEL_KC + 1
    l_sel = attend(0, hi, ks_ref, vts_ref, sel_bias)
    osel_ref[...] = acc_ref[...] / l_sel
    l_win = attend(jnp.maximum(t * tq - WINDOW, 0) // SEL_KC, hi, kw_ref, vtw_ref, win_bias)
    o_win = acc_ref[...] / l_win

    g_all = gate_ref[...]
    gate_t = jnp.where(h == 0, g_all, pltpu.roll(g_all, LANES - GATES_PER_KV, axis=1)).T
    for g in range(NSA_GROUP):
        cs = slice(g * tq, (g + 1) * tq)
        out_t = (gate_t[3 * g:3 * g + 1] * o_cmp[:, cs] + gate_t[3 * g + 1:3 * g + 2] * osel_ref[:, cs]
                 + gate_t[3 * g + 2:3 * g + 3] * o_win[:, cs])
        o_ref[:, g * LANES:(g + 1) * LANES] = out_t.T.astype(o_ref.dtype)


def nsa_prompt(qn, qr, kcmp, ks16, vts16, kw16, vtw16, gate, bsz, seq):
    nt = seq // NSA_TQ
    nb = seq // CMP_BLOCK
    nc = seq // SEL_KC
    assert nb <= LANES and seq % SEL_KC == 0
    qspec = pl.BlockSpec((NSA_TQ, NSA_GROUP * LANES), lambda b, h, t: (b * nt + t, h))
    cmp_k = pl.BlockSpec((1, nb, LANES), lambda b, h, t: (b, 0, h))
    cmp_v = pl.BlockSpec((1, nb, LANES), lambda b, h, t: (b, 0, KV_HEADS + h))
    key = pl.BlockSpec((seq, LANES), lambda b, h, t: (b, h))
    val_t = pl.BlockSpec((nc, 1, HEAD_DIM, SEL_KC), lambda b, h, t: (b, h, 0, 0))
    kc3 = kcmp.reshape(bsz, nb, KV_COLS)
    return pl.pallas_call(
        functools.partial(_nsa_prompt_kernel, seq=seq),
        grid=(bsz, KV_HEADS, nt),
        in_specs=[qspec, qspec, cmp_k, cmp_v, key, val_t, key, val_t,
                  pl.BlockSpec((NSA_TQ, LANES), lambda b, h, t: (b * nt + t, 0))],
        out_specs=qspec,
        out_shape=jax.ShapeDtypeStruct((bsz * seq, B_WIDTH), BF16),
        scratch_shapes=[pltpu.VMEM((HEAD_DIM, NSA_GROUP * NSA_TQ), F32),
                        pltpu.VMEM((HEAD_DIM, NSA_GROUP * NSA_TQ), F32)],
        compiler_params=_cparams(("parallel", "parallel", "arbitrary")),
        name="nsa_prompt",
    )(qn, qr, kc3, kc3, ks16, vts16, kw16, vtw16, gate)


PAGES_PER_STEP = 16
PAGE_ROWS = PAGE_SIZE * KV_SLOTS
BLOCK_ROWS = CMP_BLOCK * KV_SLOTS
BLOCKS_PER_PAGE = PAGE_SIZE // CMP_BLOCK


def _pool_past_kernel(pt_ref, *refs, pps):
    page_refs, pool_ref, o_ref = refs[:pps], refs[pps], refs[pps + 1]
    tiles = BLOCK_ROWS // SUBLANES
    pool = pool_ref[...].reshape(tiles, SUBLANES, LANES)
    fold = lambda a, op: op(a, pltpu.roll(a, KV_SLOTS, axis=0))
    mx = fold(jnp.max(pool, axis=0), jnp.maximum)
    e = jnp.exp(pool - mx[None])
    w = e / fold(jnp.sum(e, axis=0), jnp.add)[None]
    first = _iota((SUBLANES, LANES), 0) < KV_SLOTS
    for k, ref in enumerate(page_refs):
        x = ref[0].reshape(BLOCKS_PER_PAGE, tiles, SUBLANES, LANES)
        sums = [fold(jnp.sum(x[b] * w, axis=0), jnp.add) for b in range(BLOCKS_PER_PAGE)]
        o_ref[0, k * SUBLANES:(k + 1) * SUBLANES, :] = jnp.where(first, sums[0], sums[1])


def pool_past(cache, layer_off, page_table, pool):
    bsz, n_pages = page_table.shape
    pps = min(PAGES_PER_STEP, n_pages)
    steps = n_pages // pps
    assert BLOCKS_PER_PAGE * KV_SLOTS == SUBLANES
    pool_rows = jnp.broadcast_to(jnp.tile(pool.T, (1, 2)).reshape(BLOCK_ROWS, 1), (BLOCK_ROWS, LANES))

    def page_spec(k):
        return pl.BlockSpec((1, PAGE_ROWS, LANES),
                            lambda b, s, pt: (layer_off + pt[b * n_pages + s * pps + k], 0, 0))

    grid_spec = pltpu.PrefetchScalarGridSpec(
        num_scalar_prefetch=1,
        grid=(bsz, steps),
        in_specs=[page_spec(k) for k in range(pps)]
        + [pl.BlockSpec((BLOCK_ROWS, LANES), lambda b, s, pt: (0, 0))],
        out_specs=pl.BlockSpec((1, pps * SUBLANES, LANES), lambda b, s, pt: (b, s, 0)),
    )
    return pl.pallas_call(
        functools.partial(_pool_past_kernel, pps=pps),
        grid_spec=grid_spec,
        out_shape=jax.ShapeDtypeStruct((bsz, n_pages * SUBLANES, LANES), F32),
        compiler_params=_cparams(("parallel", "arbitrary")),
        name="pool_past",
    )(page_table.reshape(-1), *([cache] * pps), pool_rows)


def _sample_select_kernel(q_ref, kc_ref, oc_ref, idx_ref, score_ref, *, dec):
    bsz, rows = kc_ref.shape[0], kc_ref.shape[1]
    n_q = NSA_HEADS * dec
    rows_per_kv = NSA_GROUP * dec
    assert rows_per_kv == 2 * SUBLANES and dec <= SUBLANES // 2
    lane = _iota((n_q, rows), 1)
    is_key = lane % KV_SLOTS == _iota((n_q, 1), 0) // rows_per_kv
    lane8 = _iota((SUBLANES, rows), 1)
    for b in range(bsz):
        x = kc_ref[b].astype(BF16)
        s = jnp.where(is_key, _dot_nt(q_ref[b], x) * ATTN_SCALE, NEG)
        e = jnp.where(is_key, jnp.exp(s - jnp.max(s, axis=-1, keepdims=True)), 0.0)
        p = e / jnp.maximum(jnp.sum(e, axis=-1, keepdims=True), TINY)
        oc_ref[b] = _dot(pltpu.roll(p, KV_HEADS, axis=1).astype(BF16), x)
        for h in range(KV_HEADS):
            t = p[h * rows_per_kv:h * rows_per_kv + SUBLANES] + p[h * rows_per_kv + SUBLANES:(h + 1) * rows_per_kv]
            imp = t + pltpu.roll(t, SUBLANES // 2, axis=0)
            score = jnp.where((lane8 % KV_SLOTS == h) & (lane8 >= KV_SLOTS), imp, -jnp.inf)
            r0 = (b * KV_HEADS + h) * dec
            score_ref[r0:r0 + dec, :] = score[0:dec]
    score = score_ref[...]
    lane_all = _iota(score.shape, 1)
    slot = _iota((score.shape[0], SEL_TOPN), 1)
    picked = jnp.zeros((score.shape[0], SEL_TOPN), jnp.int32)
    for r in range(SEL_TOPN - 2):
        best = jnp.max(score, axis=-1, keepdims=True)
        arg = jnp.min(jnp.where(score == best, lane_all, rows), axis=-1, keepdims=True)
        picked = jnp.where(slot == r + 1, arg // KV_SLOTS, picked)
        score = jnp.where(lane_all == arg, -jnp.inf, score)
    idx_ref[...] = picked


def sample_select(qn, kcmp_past):
    bsz, dec, _ = qn.shape
    rows = kcmp_past.shape[1]
    n_sel = bsz * KV_HEADS * dec
    q_rows = qn.reshape(bsz, dec, NSA_HEADS, HEAD_DIM).swapaxes(1, 2).reshape(bsz, NSA_HEADS * dec, HEAD_DIM)
    o_cmp, idx = pl.pallas_call(
        functools.partial(_sample_select_kernel, dec=dec),
        out_shape=[jax.ShapeDtypeStruct((bsz, NSA_HEADS * dec, HEAD_DIM), F32),
                   jax.ShapeDtypeStruct((n_sel, SEL_TOPN), jnp.int32)],
        scratch_shapes=[pltpu.VMEM((n_sel, rows), F32)],
        compiler_params=pltpu.CompilerParams(vmem_limit_bytes=VMEM_LIMIT_BYTES),
        name="sample_select",
    )(q_rows, kcmp_past)
    o_cmp = o_cmp.reshape(bsz, NSA_HEADS, dec, HEAD_DIM).swapaxes(1, 2).reshape(bsz, dec, B_WIDTH)
    return o_cmp, idx.reshape(bsz, KV_HEADS * dec, SEL_TOPN)


def _rows_to_tile(rows):
    r = _iota((SUBLANES, rows[0].shape[1]), 0)
    out = jnp.zeros((SUBLANES, rows[0].shape[1]), rows[0].dtype)
    for i, x in enumerate(rows):
        out = jnp.where(r == i, jnp.broadcast_to(x, out.shape), out)
    return out


def _sample_attn_kernel(idx_ref, pt_ref, *refs, dec):
    n = SEL_TOPN
    blk_refs = refs[:dec * n]
    (qr_ref, ksn_ref, vsn_ref, wbuf_ref, kwn_ref, vwn_ref, gate_ref, oc_ref, o_ref) = refs[dec * n:]
    h = pl.program_id(1)
    q_all = jnp.concatenate(
        [_rows_to_tile([qr_ref[0, qi, :, g * LANES:(g + 1) * LANES].astype(F32) for g in range(NSA_GROUP)])
         for qi in range(dec)], axis=0).astype(BF16)
    rowq = _iota((dec * SUBLANES, 1), 0) // SUBLANES
    newpos = _iota((1, dec), 1)

    def attend(x_old, mask_old, k_new, v_new):
        s_old = jnp.where(mask_old, _dot_nt(q_all, x_old) * ATTN_SCALE, NEG)
        mask_new = newpos <= rowq
        s_new = jnp.where(mask_new, _dot_nt(q_all, k_new) * ATTN_SCALE, NEG)
        m = jnp.maximum(jnp.max(s_old, axis=-1, keepdims=True), jnp.max(s_new, axis=-1, keepdims=True))
        p_old = jnp.where(mask_old, jnp.exp(s_old - m), 0.0)
        p_new = jnp.where(mask_new, jnp.exp(s_new - m), 0.0)
        l = jnp.sum(p_old, axis=-1, keepdims=True) + jnp.sum(p_new, axis=-1, keepdims=True)
        pv = _dot(pltpu.roll(p_old, KV_HEADS, axis=1).astype(BF16), x_old)
        return (pv + _dot(p_new.astype(BF16), v_new)) / l

    x_sel = jnp.concatenate([r[0].astype(BF16) for r in blk_refs], axis=0)
    per_q = n * BLOCK_ROWS
    lane = _iota((1, dec * per_q), 1)
    mask_sel = (lane // per_q == rowq) & (lane % KV_SLOTS == h) & (lane % per_q < (n - 1) * BLOCK_ROWS)
    o_sel = attend(x_sel, mask_sel, ksn_ref[0].astype(BF16), vsn_ref[0].astype(BF16))

    wrows = wbuf_ref.shape[1]
    wb = wrows // KV_SLOTS
    lane_w = _iota((1, wrows), 1)
    mask_win = (lane_w % KV_SLOTS == h) & (lane_w // KV_SLOTS >= rowq + (wb - WINDOW))
    o_win = attend(wbuf_ref[0].astype(BF16), mask_win, kwn_ref[0].astype(BF16), vwn_ref[0].astype(BF16))

    for qi in range(dec):
        gate = _kv_gates(gate_ref.at[0, qi], h)
        outs = []
        for g in range(NSA_GROUP):
            r = qi * SUBLANES + g
            outs.append(gate[:, 3 * g:3 * g + 1] * oc_ref[0, qi, :, g * LANES:(g + 1) * LANES]
                        + gate[:, 3 * g + 1:3 * g + 2] * o_sel[r:r + 1]
                        + gate[:, 3 * g + 2:3 * g + 3] * o_win[r:r + 1])
        o_ref[0, qi] = jnp.concatenate(outs, axis=1).astype(o_ref.dtype)


def sample_attn(idx, page_table, cache_sel, sel_off, cache_win, win_off, qr, rows_s, rows_w, gate, o_cmp):
    bsz, dec, _ = rows_s.shape
    n_pages = page_table.shape[1]
    grp = NSA_GROUP * LANES

    def bspec(q, k):
        def imap(b, h, idx_r, pt_r):
            blk = idx_r[((b * KV_HEADS + h) * dec + q) * SEL_TOPN + k]
            page = sel_off + pt_r[b * n_pages + blk // BLOCKS_PER_PAGE]
            return (page * BLOCKS_PER_PAGE + blk % BLOCKS_PER_PAGE, 0, 0)
        return pl.BlockSpec((1, BLOCK_ROWS, LANES), imap)

    qmap = lambda b, h, idx_r, pt_r: (b, 0, 0, h)
    newk = pl.BlockSpec((1, dec, LANES), lambda b, h, idx_r, pt_r: (b, 0, h))
    newv = pl.BlockSpec((1, dec, LANES), lambda b, h, idx_r, pt_r: (b, 0, KV_HEADS + h))
    wbuf = pl.BlockSpec((1, cache_win.shape[1], LANES), lambda b, h, idx_r, pt_r: (win_off + b, 0, 0))
    grid_spec = pltpu.PrefetchScalarGridSpec(
        num_scalar_prefetch=2,
        grid=(bsz, KV_HEADS),
        in_specs=[bspec(q, k) for q in range(dec) for k in range(SEL_TOPN)]
        + [pl.BlockSpec((1, dec, 1, grp), qmap), newk, newv, wbuf, newk, newv,
           pl.BlockSpec((1, dec, 1, LANES), lambda b, h, idx_r, pt_r: (b, 0, 0, 0)),
           pl.BlockSpec((1, dec, 1, grp), qmap)],
        out_specs=pl.BlockSpec((1, dec, 1, grp), qmap),
    )
    q4 = lambda a: a.reshape(bsz, dec, 1, a.shape[-1])
    return pl.pallas_call(
        functools.partial(_sample_attn_kernel, dec=dec),
        grid_spec=grid_spec,
        out_shape=jax.ShapeDtypeStruct((bsz, dec, 1, B_WIDTH), BF16),
        compiler_params=_cparams(("parallel", "arbitrary")),
        name="sample_attn",
    )(idx.reshape(-1), page_table.reshape(-1), *([cache_sel] * (dec * SEL_TOPN)),
      q4(qr), rows_s, rows_s, cache_win, rows_w, rows_w, q4(gate), q4(o_cmp))


def _shift_rows(x, prev, k):
    xr = pltpu.roll(x, k, axis=0)
    pr = pltpu.roll(prev, k, axis=0)
    top = jnp.where(_iota((SUBLANES, 1), 0) < k, pr, xr[:SUBLANES])
    return jnp.concatenate([top, xr[SUBLANES:]], axis=0)


def _taps_kernel(*refs, width, act):
    x_refs, w_ref, b_ref, o_ref = refs[:width], refs[width], refs[width + 1], refs[width + 2]
    y = b_ref[...] + x_refs[0][...] * w_ref[0:1, :]
    for k in range(1, width):
        y = y + x_refs[k][...] * w_ref[k:k + 1, :]
    if act == "silu":
        o_ref[...] = y * jax.nn.sigmoid(y)
    else:
        o_ref[...] = y


def conv_taps(views, w, b, *, act, tile):
    width = len(views)
    m, n = views[0].shape
    col = pl.BlockSpec((m, tile), lambda j: (0, j))
    return pl.pallas_call(
        functools.partial(_taps_kernel, width=width, act=act),
        grid=(n // tile,),
        in_specs=[col] * width + [pl.BlockSpec((width, tile), lambda j: (0, j)),
                                  pl.BlockSpec((1, tile), lambda j: (0, j))],
        out_specs=col,
        out_shape=jax.ShapeDtypeStruct((m, n), F32),
        compiler_params=_cparams(("parallel",)),
        name="conv_taps_" + act,
    )(*views, w, b.reshape(1, -1))


def _glu_kernel(g_ref, u_ref, o_ref):
    g = g_ref[...]
    o_ref[...] = (g * jax.nn.sigmoid(g) * u_ref[...]).astype(o_ref.dtype)


def glu(hc, half, tile):
    m = hc.shape[0]
    nt = half // tile
    return pl.pallas_call(
        _glu_kernel,
        grid=(nt,),
        in_specs=[pl.BlockSpec((m, tile), lambda j: (0, j)), pl.BlockSpec((m, tile), lambda j: (0, nt + j))],
        out_specs=pl.BlockSpec((m, tile), lambda j: (0, j)),
        out_shape=jax.ShapeDtypeStruct((m, half), BF16),
        compiler_params=_cparams(("parallel",)),
        name="glu",
    )(hc, hc)


SSD_GPS = SSM_GROUPS


def _ssd_kernel(*refs, n_chunks, fuse_conv):
    c = pl.program_id(2)
    q = SSM_CHUNK
    e_heads = SSM_GROUP_HEADS
    gw = SSM_GROUP_WIDTH
    if fuse_conv:
        (xr_ref, br_ref, cr_ref, z_ref, cwx_ref, cwb_ref, cwc_ref, cbx_ref, cbb_ref, cbc_ref,
         dt_ref, dtt_ref, al_ref, alt_ref, dsk_ref, ng_ref, h0_ref,
         y_ref, ht_ref, tx_ref, tb_ref, tc_ref, st_ref, hx_ref, hb_ref, hc_ref) = refs

        def conv_act(raw_ref, hist_ref, cw_ref, cb_ref, tail_ref):
            x = raw_ref[...]
            prev = jnp.where(c == 0, 0.0, hist_ref[...])
            y = x * cw_ref[0, SSM_CONV - 1:SSM_CONV, :] + cb_ref[0]
            for d in range(1, SSM_CONV):
                y = y + _shift_rows(x, prev, d) * cw_ref[0, SSM_CONV - 1 - d:SSM_CONV - d, :]
            last = x[q - SUBLANES:, :]
            hist_ref[...] = last
            tail_ref[0] = last
            return y * jax.nn.sigmoid(y)

        xs_all = conv_act(xr_ref, hx_ref, cwx_ref, cbx_ref, tx_ref)
        bm_all = conv_act(br_ref, hb_ref, cwb_ref, cbb_ref, tb_ref)
        cm_all = conv_act(cr_ref, hc_ref, cwc_ref, cbc_ref, tc_ref)
    else:
        (xs_ref, bm_ref, cm_ref, z_ref, dt_ref, dtt_ref, al_ref, alt_ref, dsk_ref, ng_ref, h0_ref,
         y_ref, ht_ref, st_ref) = refs
        xs_all, bm_all, cm_all = xs_ref[...], bm_ref[...], cm_ref[...]

    @pl.when(c == 0)
    def _():
        st_ref[...] = h0_ref[0]

    ri = _iota((q, q), 0)
    ci = _iota((q, q), 1)
    tril = ri >= ci
    tril_b = tril.astype(BF16)
    triu_b = (ri <= ci).astype(BF16)
    expand = (_iota((e_heads, gw), 0) == _iota((e_heads, gw), 1) // SSM_HEAD_DIM).astype(BF16)
    lane = _iota((q, 2 * SSM_HEAD_DIM), 1)

    for k in range(SSD_GPS):
        gs = slice(k * gw, (k + 1) * gw)
        ns = slice(k * SSM_STATE, (k + 1) * SSM_STATE)
        hs = slice(k * e_heads, (k + 1) * e_heads)
        dt = dt_ref[:, hs]
        da = dt * (-jnp.exp(al_ref[:, hs]))
        da_t = dtt_ref[hs, :] * (-jnp.exp(alt_ref[hs, :]))
        acs = _dot3_r(tril_b, da)
        acs_t = _dot3_l(da_t, triu_b)
        tot = acs[q - 1:q, :]

        xs = xs_all[:, gs]
        xdt = xs * _dot2_l(dt, expand)
        xdt_b = xdt.astype(BF16)
        bm = bm_all[:, ns]
        cm_b = cm_all[:, ns].astype(BF16)
        cb = _dot_nt(cm_b, bm.astype(BF16))

        y_parts = []
        for pair in range(e_heads // 2):
            cols = slice(pair * 2 * SSM_HEAD_DIM, (pair + 1) * 2 * SSM_HEAD_DIM)
            res = []
            for e in (2 * pair, 2 * pair + 1):
                seg = jnp.where(tril, acs[:, e:e + 1] - acs_t[e:e + 1, :], NEG)
                mix = (cb * jnp.exp(seg)).astype(BF16)
                res.append(_dot(mix, xdt_b[:, cols]))
            y_parts.append(jnp.where(lane < SSM_HEAD_DIM, res[0], res[1]))
        y = jnp.concatenate(y_parts, axis=1)

        state = st_ref[:, gs]
        ex = _dot2_l(jnp.concatenate([jnp.exp(acs), jnp.exp(tot - acs)], axis=0), expand)
        y = y + _dot(cm_b, state.astype(BF16)) * ex[0:q]
        contrib = _dot(bm.T.astype(BF16), (xdt * ex[q:2 * q]).astype(BF16))
        st_ref[:, gs] = state * ex[q - 1:q] + contrib

        y = y + xs * dsk_ref[:, gs]
        z = z_ref[:, gs]
        y = y * (z * jax.nn.sigmoid(z))
        ms = jnp.mean(y * y, axis=-1, keepdims=True)
        y_ref[:, gs] = (y * lax.rsqrt(ms + EPS) * ng_ref[:, gs]).astype(y_ref.dtype)

    @pl.when(c == n_chunks - 1)
    def _():
        ht_ref[0] = st_ref[...]


def ssd(xbc, xbc_col0, zsrc, dt, a_log, d_skip, norm_g, h0, bsz, seq, conv=None):
    nc = seq // SSM_CHUNK
    gps = SSD_GPS
    gw = SSM_GROUP_WIDTH * gps
    nw = SSM_STATE * gps
    e = SSM_GROUP_HEADS
    m = bsz * seq
    assert gps == SSM_GROUPS
    dt_t = dt.T
    al = a_log.astype(F32).reshape(1, SSM_HEADS)
    al_t = a_log.astype(F32).reshape(SSM_HEADS, 1)
    dsk = jnp.repeat(d_skip.astype(F32), SSM_HEAD_DIM).reshape(1, SSM_INNER)
    bc = SSM_GROUPS * SSM_STATE
    x_off = xbc_col0 // gw
    b_off = (xbc_col0 + SSM_INNER) // nw
    c_off = (xbc_col0 + SSM_INNER + bc) // nw
    rowmap = lambda col: (lambda b, g, c: (b * nc + c, col(g)))
    xbc_specs = [pl.BlockSpec((SSM_CHUNK, gw), rowmap(lambda g: x_off + g)),
                 pl.BlockSpec((SSM_CHUNK, nw), rowmap(lambda g: b_off + g)),
                 pl.BlockSpec((SSM_CHUNK, nw), rowmap(lambda g: c_off + g)),
                 pl.BlockSpec((SSM_CHUNK, gw), rowmap(lambda g: g))]
    rest_specs = [pl.BlockSpec((SSM_CHUNK, SSM_HEADS), lambda b, g, c: (b * nc + c, 0)),
                  pl.BlockSpec((SSM_HEADS, SSM_CHUNK), lambda b, g, c: (0, b * nc + c)),
                  pl.BlockSpec((1, SSM_HEADS), lambda b, g, c: (0, 0)),
                  pl.BlockSpec((SSM_HEADS, 1), lambda b, g, c: (0, 0)),
                  pl.BlockSpec((1, gw), lambda b, g, c: (0, g)),
                  pl.BlockSpec((1, gw), lambda b, g, c: (0, g)),
                  pl.BlockSpec((1, SSM_STATE, gw), lambda b, g, c: (b, 0, g))]
    rest_args = [dt, dt_t, al, al_t, dsk, norm_g.astype(F32).reshape(1, SSM_INNER), h0]
    out_specs = [pl.BlockSpec((SSM_CHUNK, gw), rowmap(lambda g: g)),
                 pl.BlockSpec((1, SSM_STATE, gw), lambda b, g, c: (b, 0, g))]
    out_shape = [jax.ShapeDtypeStruct((m, SSM_INNER), BF16),
                 jax.ShapeDtypeStruct((bsz, SSM_STATE, SSM_INNER), F32)]
    scratch = [pltpu.VMEM((SSM_STATE, gw), F32)]
    conv_specs, conv_args = [], []
    if conv is not None:
        cw, cb, layer = conv
        cb3 = cb.reshape(cb.shape[0], 1, -1)
        offs = ((gw, 0), (nw, SSM_INNER // nw), (nw, (SSM_INNER + bc) // nw))
        conv_specs = [pl.BlockSpec((1, SSM_CONV, w), lambda b, g, c, o=o: (layer, 0, o + g)) for w, o in offs]
        conv_specs += [pl.BlockSpec((1, 1, w), lambda b, g, c, o=o: (layer, 0, o + g)) for w, o in offs]
        conv_args = [cw, cw, cw, cb3, cb3, cb3]
        for w, total in ((gw, SSM_INNER), (nw, bc), (nw, bc)):
            out_specs.append(pl.BlockSpec((1, SUBLANES, w), lambda b, g, c: (b, 0, g)))
            out_shape.append(jax.ShapeDtypeStruct((bsz, SUBLANES, total), F32))
            scratch.append(pltpu.VMEM((SUBLANES, w), F32))
    return pl.pallas_call(
        functools.partial(_ssd_kernel, n_chunks=nc, fuse_conv=conv is not None),
        grid=(bsz, SSM_GROUPS // gps, nc),
        in_specs=xbc_specs + conv_specs + rest_specs,
        out_specs=out_specs,
        out_shape=out_shape,
        scratch_shapes=scratch,
        compiler_params=_cparams(("parallel", "parallel", "arbitrary")),
        name="ssd",
    )(xbc, xbc, xbc, zsrc, *conv_args, *rest_args)


def _state_to_cols(h):
    b = h.shape[0]
    return h.transpose(0, 3, 1, 2).reshape(b, SSM_STATE, SSM_INNER)


def _state_from_cols(s):
    b = s.shape[0]
    return s.reshape(b, SSM_STATE, SSM_HEADS, SSM_HEAD_DIM).transpose(0, 2, 3, 1)


FFN_BM = 1024
FFN_SUB = 256
FFN_TILE = 512
FFN_TILES = D_FF // FFN_TILE


def _conv3(h, prev, cw_ref, cb_ref):
    return (_shift_rows(h, prev, 2) * cw_ref[0, 0:1, :] + _shift_rows(h, prev, 1) * cw_ref[0, 1:2, :]
            + h * cw_ref[0, 2:3, :] + cb_ref[0])


def _ffn_up_kernel(x_ref, xs_ref, wg_ref, wu_ref, cwg_ref, cwu_ref, cbg_ref, cbu_ref,
                   act_ref, tg_ref, tu_ref, sg_ref, su_ref, wgb_ref, wub_ref, cg_ref, cu_ref, *, tiles_per_seq):
    i = pl.program_id(1)

    @pl.when(i == 0)
    def _():
        wgb_ref[...] = wg_ref[0].astype(BF16)
        wub_ref[...] = wu_ref[0].astype(BF16)
        sg_ref[...] = _dot(xs_ref[...], wgb_ref[...])
        su_ref[...] = _dot(xs_ref[...], wub_ref[...])

    @pl.when(i % tiles_per_seq == 0)
    def _():
        cg_ref[...] = jnp.zeros_like(cg_ref)
        cu_ref[...] = jnp.zeros_like(cu_ref)

    for r in range(FFN_BM // FFN_SUB):
        rs = slice(r * FFN_SUB, (r + 1) * FFN_SUB)
        x = x_ref[rs, :]
        hg = _dot(x, wgb_ref[...])
        hu = _dot(x, wub_ref[...])
        g = _conv3(hg, cg_ref[...], cwg_ref, cbg_ref)
        u = _conv3(hu, cu_ref[...], cwu_ref, cbu_ref)
        act_ref[rs, :] = (g * jax.nn.sigmoid(g) * u).astype(act_ref.dtype)
        cg_ref[...] = hg[FFN_SUB - SUBLANES:, :]
        cu_ref[...] = hu[FFN_SUB - SUBLANES:, :]
    tg_ref[0] = cg_ref[...]
    tu_ref[0] = cu_ref[...]


def ffn_up(xp, xs, w_up, conv_w, conv_b, layer, bsz, seq):
    m, k = xp.shape
    ms = xs.shape[0]
    tiles_per_seq = seq // FFN_BM
    nt = FFN_TILES
    wspec = lambda off: pl.BlockSpec((1, k, FFN_TILE), lambda j, i: (layer, 0, off + j))
    cwspec = lambda off: pl.BlockSpec((1, FFN_CONV, FFN_TILE), lambda j, i: (layer, 0, off + j))
    cbspec = lambda off: pl.BlockSpec((1, 1, FFN_TILE), lambda j, i: (layer, 0, off + j))
    tail = pl.BlockSpec((1, SUBLANES, FFN_TILE), lambda j, i: (i // tiles_per_seq, 0, j))
    samp = pl.BlockSpec((ms, FFN_TILE), lambda j, i: (0, j))
    cb3 = conv_b.reshape(conv_b.shape[0], 1, -1)
    carry = pltpu.VMEM((SUBLANES, FFN_TILE), F32)
    return pl.pallas_call(
        functools.partial(_ffn_up_kernel, tiles_per_seq=tiles_per_seq),
        grid=(nt, m // FFN_BM),
        in_specs=[pl.BlockSpec((FFN_BM, k), lambda j, i: (i, 0)), pl.BlockSpec((ms, k), lambda j, i: (0, 0)),
                  wspec(0), wspec(nt), cwspec(0), cwspec(nt), cbspec(0), cbspec(nt)],
        out_specs=[pl.BlockSpec((FFN_BM, FFN_TILE), lambda j, i: (i, j)), tail, tail, samp, samp],
        out_shape=[jax.ShapeDtypeStruct((m, D_FF), BF16),
                   jax.ShapeDtypeStruct((bsz, SUBLANES, D_FF), F32), jax.ShapeDtypeStruct((bsz, SUBLANES, D_FF), F32),
                   jax.ShapeDtypeStruct((ms, D_FF), F32), jax.ShapeDtypeStruct((ms, D_FF), F32)],
        scratch_shapes=[pltpu.VMEM((k, FFN_TILE), BF16), pltpu.VMEM((k, FFN_TILE), BF16), carry, carry],
        compiler_params=_cparams(("parallel", "arbitrary")),
        name="ffn_up",
    )(xp, xs, w_up, w_up, conv_w, conv_w, cb3, cb3)


def _rope_tables(pos):
    half = HEAD_DIM // 2
    inv = ROPE_THETA ** (-jnp.arange(half, dtype=F32) / half)
    ang = pos.astype(F32)[:, None] * inv[None, :]
    cos, sin = jnp.cos(ang), jnp.sin(ang)
    return jnp.concatenate([cos, cos], axis=1), jnp.concatenate([-sin, sin], axis=1)


def _kv_out(rows, bsz):
    return rows.reshape(bsz, -1, 2, KV_HEADS, HEAD_DIM)


def kernel(x_prompt, x_sample, cache_kv_cmp, cache_kv_sel, cache_kv_win, state_ssm_conv, state_ssm, state_ffn_conv, page_table, norm_mix, norm_ffn, w_in_even, w_out_even, gmlp_v_norm, gmlp_ws, gmlp_bs, q_norm, k_norm, cmp_pool, w_in_odd, ssm_conv_w, ssm_conv_b, ssm_dt_bias, ssm_a_log, ssm_d, ssm_norm, w_out_odd, ffn_w_up, ffn_conv_w, ffn_conv_b, ffn_w_down):
    bp, sp, _ = x_prompt.shape
    bs, ss, _ = x_sample.shape
    depth = norm_mix.shape[0]
    n_pool = cache_kv_cmp.shape[1]
    n_pages = page_table.shape[1]
    past_len = n_pages * PAGE_SIZE
    wb = cache_kv_win.shape[2]
    mp, ms = bp * sp, bs * ss

    cos_p, sin_p = _rope_tables(jnp.arange(sp, dtype=jnp.int32))
    cos_p, sin_p = jnp.tile(cos_p, (bp, 1)), jnp.tile(sin_p, (bp, 1))
    cos_s, sin_s = _rope_tables(past_len + jnp.arange(ss, dtype=jnp.int32))
    cos_s, sin_s = jnp.tile(cos_s, (bs, 1)), jnp.tile(sin_s, (bs, 1))

    cache_c = cache_kv_cmp.reshape(-1, PAGE_ROWS, LANES)
    cache_s = cache_kv_sel.reshape(-1, BLOCK_ROWS, LANES)
    cache_w = cache_kv_win.reshape(-1, wb * KV_SLOTS, LANES)

    w_down16 = ffn_w_down.astype(BF16)
    w_out_odd16 = w_out_odd.astype(BF16)
    w_in_even_t = jnp.swapaxes(w_in_even, 1, 2)
    w_in_odd_t = jnp.swapaxes(w_in_odd, 1, 2)

    hp = x_prompt.reshape(mp, D_MODEL)
    hs = x_sample.reshape(ms, D_MODEL)
    outs = {k: [] for k in ("p_cmp", "p_sel", "p_win", "s_cmp", "s_sel", "s_win", "s_v",
                            "p_sconv", "p_sst", "s_sconv", "s_sst", "p_fconv", "s_fconv")}

    for layer in range(depth):
        i = layer // 2
        xp = rms_cast(hp, norm_mix[layer])
        xs = rms_cast(hs, norm_mix[layer])
        if layer % 2 == 0:
            w_gate = w_in_even_t[i, E_MAIN:]
            proj_p, proj_s = matmul_ws([xp], [xs], w_in_even_t, i, E_MAIN, bn=768, w_t=True, name="even_in")
            gl = matmul_narrow(xp, w_gate)
            u, v, qn, qr, rc, rs, rw, gate, ks16, kw16, vts16, vtw16, rc_stored = even_prep(
                proj_p, gl, cos_p, sin_p, gmlp_v_norm[i], q_norm[i], k_norm[i], for_mxu=True)
            a_p = gmlp(u, v, gmlp_ws[i], gmlp_bs[i])
            kcmp = compress(rc, cmp_pool[i])
            b_p = nsa_prompt(qn, qr, kcmp, ks16, vts16, kw16, vtw16, gate, bp, sp)
            outs["p_cmp"].append(_kv_out(rc_stored, bp))
            outs["p_sel"].append(_kv_out(rs, bp))
            outs["p_win"].append(_kv_out(rw, bp)[:, sp - min(WINDOW, sp):])
            gl = matmul_narrow(xs, w_gate)
            u, v, qn, qr, rc, rs, rw, gate = even_prep(
                proj_s, gl, cos_s, sin_s, gmlp_v_norm[i], q_norm[i], k_norm[i], for_mxu=False)
            lpad = ((0, 0), (0, A_CHUNK - ss), (0, 0))
            a_s = gmlp(jnp.pad(u.reshape(bs, ss, -1), lpad).reshape(bs * A_CHUNK, -1),
                       jnp.pad(v.reshape(bs, ss, -1), lpad).reshape(bs * A_CHUNK, -1),
                       gmlp_ws[i], gmlp_bs[i]).reshape(bs, A_CHUNK, -1)[:, :ss].reshape(ms, -1)
            kc_past = pool_past(cache_c, i * n_pool, page_table, cmp_pool[i])
            o_cmp, idx = sample_select(qn.reshape(bs, ss, -1), kc_past)
            b_s = sample_attn(idx, page_table, cache_s, i * n_pool, cache_w, i * bs,
                              qr.reshape(bs, ss, -1), rs.reshape(bs, ss, -1), rw.reshape(bs, ss, -1),
                              gate.reshape(bs, ss, -1), o_cmp).reshape(ms, B_WIDTH)
            outs["s_cmp"].append(_kv_out(rc, bs))
            outs["s_sel"].append(_kv_out(rs, bs))
            outs["s_win"].append(_kv_out(rw, bs))
            outs["s_v"].append(v.reshape(bs, ss, A_WIDTH))
            hp, hs = matmul_ws([a_p, b_p], [a_s, b_s], w_out_even, i, D_MODEL, bn=1024, res=(hp, hs), name="even_out")
        else:
            zx = SSM_INNER + SSM_CONV_DIM
            w_dt = w_in_odd_t[i, zx:]
            proj_p, proj_s = matmul_ws([xp], [xs], w_in_odd_t, i, zx, bn=1024, w_t=True, name="odd_in")
            dt = matmul_narrow(xp, w_dt, ssm_dt_bias[i])[:, :SSM_HEADS]
            h0 = jnp.zeros((bp, SSM_STATE, SSM_INNER), F32)
            y, ht, *tails = ssd(proj_p, SSM_INNER, proj_p, dt, ssm_a_log[i], ssm_d[i], ssm_norm[i], h0, bp, sp,
                                conv=(ssm_conv_w, ssm_conv_b, i))
            tail = jnp.concatenate(tails, axis=2)
            hp = matmul(y, w_out_odd16, layer=i, bn=512, res=hp, name="odd_out")
            outs["p_sconv"].append(tail[:, SUBLANES - (SSM_CONV - 1):])
            outs["p_sst"].append(_state_from_cols(ht))
            dt = matmul_narrow(xs, w_dt, ssm_dt_bias[i])[:, :SSM_HEADS]
            xin = jnp.concatenate([state_ssm_conv[i], proj_s[:, SSM_INNER:].reshape(bs, ss, -1)], axis=1)
            views = [xin[:, k:k + ss].reshape(ms, -1) for k in range(SSM_CONV)]
            xbc = conv_taps(views, ssm_conv_w[i], ssm_conv_b[i], act="silu", tile=1024)
            cpad = ((0, 0), (0, SSM_CHUNK - ss), (0, 0))
            padrows = lambda a: jnp.pad(a.reshape(bs, ss, -1), cpad).reshape(bs * SSM_CHUNK, -1)
            y, ht = ssd(padrows(xbc), 0, padrows(proj_s[:, :SSM_INNER]), padrows(dt), ssm_a_log[i], ssm_d[i], ssm_norm[i],
                        _state_to_cols(state_ssm[i].astype(F32)), bs, SSM_CHUNK)
            y = y.reshape(bs, SSM_CHUNK, -1)[:, :ss].reshape(ms, -1)
            hs = matmul(y, w_out_odd16, layer=i, bn=1024, res=hs, name="odd_out_s")
            outs["s_sconv"].append(xin[:, ss:])
            outs["s_sst"].append(_state_from_cols(ht))
        xp = rms_cast(hp, norm_ffn[layer])
        xs = rms_cast(hs, norm_ffn[layer])
        act, tail_g, tail_u, hu_g, hu_u = ffn_up(xp, xs, ffn_w_up, ffn_conv_w, ffn_conv_b, layer, bp, sp)
        hp = matmul(act, w_down16, layer=layer, bn=512, res=hp, name="ffn_down")
        outs["p_fconv"].append(jnp.concatenate([tail_g, tail_u], axis=2)[:, SUBLANES - (FFN_CONV - 1):])
        hu = jnp.concatenate([hu_g, hu_u], axis=1)
        xin = jnp.concatenate([state_ffn_conv[layer], hu.reshape(bs, ss, -1)], axis=1)
        views = [xin[:, k:k + ss].reshape(ms, -1) for k in range(FFN_CONV)]
        hc = conv_taps(views, ffn_conv_w[layer], ffn_conv_b[layer], act="none", tile=1024)
        hs = matmul(glu(hc, D_FF, 512), w_down16, layer=layer, bn=512, res=hs, name="ffn_down_s")
        outs["s_fconv"].append(xin[:, ss:])

    st = lambda k: jnp.stack(outs[k])
    return (hp.reshape(bp, sp, D_MODEL), hs.reshape(bs, ss, D_MODEL), st("p_cmp"), st("p_sel"), st("p_win"),
            st("p_sconv"), st("p_sst"), st("p_fconv"), st("s_cmp"), st("s_sel"), st("s_win"), st("s_v"),
            st("s_sconv"), st("s_sst"), st("s_fconv"))
```

```python
import functools

import jax
import jax.numpy as jnp
from jax import lax
from jax.experimental import pallas as pl
from jax.experimental.pallas import tpu as pltpu

F32 = jnp.float32
BF16 = jnp.bfloat16

D_MODEL = 2048
A_WIDTH = D_MODEL // 2
A_GROUPS = 8
A_CHUNK = 128
NSA_HEADS = 8
HEAD_DIM = 128
KV_HEADS = 2
NSA_GROUP = NSA_HEADS // KV_HEADS
B_WIDTH = NSA_HEADS * HEAD_DIM
KV_COLS = 2 * KV_HEADS * HEAD_DIM
KV_SLOTS = 2 * KV_HEADS
CMP_BLOCK = 64
SEL_TOPN = 16
WINDOW = 512
PAGE_SIZE = 128
ROPE_THETA = 10000.0
ATTN_SCALE = HEAD_DIM ** -0.5
LOG2E = 1.4426950408889634
SSM_INNER = 2 * D_MODEL
SSM_HEAD_DIM = 64
SSM_HEADS = SSM_INNER // SSM_HEAD_DIM
SSM_GROUPS = 8
SSM_GROUP_HEADS = SSM_HEADS // SSM_GROUPS
SSM_GROUP_WIDTH = SSM_INNER // SSM_GROUPS
SSM_STATE = 128
SSM_CONV = 4
SSM_CHUNK = 128
SSM_CONV_DIM = SSM_INNER + 2 * SSM_GROUPS * SSM_STATE
D_FF = 5632
FFN_CONV = 3
EPS = 1e-6
NEG = -1e30
FORCE = 1e4
TINY = 1e-30

VMEM_LIMIT_BYTES = 56 * 1024 * 1024
LANES = 128
SUBLANES = 8

E_MAIN = 2 * A_WIDTH + B_WIDTH + 3 * KV_COLS
N_GATES = 3 * NSA_HEADS
GATES_PER_KV = 3 * NSA_GROUP


def _cparams(sem):
    return pltpu.CompilerParams(dimension_semantics=sem, vmem_limit_bytes=VMEM_LIMIT_BYTES)


def _dot(a, b):
    return jnp.dot(a, b, preferred_element_type=F32)


def _dot_nt(a, b):
    return lax.dot_general(a, b, (((1,), (1,)), ((), ())), preferred_element_type=F32)


def _iota(shape, dim):
    return lax.broadcasted_iota(jnp.int32, shape, dim)


def _split3(x):
    hi = x.astype(BF16)
    r1 = x - hi.astype(F32)
    mid = r1.astype(BF16)
    lo = (r1 - mid.astype(F32)).astype(BF16)
    return hi, mid, lo


def _dot3_l(x, m):
    hi, mid, lo = _split3(x)
    return _dot(hi, m) + _dot(mid, m) + _dot(lo, m)


def _dot3_r(m, x):
    hi, mid, lo = _split3(x)
    return _dot(m, hi) + _dot(m, mid) + _dot(m, lo)


def _dot2_l(x, m):
    hi = x.astype(BF16)
    return _dot(hi, m) + _dot((x - hi.astype(F32)).astype(BF16), m)


def _rms_kernel(x_ref, g_ref, o_ref):
    x = x_ref[...]
    ms = jnp.mean(x * x, axis=-1, keepdims=True)
    o_ref[...] = (x * lax.rsqrt(ms + EPS) * g_ref[...]).astype(o_ref.dtype)


def rms_cast(x, g):
    m, d = x.shape
    tr = min(m, 1024)
    return pl.pallas_call(
        _rms_kernel,
        grid=(m // tr,),
        in_specs=[pl.BlockSpec((tr, d), lambda i: (i, 0)), pl.BlockSpec((1, d), lambda i: (0, 0))],
        out_specs=pl.BlockSpec((tr, d), lambda i: (i, 0)),
        out_shape=jax.ShapeDtypeStruct((m, d), BF16),
        compiler_params=_cparams(("parallel",)),
        name="rms_cast",
    )(x, g.reshape(1, d))


def _mm_kernel(x_ref, w_ref, o_ref):
    o_ref[...] = _dot(x_ref[...], w_ref[0]).astype(o_ref.dtype)


def _mm_res_kernel(x_ref, w_ref, r_ref, o_ref):
    o_ref[...] = (r_ref[...] + _dot(x_ref[...], w_ref[0])).astype(o_ref.dtype)


def matmul(x, w, *, bn, layer=0, res=None, out_dtype=F32, name="mm"):
    m, k = x.shape
    if w.ndim == 2:
        w = w[None]
    n = w.shape[2]
    bm = min(m, 1024)
    in_specs = [pl.BlockSpec((bm, k), lambda i, j: (i, 0)), pl.BlockSpec((1, k, bn), lambda i, j: (layer, 0, j))]
    args = [x, w]
    kern = _mm_kernel
    if res is not None:
        in_specs.append(pl.BlockSpec((bm, bn), lambda i, j: (i, j)))
        args.append(res)
        kern = _mm_res_kernel
    return pl.pallas_call(
        kern,
        grid=(m // bm, n // bn),
        in_specs=in_specs,
        out_specs=pl.BlockSpec((bm, bn), lambda i, j: (i, j)),
        out_shape=jax.ShapeDtypeStruct((m, n), out_dtype),
        compiler_params=_cparams(("parallel", "parallel")),
        name=name,
    )(*args)


WS_BM = 1024


def _mm_ws_kernel(*refs, n_parts, has_res, w_t):
    xp, xs, w_ref = refs[:n_parts], refs[n_parts:2 * n_parts], refs[2 * n_parts]
    pos = 2 * n_parts + 1
    if has_res:
        r_ref, rs_ref = refs[pos], refs[pos + 1]
        pos += 2
    o_ref, os_ref, wb_ref = refs[pos], refs[pos + 1], refs[pos + 2]

    def mm(parts):
        acc, k0 = None, 0
        for p in parts:
            kp = p.shape[1]
            t = _dot(p[...], wb_ref[k0:k0 + kp, :])
            acc = t if acc is None else acc + t
            k0 += kp
        return acc

    @pl.when(pl.program_id(1) == 0)
    def _():
        wb_ref[...] = (w_ref[0].T if w_t else w_ref[0]).astype(BF16)
        ys = mm(xs)
        if has_res:
            ys = rs_ref[...] + ys
        os_ref[...] = ys.astype(os_ref.dtype)

    y = mm(xp)
    if has_res:
        y = r_ref[...] + y
    o_ref[...] = y.astype(o_ref.dtype)


def matmul_ws(xp_parts, xs_parts, w, layer, n_cols, *, bn, w_t=False, res=None, name="mm_ws"):
    mp = xp_parts[0].shape[0]
    ms = xs_parts[0].shape[0]
    k = w.shape[2] if w_t else w.shape[1]
    n_parts = len(xp_parts)
    in_specs = [pl.BlockSpec((WS_BM, p.shape[1]), lambda j, i: (i, 0)) for p in xp_parts]
    in_specs += [pl.BlockSpec((ms, p.shape[1]), lambda j, i: (0, 0)) for p in xs_parts]
    in_specs.append(pl.BlockSpec((1, bn, k), lambda j, i: (layer, j, 0)) if w_t
                    else pl.BlockSpec((1, k, bn), lambda j, i: (layer, 0, j)))
    args = list(xp_parts) + list(xs_parts) + [w]
    if res is not None:
        in_specs += [pl.BlockSpec((WS_BM, bn), lambda j, i: (i, j)), pl.BlockSpec((ms, bn), lambda j, i: (0, j))]
        args += list(res)
    return pl.pallas_call(
        functools.partial(_mm_ws_kernel, n_parts=n_parts, has_res=res is not None, w_t=w_t),
        grid=(n_cols // bn, mp // WS_BM),
        in_specs=in_specs,
        out_specs=[pl.BlockSpec((WS_BM, bn), lambda j, i: (i, j)), pl.BlockSpec((ms, bn), lambda j, i: (0, j))],
        out_shape=[jax.ShapeDtypeStruct((mp, n_cols), F32), jax.ShapeDtypeStruct((ms, n_cols), F32)],
        scratch_shapes=[pltpu.VMEM((k, bn), BF16)],
        compiler_params=_cparams(("parallel", "arbitrary")),
        name=name,
    )(*args)


def _mm_narrow_kernel(x_ref, wt_ref, b_ref, o_ref, *, softplus):
    y = _dot_nt(x_ref[...], wt_ref[...].astype(BF16))
    if softplus:
        y = jax.nn.softplus(y + b_ref[...])
    o_ref[...] = y


def matmul_narrow(x, wt, bias=None):
    m, k = x.shape
    n = wt.shape[0]
    bm = min(m, 1024)
    b = jnp.zeros((LANES,), F32) if bias is None else jnp.pad(bias.astype(F32), (0, LANES - n))
    return pl.pallas_call(
        functools.partial(_mm_narrow_kernel, softplus=bias is not None),
        grid=(m // bm,),
        in_specs=[pl.BlockSpec((bm, k), lambda i: (i, 0)), pl.BlockSpec((LANES, k), lambda i: (0, 0)),
                  pl.BlockSpec((1, LANES), lambda i: (0, 0))],
        out_specs=pl.BlockSpec((bm, LANES), lambda i: (i, 0)),
        out_shape=jax.ShapeDtypeStruct((m, LANES), F32),
        compiler_params=_cparams(("parallel",)),
        name="mm_narrow",
    )(x, jnp.pad(wt, ((0, LANES - n), (0, 0))), b.reshape(1, LANES))


def _head_rms(x, gain):
    ms = jnp.mean(x * x, axis=-1, keepdims=True)
    return x * lax.rsqrt(ms + EPS) * gain


def _rope(x, cosf, sinf):
    return x * cosf + pltpu.roll(x, HEAD_DIM // 2, axis=1) * sinf


def _even_prep_kernel(p_ref, gl_ref, cos_ref, sin_ref, vg_ref, qn_ref, kn_ref,
                      u_ref, v_ref, q_ref, qr_ref, rc_ref, rs_ref, rw_ref, gate_ref, *mxu_refs):
    cosf = cos_ref[...]
    sinf = sin_ref[...]
    u_ref[...] = jax.nn.gelu(p_ref[:, 0:A_WIDTH]).astype(u_ref.dtype)
    for g in range(A_GROUPS):
        sl = slice(g * LANES, (g + 1) * LANES)
        vg = jax.nn.gelu(p_ref[:, A_WIDTH + g * LANES:A_WIDTH + (g + 1) * LANES])
        v_ref[:, sl] = _head_rms(vg, vg_ref[:, sl])
    qgain = qn_ref[...]
    for h in range(NSA_HEADS):
        sl = slice(h * LANES, (h + 1) * LANES)
        q = _head_rms(p_ref[:, 2 * A_WIDTH + h * LANES:2 * A_WIDTH + (h + 1) * LANES], qgain)
        q_ref[:, sl] = q.astype(q_ref.dtype)
        q_rot = _rope(q, cosf, sinf)
        if mxu_refs:
            q_rot = q_rot * EXP2_SCALE
        qr_ref[:, sl] = q_rot.astype(qr_ref.dtype)
    base = 2 * A_WIDTH + B_WIDTH
    for which, row_ref in enumerate((rc_ref, rs_ref, rw_ref)):
        gain = kn_ref[which:which + 1, :]
        off = base + which * KV_COLS
        k16_ref, vt_ref = (mxu_refs[which - 1], mxu_refs[which + 1]) if (mxu_refs and which > 0) else (None, None)
        stored_ref = (mxu_refs[4] if which == 0 else row_ref) if mxu_refs else None
        tr = p_ref.shape[0]
        for h in range(KV_HEADS):
            sl = slice(h * LANES, (h + 1) * LANES)
            k = _head_rms(p_ref[:, off + h * LANES:off + (h + 1) * LANES], gain)
            if which > 0:
                k = _rope(k, cosf, sinf)
            vals = p_ref[:, off + (KV_HEADS + h) * LANES:off + (KV_HEADS + h + 1) * LANES]
            if stored_ref is not None:
                stored_ref[pl.ds(h, tr, stride=KV_SLOTS), :] = k
                stored_ref[pl.ds(KV_HEADS + h, tr, stride=KV_SLOTS), :] = vals
            if stored_ref is not row_ref:
                row_ref[:, sl] = k
                row_ref[:, (KV_HEADS + h) * LANES:(KV_HEADS + h + 1) * LANES] = vals
            if k16_ref is not None:
                k16_ref[:, sl] = k.astype(BF16)
                vt_ref[0, h] = vals.T.astype(BF16)
    gate_ref[...] = jax.nn.sigmoid(gl_ref[...])


def even_prep(proj, gl, cosf, sinf, v_gain, q_gain, k_gain, *, for_mxu):
    m = proj.shape[0]
    tr = min(m, SEL_KC)
    row = lambda w: pl.BlockSpec((tr, w), lambda i: (i, 0))
    full = lambda r, w: pl.BlockSpec((r, w), lambda i: (0, 0))
    out_shapes = [
        jax.ShapeDtypeStruct((m, A_WIDTH), BF16),
        jax.ShapeDtypeStruct((m, A_WIDTH), F32),
        jax.ShapeDtypeStruct((m, B_WIDTH), BF16),
        jax.ShapeDtypeStruct((m, B_WIDTH), BF16),
        jax.ShapeDtypeStruct((m, KV_COLS), F32),
        jax.ShapeDtypeStruct((m, KV_COLS), F32),
        jax.ShapeDtypeStruct((m, KV_COLS), F32),
        jax.ShapeDtypeStruct((m, LANES), F32),
    ]
    out_specs = [row(s.shape[1]) for s in out_shapes]
    if for_mxu:
        assert tr == SEL_KC
        stored = jax.ShapeDtypeStruct((m * KV_SLOTS, LANES), F32)
        stored_spec = pl.BlockSpec((tr * KV_SLOTS, LANES), lambda i: (i, 0))
        out_shapes[5:7] = [stored, stored]
        out_specs[5:7] = [stored_spec, stored_spec]
        out_shapes += [jax.ShapeDtypeStruct((m, KV_HEADS * LANES), BF16)] * 2
        out_specs += [row(KV_HEADS * LANES)] * 2
        out_shapes += [jax.ShapeDtypeStruct((m // tr, KV_HEADS, HEAD_DIM, tr), BF16)] * 2
        out_specs += [pl.BlockSpec((1, KV_HEADS, HEAD_DIM, tr), lambda i: (i, 0, 0, 0))] * 2
        out_shapes.append(stored)
        out_specs.append(stored_spec)
    return pl.pallas_call(
        _even_prep_kernel,
        grid=(m // tr,),
        in_specs=[row(E_MAIN), row(LANES), row(LANES), row(LANES),
                  full(1, A_WIDTH), full(1, LANES), full(3, LANES)],
        out_specs=out_specs,
        out_shape=out_shapes,
        compiler_params=_cparams(("parallel",)),
        name="even_prep",
    )(proj, gl, cosf, sinf, v_gain.reshape(1, A_WIDTH), q_gain.reshape(1, LANES), k_gain)


GMLP_CHUNKS = 4


def _gmlp_kernel(u_ref, v_ref, ws_ref, bst_ref, o_ref):
    tril = _iota((A_CHUNK, A_CHUNK), 0) >= _iota((A_CHUNK, A_CHUNK), 1)
    for g in range(A_GROUPS):
        sl = slice(g * LANES, (g + 1) * LANES)
        wm = jnp.where(tril, ws_ref[g], 0.0).astype(BF16)
        bias = bst_ref[:, g:g + 1]
        for c in range(u_ref.shape[0] // A_CHUNK):
            rs = slice(c * A_CHUNK, (c + 1) * A_CHUNK)
            s = _dot(wm, v_ref[rs, sl].astype(BF16)) + bias
            o_ref[rs, sl] = (u_ref[rs, sl].astype(F32) * s).astype(o_ref.dtype)


def gmlp(u, v, ws, bs):
    m = u.shape[0]
    rows = A_CHUNK * GMLP_CHUNKS if m % (A_CHUNK * GMLP_CHUNKS) == 0 else A_CHUNK
    row = pl.BlockSpec((rows, A_WIDTH), lambda i: (i, 0))
    return pl.pallas_call(
        _gmlp_kernel,
        grid=(m // rows,),
        in_specs=[row, row,
                  pl.BlockSpec((A_GROUPS, A_CHUNK, A_CHUNK), lambda i: (0, 0, 0)),
                  pl.BlockSpec((A_CHUNK, A_GROUPS), lambda i: (0, 0))],
        out_specs=row,
        out_shape=jax.ShapeDtypeStruct((m, A_WIDTH), BF16),
        compiler_params=_cparams(("parallel",)),
        name="gmlp",
    )(u, v, ws, bs.T)


def _compress_kernel(x_ref, pool_ref, o_ref):
    pool = pool_ref[...]
    e = jnp.exp(pool - jnp.max(pool, axis=0, keepdims=True))
    w = e / jnp.sum(e, axis=0, keepdims=True)
    x = x_ref[...]
    nb = x.shape[0] // CMP_BLOCK
    o_ref[...] = jnp.sum(x.reshape(nb, CMP_BLOCK, KV_COLS) * w[None], axis=1)


def compress(rows, pool):
    m = rows.shape[0]
    tr = min(m, 1024)
    pool_cols = jnp.concatenate([jnp.repeat(pool.T, LANES, axis=1)] * 2, axis=1)
    return pl.pallas_call(
        _compress_kernel,
        grid=(m // tr,),
        in_specs=[pl.BlockSpec((tr, KV_COLS), lambda i: (i, 0)),
                  pl.BlockSpec((CMP_BLOCK, KV_COLS), lambda i: (0, 0))],
        out_specs=pl.BlockSpec((tr // CMP_BLOCK, KV_COLS), lambda i: (i, 0)),
        out_shape=jax.ShapeDtypeStruct((m // CMP_BLOCK, KV_COLS), F32),
        compiler_params=_cparams(("parallel",)),
        name="compress",
    )(rows, pool_cols)


NSA_TQ = 512
SEL_KC = 512
EXP2_SCALE = ATTN_SCALE * LOG2E


def _kv_gates(gate_ref, h):
    g = gate_ref[...]
    return jnp.where(h == 0, g[:, 0:GATES_PER_KV], g[:, GATES_PER_KV:2 * GATES_PER_KV])


def _pad_rows(x, rows):
    return jnp.concatenate([x, jnp.zeros((rows - x.shape[0], x.shape[1]), x.dtype)], axis=0)


def _nsa_prompt_kernel(qn_ref, qr_ref, kc_ref, vc_ref, ks_ref, vts_ref, kw_ref, vtw_ref, gate_ref, o_ref,
                       acc_ref, osel_ref, *, seq):
    h = pl.program_id(1)
    t = pl.program_id(2)
    nb = seq // CMP_BLOCK
    tq = NSA_TQ
    qpos_t = t * tq + _iota((1, tq), 1)
    blk_t = _iota((nb, 1), 0)

    heads = lambda ref: jnp.concatenate([ref[:, g * LANES:(g + 1) * LANES] for g in range(NSA_GROUP)], axis=0)
    wide = lambda x: jnp.concatenate([x] * NSA_GROUP, axis=1)
    qn4 = heads(qn_ref)
    qr4 = heads(qr_ref)

    kc = kc_ref[0].astype(BF16)
    vc_t = _pad_rows(vc_ref[0], LANES).T.astype(BF16)
    valid_ct = ((blk_t + 1) * CMP_BLOCK - 1) <= t * tq + _iota((1, NSA_GROUP * tq), 1) % tq
    st = jnp.where(valid_ct, _dot_nt(kc, qn4) * ATTN_SCALE, NEG)
    et = jnp.where(valid_ct, jnp.exp(st - jnp.max(st, axis=0, keepdims=True)), 0.0)
    pt = et / jnp.maximum(jnp.sum(et, axis=0, keepdims=True), TINY)
    o_cmp = _dot(vc_t, _pad_rows(pt, LANES).astype(BF16))
    imp_t = pt[:, 0:tq]
    for g in range(1, NSA_GROUP):
        imp_t = imp_t + pt[:, g * tq:(g + 1) * tq]

    cur_t = qpos_t // CMP_BLOCK
    forced = (blk_t == 0) | (blk_t == cur_t)
    score = jnp.where(forced, FORCE, jnp.where(blk_t > cur_t, NEG, imp_t))
    rank = jnp.zeros((nb, tq), jnp.int32)
    for j in range(nb):
        row = score[j:j + 1, :]
        beats = (row > score) | ((row == score) & (blk_t > j))
        rank = rank + beats.astype(jnp.int32)
    sel = _pad_rows((rank < min(SEL_TOPN, nb)).astype(F32), LANES).astype(BF16)

    def attend(lo, hi, k_ref, vt_ref, bias_fn):
        acc_ref[...] = jnp.zeros(acc_ref.shape, F32)

        def body(c, carry):
            m, l = carry
            start = pl.multiple_of(c * SEL_KC, SEL_KC)
            s = _dot_nt(k_ref[pl.ds(start, SEL_KC), :], qr4) + wide(bias_fn(c))
            m_new = jnp.maximum(m, jnp.max(s, axis=0, keepdims=True))
            p = jnp.exp2(s - m_new)
            alpha = jnp.exp2(m - m_new)
            acc_ref[...] = alpha * acc_ref[...] + _dot(vt_ref[c, 0], p.astype(BF16))
            return m_new, alpha * l + jnp.sum(p, axis=0, keepdims=True)

        init = (jnp.full((1, NSA_GROUP * tq), NEG, F32), jnp.zeros((1, NSA_GROUP * tq), F32))
        return lax.fori_loop(lo, hi, body, init)[1]

    def kpos_col(c):
        return c * SEL_KC + _iota((SEL_KC, 1), 0)

    def sel_bias(c):
        expand = ((c * SEL_KC + _iota((SEL_KC, LANES), 0)) // CMP_BLOCK == _iota((SEL_KC, LANES), 1)).astype(BF16)
        chosen = _dot(expand, sel) > 0.5
        return jnp.where(chosen & (kpos_col(c) <= qpos_t), 0.0, NEG)

    def win_bias(c):
        kpos = kpos_col(c)
        return jnp.where((kpos <= qpos_t) & (kpos >= qpos_t - WINDOW), 0.0, NEG)

    hi = (t * tq + tq - 1) // SEL_KC + 1
    l_sel = attend(0, hi, ks_ref, vts_ref, sel_bias)
    osel_ref[...] = acc_ref[...] / l_sel
    l_win = attend(jnp.maximum(t * tq - WINDOW, 0) // SEL_KC, hi, kw_ref, vtw_ref, win_bias)
    o_win = acc_ref[...] / l_win

    g_all = gate_ref[...]
    gate_t = jnp.where(h == 0, g_all, pltpu.roll(g_all, LANES - GATES_PER_KV, axis=1)).T
    for g in range(NSA_GROUP):
        cs = slice(g * tq, (g + 1) * tq)
        out_t = (gate_t[3 * g:3 * g + 1] * o_cmp[:, cs] + gate_t[3 * g + 1:3 * g + 2] * osel_ref[:, cs]
                 + gate_t[3 * g + 2:3 * g + 3] * o_win[:, cs])
        o_ref[:, g * LANES:(g + 1) * LANES] = out_t.T.astype(o_ref.dtype)


def nsa_prompt(qn, qr, kcmp, ks16, vts16, kw16, vtw16, gate, bsz, seq):
    nt = seq // NSA_TQ
    nb = seq // CMP_BLOCK
    nc = seq // SEL_KC
    assert nb <= LANES and seq % SEL_KC == 0
    qspec = pl.BlockSpec((NSA_TQ, NSA_GROUP * LANES), lambda b, h, t: (b * nt + t, h))
    cmp_k = pl.BlockSpec((1, nb, LANES), lambda b, h, t: (b, 0, h))
    cmp_v = pl.BlockSpec((1, nb, LANES), lambda b, h, t: (b, 0, KV_HEADS + h))
    key = pl.BlockSpec((seq, LANES), lambda b, h, t: (b, h))
    val_t = pl.BlockSpec((nc, 1, HEAD_DIM, SEL_KC), lambda b, h, t: (b, h, 0, 0))
    kc3 = kcmp.reshape(bsz, nb, KV_COLS)
    return pl.pallas_call(
        functools.partial(_nsa_prompt_kernel, seq=seq),
        grid=(bsz, KV_HEADS, nt),
        in_specs=[qspec, qspec, cmp_k, cmp_v, key, val_t, key, val_t,
                  pl.BlockSpec((NSA_TQ, LANES), lambda b, h, t: (b * nt + t, 0))],
        out_specs=qspec,
        out_shape=jax.ShapeDtypeStruct((bsz * seq, B_WIDTH), BF16),
        scratch_shapes=[pltpu.VMEM((HEAD_DIM, NSA_GROUP * NSA_TQ), F32),
                        pltpu.VMEM((HEAD_DIM, NSA_GROUP * NSA_TQ), F32)],
        compiler_params=_cparams(("parallel", "parallel", "arbitrary")),
        name="nsa_prompt",
    )(qn, qr, kc3, kc3, ks16, vts16, kw16, vtw16, gate)


PAGES_PER_STEP = 32
PAGE_ROWS = PAGE_SIZE * KV_SLOTS
BLOCK_ROWS = CMP_BLOCK * KV_SLOTS
BLOCKS_PER_PAGE = PAGE_SIZE // CMP_BLOCK


def _pool_past_kernel(pt_ref, *refs, pps):
    page_refs, pool_ref, o_ref = refs[:pps], refs[pps], refs[pps + 1]
    tiles = BLOCK_ROWS // SUBLANES
    pool = pool_ref[...].reshape(tiles, SUBLANES, LANES)
    fold = lambda a, op: op(a, pltpu.roll(a, KV_SLOTS, axis=0))
    mx = fold(jnp.max(pool, axis=0), jnp.maximum)
    e = jnp.exp(pool - mx[None])
    w = e / fold(jnp.sum(e, axis=0), jnp.add)[None]
    first = _iota((SUBLANES, LANES), 0) < KV_SLOTS
    for k, ref in enumerate(page_refs):
        x = ref[0].reshape(BLOCKS_PER_PAGE, tiles, SUBLANES, LANES)
        sums = [fold(jnp.sum(x[b] * w, axis=0), jnp.add) for b in range(BLOCKS_PER_PAGE)]
        o_ref[0, k * SUBLANES:(k + 1) * SUBLANES, :] = jnp.where(first, sums[0], sums[1])


def pool_past(cache, layer_off, page_table, pool):
    bsz, n_pages = page_table.shape
    pps = min(PAGES_PER_STEP, n_pages)
    steps = n_pages // pps
    assert BLOCKS_PER_PAGE * KV_SLOTS == SUBLANES
    pool_rows = jnp.broadcast_to(jnp.tile(pool.T, (1, 2)).reshape(BLOCK_ROWS, 1), (BLOCK_ROWS, LANES))

    def page_spec(k):
        return pl.BlockSpec((1, PAGE_ROWS, LANES),
                            lambda b, s, pt: (layer_off + pt[b * n_pages + s * pps + k], 0, 0))

    grid_spec = pltpu.PrefetchScalarGridSpec(
        num_scalar_prefetch=1,
        grid=(bsz, steps),
        in_specs=[page_spec(k) for k in range(pps)]
        + [pl.BlockSpec((BLOCK_ROWS, LANES), lambda b, s, pt: (0, 0))],
        out_specs=pl.BlockSpec((1, pps * SUBLANES, LANES), lambda b, s, pt: (b, s, 0)),
    )
    return pl.pallas_call(
        functools.partial(_pool_past_kernel, pps=pps),
        grid_spec=grid_spec,
        out_shape=jax.ShapeDtypeStruct((bsz, n_pages * SUBLANES, LANES), F32),
        compiler_params=_cparams(("parallel", "arbitrary")),
        name="pool_past",
    )(page_table.reshape(-1), *([cache] * pps), pool_rows)


def _sample_select_kernel(q_ref, kc_ref, oc_ref, idx_ref, score_ref, *, dec):
    bsz, rows = kc_ref.shape[0], kc_ref.shape[1]
    n_q = NSA_HEADS * dec
    rows_per_kv = NSA_GROUP * dec
    assert rows_per_kv == 2 * SUBLANES and dec <= SUBLANES // 2
    lane = _iota((n_q, rows), 1)
    is_key = lane % KV_SLOTS == _iota((n_q, 1), 0) // rows_per_kv
    lane8 = _iota((SUBLANES, rows), 1)
    for b in range(bsz):
        x = kc_ref[b].astype(BF16)
        s = jnp.where(is_key, _dot_nt(q_ref[b], x) * ATTN_SCALE, NEG)
        e = jnp.where(is_key, jnp.exp(s - jnp.max(s, axis=-1, keepdims=True)), 0.0)
        p = e / jnp.maximum(jnp.sum(e, axis=-1, keepdims=True), TINY)
        oc_ref[b] = _dot(pltpu.roll(p, KV_HEADS, axis=1).astype(BF16), x)
        for h in range(KV_HEADS):
            t = p[h * rows_per_kv:h * rows_per_kv + SUBLANES] + p[h * rows_per_kv + SUBLANES:(h + 1) * rows_per_kv]
            imp = t + pltpu.roll(t, SUBLANES // 2, axis=0)
            score = jnp.where((lane8 % KV_SLOTS == h) & (lane8 >= KV_SLOTS), imp, -jnp.inf)
            r0 = (b * KV_HEADS + h) * dec
            score_ref[r0:r0 + dec, :] = score[0:dec]
    score = score_ref[...]
    lane_all = _iota(score.shape, 1)
    slot = _iota((score.shape[0], SEL_TOPN), 1)
    picked = jnp.zeros((score.shape[0], SEL_TOPN), jnp.int32)
    for r in range(SEL_TOPN - 2):
        best = jnp.max(score, axis=-1, keepdims=True)
        arg = jnp.min(jnp.where(score == best, lane_all, rows), axis=-1, keepdims=True)
        picked = jnp.where(slot == r + 1, arg // KV_SLOTS, picked)
        score = jnp.where(lane_all == arg, -jnp.inf, score)
    idx_ref[...] = picked


def sample_select(qn, kcmp_past):
    bsz, dec, _ = qn.shape
    rows = kcmp_past.shape[1]
    n_sel = bsz * KV_HEADS * dec
    q_rows = qn.reshape(bsz, dec, NSA_HEADS, HEAD_DIM).swapaxes(1, 2).reshape(bsz, NSA_HEADS * dec, HEAD_DIM)
    o_cmp, idx = pl.pallas_call(
        functools.partial(_sample_select_kernel, dec=dec),
        out_shape=[jax.ShapeDtypeStruct((bsz, NSA_HEADS * dec, HEAD_DIM), F32),
                   jax.ShapeDtypeStruct((n_sel, SEL_TOPN), jnp.int32)],
        scratch_shapes=[pltpu.VMEM((n_sel, rows), F32)],
        compiler_params=pltpu.CompilerParams(vmem_limit_bytes=VMEM_LIMIT_BYTES),
        name="sample_select",
    )(q_rows, kcmp_past)
    o_cmp = o_cmp.reshape(bsz, NSA_HEADS, dec, HEAD_DIM).swapaxes(1, 2).reshape(bsz, dec, B_WIDTH)
    return o_cmp, idx.reshape(bsz, KV_HEADS * dec, SEL_TOPN)


def _rows_to_tile(rows):
    r = _iota((SUBLANES, rows[0].shape[1]), 0)
    out = jnp.zeros((SUBLANES, rows[0].shape[1]), rows[0].dtype)
    for i, x in enumerate(rows):
        out = jnp.where(r == i, jnp.broadcast_to(x, out.shape), out)
    return out


def _sample_attn_kernel(idx_ref, pt_ref, *refs, dec):
    n = SEL_TOPN
    blk_refs = refs[:dec * n]
    (qr_ref, ksn_ref, vsn_ref, wbuf_ref, kwn_ref, vwn_ref, gate_ref, oc_ref, o_ref) = refs[dec * n:]
    h = pl.program_id(1)
    q_all = jnp.concatenate(
        [_rows_to_tile([qr_ref[0, qi, :, g * LANES:(g + 1) * LANES].astype(F32) for g in range(NSA_GROUP)])
         for qi in range(dec)], axis=0).astype(BF16)
    rowq = _iota((dec * SUBLANES, 1), 0) // SUBLANES
    newpos = _iota((1, dec), 1)

    def attend(x_old, mask_old, k_new, v_new):
        s_old = jnp.where(mask_old, _dot_nt(q_all, x_old) * ATTN_SCALE, NEG)
        mask_new = newpos <= rowq
        s_new = jnp.where(mask_new, _dot_nt(q_all, k_new) * ATTN_SCALE, NEG)
        m = jnp.maximum(jnp.max(s_old, axis=-1, keepdims=True), jnp.max(s_new, axis=-1, keepdims=True))
        p_old = jnp.where(mask_old, jnp.exp(s_old - m), 0.0)
        p_new = jnp.where(mask_new, jnp.exp(s_new - m), 0.0)
        l = jnp.sum(p_old, axis=-1, keepdims=True) + jnp.sum(p_new, axis=-1, keepdims=True)
        pv = _dot(pltpu.roll(p_old, KV_HEADS, axis=1).astype(BF16), x_old)
        return (pv + _dot(p_new.astype(BF16), v_new)) / l

    x_sel = jnp.concatenate([r[0].astype(BF16) for r in blk_refs], axis=0)
    per_q = n * BLOCK_ROWS
    lane = _iota((1, dec * per_q), 1)
    mask_sel = (lane // per_q == rowq) & (lane % KV_SLOTS == h) & (lane % per_q < (n - 1) * BLOCK_ROWS)
    o_sel = attend(x_sel, mask_sel, ksn_ref[0].astype(BF16), vsn_ref[0].astype(BF16))

    wrows = wbuf_ref.shape[1]
    wb = wrows // KV_SLOTS
    lane_w = _iota((1, wrows), 1)
    mask_win = (lane_w % KV_SLOTS == h) & (lane_w // KV_SLOTS >= rowq + (wb - WINDOW))
    o_win = attend(wbuf_ref[0].astype(BF16), mask_win, kwn_ref[0].astype(BF16), vwn_ref[0].astype(BF16))

    for qi in range(dec):
        gate = _kv_gates(gate_ref.at[0, qi], h)
        outs = []
        for g in range(NSA_GROUP):
            r = qi * SUBLANES + g
            outs.append(gate[:, 3 * g:3 * g + 1] * oc_ref[0, qi, :, g * LANES:(g + 1) * LANES]
                        + gate[:, 3 * g + 1:3 * g + 2] * o_sel[r:r + 1]
                        + gate[:, 3 * g + 2:3 * g + 3] * o_win[r:r + 1])
        o_ref[0, qi] = jnp.concatenate(outs, axis=1).astype(o_ref.dtype)


def sample_attn(idx, page_table, cache_sel, sel_off, cache_win, win_off, qr, rows_s, rows_w, gate, o_cmp):
    bsz, dec, _ = rows_s.shape
    n_pages = page_table.shape[1]
    grp = NSA_GROUP * LANES

    def bspec(q, k):
        def imap(b, h, idx_r, pt_r):
            blk = idx_r[((b * KV_HEADS + h) * dec + q) * SEL_TOPN + k]
            page = sel_off + pt_r[b * n_pages + blk // BLOCKS_PER_PAGE]
            return (page * BLOCKS_PER_PAGE + blk % BLOCKS_PER_PAGE, 0, 0)
        return pl.BlockSpec((1, BLOCK_ROWS, LANES), imap)

    qmap = lambda b, h, idx_r, pt_r: (b, 0, 0, h)
    newk = pl.BlockSpec((1, dec, LANES), lambda b, h, idx_r, pt_r: (b, 0, h))
    newv = pl.BlockSpec((1, dec, LANES), lambda b, h, idx_r, pt_r: (b, 0, KV_HEADS + h))
    wbuf = pl.BlockSpec((1, cache_win.shape[1], LANES), lambda b, h, idx_r, pt_r: (win_off + b, 0, 0))
    grid_spec = pltpu.PrefetchScalarGridSpec(
        num_scalar_prefetch=2,
        grid=(bsz, KV_HEADS),
        in_specs=[bspec(q, k) for q in range(dec) for k in range(SEL_TOPN)]
        + [pl.BlockSpec((1, dec, 1, grp), qmap), newk, newv, wbuf, newk, newv,
           pl.BlockSpec((1, dec, 1, LANES), lambda b, h, idx_r, pt_r: (b, 0, 0, 0)),
           pl.BlockSpec((1, dec, 1, grp), qmap)],
        out_specs=pl.BlockSpec((1, dec, 1, grp), qmap),
    )
    q4 = lambda a: a.reshape(bsz, dec, 1, a.shape[-1])
    return pl.pallas_call(
        functools.partial(_sample_attn_kernel, dec=dec),
        grid_spec=grid_spec,
        out_shape=jax.ShapeDtypeStruct((bsz, dec, 1, B_WIDTH), BF16),
        compiler_params=_cparams(("parallel", "arbitrary")),
        name="sample_attn",
    )(idx.reshape(-1), page_table.reshape(-1), *([cache_sel] * (dec * SEL_TOPN)),
      q4(qr), rows_s, rows_s, cache_win, rows_w, rows_w, q4(gate), q4(o_cmp))


def _shift_rows(x, prev, k):
    xr = pltpu.roll(x, k, axis=0)
    pr = pltpu.roll(prev, k, axis=0)
    top = jnp.where(_iota((SUBLANES, 1), 0) < k, pr, xr[:SUBLANES])
    return jnp.concatenate([top, xr[SUBLANES:]], axis=0)


def _taps_kernel(*refs, width, act):
    x_refs, w_ref, b_ref, o_ref = refs[:width], refs[width], refs[width + 1], refs[width + 2]
    y = b_ref[...] + x_refs[0][...] * w_ref[0:1, :]
    for k in range(1, width):
        y = y + x_refs[k][...] * w_ref[k:k + 1, :]
    if act == "silu":
        o_ref[...] = y * jax.nn.sigmoid(y)
    else:
        o_ref[...] = y


def conv_taps(views, w, b, *, act, tile):
    width = len(views)
    m, n = views[0].shape
    col = pl.BlockSpec((m, tile), lambda j: (0, j))
    return pl.pallas_call(
        functools.partial(_taps_kernel, width=width, act=act),
        grid=(n // tile,),
        in_specs=[col] * width + [pl.BlockSpec((width, tile), lambda j: (0, j)),
                                  pl.BlockSpec((1, tile), lambda j: (0, j))],
        out_specs=col,
        out_shape=jax.ShapeDtypeStruct((m, n), F32),
        compiler_params=_cparams(("parallel",)),
        name="conv_taps_" + act,
    )(*views, w, b.reshape(1, -1))


def _glu_kernel(g_ref, u_ref, o_ref):
    g = g_ref[...]
    o_ref[...] = (g * jax.nn.sigmoid(g) * u_ref[...]).astype(o_ref.dtype)


def glu(hc, half, tile):
    m = hc.shape[0]
    nt = half // tile
    return pl.pallas_call(
        _glu_kernel,
        grid=(nt,),
        in_specs=[pl.BlockSpec((m, tile), lambda j: (0, j)), pl.BlockSpec((m, tile), lambda j: (0, nt + j))],
        out_specs=pl.BlockSpec((m, tile), lambda j: (0, j)),
        out_shape=jax.ShapeDtypeStruct((m, half), BF16),
        compiler_params=_cparams(("parallel",)),
        name="glu",
    )(hc, hc)


SSD_GPS = SSM_GROUPS


def _ssd_kernel(*refs, n_chunks, fuse_conv):
    c = pl.program_id(2)
    q = SSM_CHUNK
    e_heads = SSM_GROUP_HEADS
    gw = SSM_GROUP_WIDTH
    if fuse_conv:
        (xr_ref, br_ref, cr_ref, z_ref, cwx_ref, cwb_ref, cwc_ref, cbx_ref, cbb_ref, cbc_ref,
         dt_ref, dtt_ref, al_ref, alt_ref, dsk_ref, ng_ref, h0_ref,
         y_ref, ht_ref, tx_ref, tb_ref, tc_ref, st_ref, hx_ref, hb_ref, hc_ref) = refs

        def conv_act(raw_ref, hist_ref, cw_ref, cb_ref, tail_ref):
            x = raw_ref[...]
            prev = jnp.where(c == 0, 0.0, hist_ref[...])
            y = x * cw_ref[0, SSM_CONV - 1:SSM_CONV, :] + cb_ref[0]
            for d in range(1, SSM_CONV):
                y = y + _shift_rows(x, prev, d) * cw_ref[0, SSM_CONV - 1 - d:SSM_CONV - d, :]
            last = x[q - SUBLANES:, :]
            hist_ref[...] = last
            tail_ref[0] = last
            return y * jax.nn.sigmoid(y)

        xs_all = conv_act(xr_ref, hx_ref, cwx_ref, cbx_ref, tx_ref)
        bm_all = conv_act(br_ref, hb_ref, cwb_ref, cbb_ref, tb_ref)
        cm_all = conv_act(cr_ref, hc_ref, cwc_ref, cbc_ref, tc_ref)
    else:
        (xs_ref, bm_ref, cm_ref, z_ref, dt_ref, dtt_ref, al_ref, alt_ref, dsk_ref, ng_ref, h0_ref,
         y_ref, ht_ref, st_ref) = refs
        xs_all, bm_all, cm_all = xs_ref[...], bm_ref[...], cm_ref[...]

    @pl.when(c == 0)
    def _():
        st_ref[...] = h0_ref[0]

    ri = _iota((q, q), 0)
    ci = _iota((q, q), 1)
    tril = ri >= ci
    tril_b = tril.astype(BF16)
    triu_b = (ri <= ci).astype(BF16)
    expand = (_iota((e_heads, gw), 0) == _iota((e_heads, gw), 1) // SSM_HEAD_DIM).astype(BF16)
    lane = _iota((q, 2 * SSM_HEAD_DIM), 1)

    acs_all = _dot3_r(tril_b, dt_ref[...] * (-jnp.exp(al_ref[...])))
    acs_t_all = _dot3_l(dtt_ref[...] * (-jnp.exp(alt_ref[...])), triu_b)

    for k in range(SSD_GPS):
        gs = slice(k * gw, (k + 1) * gw)
        ns = slice(k * SSM_STATE, (k + 1) * SSM_STATE)
        hs = slice(k * e_heads, (k + 1) * e_heads)
        dt = dt_ref[:, hs]
        acs = acs_all[:, hs]
        acs_t = acs_t_all[hs, :]
        tot = acs[q - 1:q, :]

        xs = xs_all[:, gs]
        xdt = xs * _dot2_l(dt, expand)
        xdt_b = xdt.astype(BF16)
        bm = bm_all[:, ns]
        cm_b = cm_all[:, ns].astype(BF16)
        cb = _dot_nt(cm_b, bm.astype(BF16))

        y_parts = []
        for pair in range(e_heads // 2):
            cols = slice(pair * 2 * SSM_HEAD_DIM, (pair + 1) * 2 * SSM_HEAD_DIM)
            res = []
            for e in (2 * pair, 2 * pair + 1):
                seg = jnp.where(tril, acs[:, e:e + 1] - acs_t[e:e + 1, :], NEG)
                mix = (cb * jnp.exp(seg)).astype(BF16)
                res.append(_dot(mix, xdt_b[:, cols]))
            y_parts.append(jnp.where(lane < SSM_HEAD_DIM, res[0], res[1]))
        y = jnp.concatenate(y_parts, axis=1)

        state = st_ref[:, gs]
        ex = _dot2_l(jnp.concatenate([jnp.exp(acs), jnp.exp(tot - acs)], axis=0), expand)
        y = y + _dot(cm_b, state.astype(BF16)) * ex[0:q]
        contrib = _dot(bm.T.astype(BF16), (xdt * ex[q:2 * q]).astype(BF16))
        st_ref[:, gs] = state * ex[q - 1:q] + contrib

        y = y + xs * dsk_ref[:, gs]
        z = z_ref[:, gs]
        y = y * (z * jax.nn.sigmoid(z))
        ms = jnp.mean(y * y, axis=-1, keepdims=True)
        y_ref[:, gs] = (y * lax.rsqrt(ms + EPS) * ng_ref[:, gs]).astype(y_ref.dtype)

    @pl.when(c == n_chunks - 1)
    def _():
        ht_ref[0] = st_ref[...]


def ssd(xbc, xbc_col0, zsrc, dt, a_log, d_skip, norm_g, h0, bsz, seq, conv=None):
    nc = seq // SSM_CHUNK
    gps = SSD_GPS
    gw = SSM_GROUP_WIDTH * gps
    nw = SSM_STATE * gps
    e = SSM_GROUP_HEADS
    m = bsz * seq
    assert gps == SSM_GROUPS
    dt_t = dt.T
    al = a_log.astype(F32).reshape(1, SSM_HEADS)
    al_t = a_log.astype(F32).reshape(SSM_HEADS, 1)
    dsk = jnp.repeat(d_skip.astype(F32), SSM_HEAD_DIM).reshape(1, SSM_INNER)
    bc = SSM_GROUPS * SSM_STATE
    x_off = xbc_col0 // gw
    b_off = (xbc_col0 + SSM_INNER) // nw
    c_off = (xbc_col0 + SSM_INNER + bc) // nw
    rowmap = lambda col: (lambda b, g, c: (b * nc + c, col(g)))
    xbc_specs = [pl.BlockSpec((SSM_CHUNK, gw), rowmap(lambda g: x_off + g)),
                 pl.BlockSpec((SSM_CHUNK, nw), rowmap(lambda g: b_off + g)),
                 pl.BlockSpec((SSM_CHUNK, nw), rowmap(lambda g: c_off + g)),
                 pl.BlockSpec((SSM_CHUNK, gw), rowmap(lambda g: g))]
    rest_specs = [pl.BlockSpec((SSM_CHUNK, SSM_HEADS), lambda b, g, c: (b * nc + c, 0)),
                  pl.BlockSpec((SSM_HEADS, SSM_CHUNK), lambda b, g, c: (0, b * nc + c)),
                  pl.BlockSpec((1, SSM_HEADS), lambda b, g, c: (0, 0)),
                  pl.BlockSpec((SSM_HEADS, 1), lambda b, g, c: (0, 0)),
                  pl.BlockSpec((1, gw), lambda b, g, c: (0, g)),
                  pl.BlockSpec((1, gw), lambda b, g, c: (0, g)),
                  pl.BlockSpec((1, SSM_STATE, gw), lambda b, g, c: (b, 0, g))]
    rest_args = [dt, dt_t, al, al_t, dsk, norm_g.astype(F32).reshape(1, SSM_INNER), h0]
    out_specs = [pl.BlockSpec((SSM_CHUNK, gw), rowmap(lambda g: g)),
                 pl.BlockSpec((1, SSM_STATE, gw), lambda b, g, c: (b, 0, g))]
    out_shape = [jax.ShapeDtypeStruct((m, SSM_INNER), BF16),
                 jax.ShapeDtypeStruct((bsz, SSM_STATE, SSM_INNER), F32)]
    scratch = [pltpu.VMEM((SSM_STATE, gw), F32)]
    conv_specs, conv_args = [], []
    if conv is not None:
        cw, cb, layer = conv
        cb3 = cb.reshape(cb.shape[0], 1, -1)
        offs = ((gw, 0), (nw, SSM_INNER // nw), (nw, (SSM_INNER + bc) // nw))
        conv_specs = [pl.BlockSpec((1, SSM_CONV, w), lambda b, g, c, o=o: (layer, 0, o + g)) for w, o in offs]
        conv_specs += [pl.BlockSpec((1, 1, w), lambda b, g, c, o=o: (layer, 0, o + g)) for w, o in offs]
        conv_args = [cw, cw, cw, cb3, cb3, cb3]
        for w, total in ((gw, SSM_INNER), (nw, bc), (nw, bc)):
            out_specs.append(pl.BlockSpec((1, SUBLANES, w), lambda b, g, c: (b, 0, g)))
            out_shape.append(jax.ShapeDtypeStruct((bsz, SUBLANES, total), F32))
            scratch.append(pltpu.VMEM((SUBLANES, w), F32))
    return pl.pallas_call(
        functools.partial(_ssd_kernel, n_chunks=nc, fuse_conv=conv is not None),
        grid=(bsz, SSM_GROUPS // gps, nc),
        in_specs=xbc_specs + conv_specs + rest_specs,
        out_specs=out_specs,
        out_shape=out_shape,
        scratch_shapes=scratch,
        compiler_params=_cparams(("parallel", "parallel", "arbitrary")),
        name="ssd",
    )(xbc, xbc, xbc, zsrc, *conv_args, *rest_args)


def _state_to_cols(h):
    b = h.shape[0]
    return h.transpose(0, 3, 1, 2).reshape(b, SSM_STATE, SSM_INNER)


def _state_from_cols(s):
    b = s.shape[0]
    return s.reshape(b, SSM_STATE, SSM_HEADS, SSM_HEAD_DIM).transpose(0, 2, 3, 1)


FFN_BM = 1024
FFN_SUB = 256
FFN_TILE = 512
FFN_TILES = D_FF // FFN_TILE


def _conv3(h, prev, cw_ref, cb_ref):
    return (_shift_rows(h, prev, 2) * cw_ref[0, 0:1, :] + _shift_rows(h, prev, 1) * cw_ref[0, 1:2, :]
            + h * cw_ref[0, 2:3, :] + cb_ref[0])


def _ffn_up_kernel(x_ref, xs_ref, wg_ref, wu_ref, cwg_ref, cwu_ref, cbg_ref, cbu_ref,
                   act_ref, tg_ref, tu_ref, sg_ref, su_ref, wgb_ref, wub_ref, cg_ref, cu_ref, *, tiles_per_seq):
    i = pl.program_id(1)

    @pl.when(i == 0)
    def _():
        wgb_ref[...] = wg_ref[0].astype(BF16)
        wub_ref[...] = wu_ref[0].astype(BF16)
        sg_ref[...] = _dot(xs_ref[...], wgb_ref[...])
        su_ref[...] = _dot(xs_ref[...], wub_ref[...])

    @pl.when(i % tiles_per_seq == 0)
    def _():
        cg_ref[...] = jnp.zeros_like(cg_ref)
        cu_ref[...] = jnp.zeros_like(cu_ref)

    for r in range(FFN_BM // FFN_SUB):
        rs = slice(r * FFN_SUB, (r + 1) * FFN_SUB)
        x = x_ref[rs, :]
        hg = _dot(x, wgb_ref[...])
        hu = _dot(x, wub_ref[...])
        g = _conv3(hg, cg_ref[...], cwg_ref, cbg_ref)
        u = _conv3(hu, cu_ref[...], cwu_ref, cbu_ref)
        act_ref[rs, :] = (g * jax.nn.sigmoid(g) * u).astype(act_ref.dtype)
        cg_ref[...] = hg[FFN_SUB - SUBLANES:, :]
        cu_ref[...] = hu[FFN_SUB - SUBLANES:, :]
    tg_ref[0] = cg_ref[...]
    tu_ref[0] = cu_ref[...]


def ffn_up(xp, xs, w_up, conv_w, conv_b, layer, bsz, seq):
    m, k = xp.shape
    ms = xs.shape[0]
    tiles_per_seq = seq // FFN_BM
    nt = FFN_TILES
    wspec = lambda off: pl.BlockSpec((1, k, FFN_TILE), lambda j, i: (layer, 0, off + j))
    cwspec = lambda off: pl.BlockSpec((1, FFN_CONV, FFN_TILE), lambda j, i: (layer, 0, off + j))
    cbspec = lambda off: pl.BlockSpec((1, 1, FFN_TILE), lambda j, i: (layer, 0, off + j))
    tail = pl.BlockSpec((1, SUBLANES, FFN_TILE), lambda j, i: (i // tiles_per_seq, 0, j))
    samp = pl.BlockSpec((ms, FFN_TILE), lambda j, i: (0, j))
    cb3 = conv_b.reshape(conv_b.shape[0], 1, -1)
    carry = pltpu.VMEM((SUBLANES, FFN_TILE), F32)
    return pl.pallas_call(
        functools.partial(_ffn_up_kernel, tiles_per_seq=tiles_per_seq),
        grid=(nt, m // FFN_BM),
        in_specs=[pl.BlockSpec((FFN_BM, k), lambda j, i: (i, 0)), pl.BlockSpec((ms, k), lambda j, i: (0, 0)),
                  wspec(0), wspec(nt), cwspec(0), cwspec(nt), cbspec(0), cbspec(nt)],
        out_specs=[pl.BlockSpec((FFN_BM, FFN_TILE), lambda j, i: (i, j)), tail, tail, samp, samp],
        out_shape=[jax.ShapeDtypeStruct((m, D_FF), BF16),
                   jax.ShapeDtypeStruct((bsz, SUBLANES, D_FF), F32), jax.ShapeDtypeStruct((bsz, SUBLANES, D_FF), F32),
                   jax.ShapeDtypeStruct((ms, D_FF), F32), jax.ShapeDtypeStruct((ms, D_FF), F32)],
        scratch_shapes=[pltpu.VMEM((k, FFN_TILE), BF16), pltpu.VMEM((k, FFN_TILE), BF16), carry, carry],
        compiler_params=_cparams(("parallel", "arbitrary")),
        name="ffn_up",
    )(xp, xs, w_up, w_up, conv_w, conv_w, cb3, cb3)


def _rope_tables(pos):
    half = HEAD_DIM // 2
    inv = ROPE_THETA ** (-jnp.arange(half, dtype=F32) / half)
    ang = pos.astype(F32)[:, None] * inv[None, :]
    cos, sin = jnp.cos(ang), jnp.sin(ang)
    return jnp.concatenate([cos, cos], axis=1), jnp.concatenate([-sin, sin], axis=1)


def _kv_out(rows, bsz):
    return rows.reshape(bsz, -1, 2, KV_HEADS, HEAD_DIM)


def kernel(x_prompt, x_sample, cache_kv_cmp, cache_kv_sel, cache_kv_win, state_ssm_conv, state_ssm, state_ffn_conv, page_table, norm_mix, norm_ffn, w_in_even, w_out_even, gmlp_v_norm, gmlp_ws, gmlp_bs, q_norm, k_norm, cmp_pool, w_in_odd, ssm_conv_w, ssm_conv_b, ssm_dt_bias, ssm_a_log, ssm_d, ssm_norm, w_out_odd, ffn_w_up, ffn_conv_w, ffn_conv_b, ffn_w_down):
    bp, sp, _ = x_prompt.shape
    bs, ss, _ = x_sample.shape
    depth = norm_mix.shape[0]
    n_pool = cache_kv_cmp.shape[1]
    n_pages = page_table.shape[1]
    past_len = n_pages * PAGE_SIZE
    wb = cache_kv_win.shape[2]
    mp, ms = bp * sp, bs * ss

    cos_p, sin_p = _rope_tables(jnp.arange(sp, dtype=jnp.int32))
    cos_p, sin_p = jnp.tile(cos_p, (bp, 1)), jnp.tile(sin_p, (bp, 1))
    cos_s, sin_s = _rope_tables(past_len + jnp.arange(ss, dtype=jnp.int32))
    cos_s, sin_s = jnp.tile(cos_s, (bs, 1)), jnp.tile(sin_s, (bs, 1))

    cache_c = cache_kv_cmp.reshape(-1, PAGE_ROWS, LANES)
    cache_s = cache_kv_sel.reshape(-1, BLOCK_ROWS, LANES)
    cache_w = cache_kv_win.reshape(-1, wb * KV_SLOTS, LANES)

    w_down16 = ffn_w_down.astype(BF16)
    w_out_odd16 = w_out_odd.astype(BF16)
    w_in_even_t = jnp.swapaxes(w_in_even, 1, 2)
    w_in_odd_t = jnp.swapaxes(w_in_odd, 1, 2)

    hp = x_prompt.reshape(mp, D_MODEL)
    hs = x_sample.reshape(ms, D_MODEL)
    outs = {k: [] for k in ("p_cmp", "p_sel", "p_win", "s_cmp", "s_sel", "s_win", "s_v",
                            "p_sconv", "p_sst", "s_sconv", "s_sst", "p_fconv", "s_fconv")}

    for layer in range(depth):
        i = layer // 2
        xp = rms_cast(hp, norm_mix[layer])
        xs = rms_cast(hs, norm_mix[layer])
        if layer % 2 == 0:
            w_gate = w_in_even_t[i, E_MAIN:]
            proj_p, proj_s = matmul_ws([xp], [xs], w_in_even_t, i, E_MAIN, bn=768, w_t=True, name="even_in")
            gl = matmul_narrow(xp, w_gate)
            u, v, qn, qr, rc, rs, rw, gate, ks16, kw16, vts16, vtw16, rc_stored = even_prep(
                proj_p, gl, cos_p, sin_p, gmlp_v_norm[i], q_norm[i], k_norm[i], for_mxu=True)
            a_p = gmlp(u, v, gmlp_ws[i], gmlp_bs[i])
            kcmp = compress(rc, cmp_pool[i])
            b_p = nsa_prompt(qn, qr, kcmp, ks16, vts16, kw16, vtw16, gate, bp, sp)
            outs["p_cmp"].append(_kv_out(rc_stored, bp))
            outs["p_sel"].append(_kv_out(rs, bp))
            outs["p_win"].append(_kv_out(rw, bp)[:, sp - min(WINDOW, sp):])
            gl = matmul_narrow(xs, w_gate)
            u, v, qn, qr, rc, rs, rw, gate = even_prep(
                proj_s, gl, cos_s, sin_s, gmlp_v_norm[i], q_norm[i], k_norm[i], for_mxu=False)
            lpad = ((0, 0), (0, A_CHUNK - ss), (0, 0))
            a_s = gmlp(jnp.pad(u.reshape(bs, ss, -1), lpad).reshape(bs * A_CHUNK, -1),
                       jnp.pad(v.reshape(bs, ss, -1), lpad).reshape(bs * A_CHUNK, -1),
                       gmlp_ws[i], gmlp_bs[i]).reshape(bs, A_CHUNK, -1)[:, :ss].reshape(ms, -1)
            kc_past = pool_past(cache_c, i * n_pool, page_table, cmp_pool[i])
            o_cmp, idx = sample_select(qn.reshape(bs, ss, -1), kc_past)
            b_s = sample_attn(idx, page_table, cache_s, i * n_pool, cache_w, i * bs,
                              qr.reshape(bs, ss, -1), rs.reshape(bs, ss, -1), rw.reshape(bs, ss, -1),
                              gate.reshape(bs, ss, -1), o_cmp).reshape(ms, B_WIDTH)
            outs["s_cmp"].append(_kv_out(rc, bs))
            outs["s_sel"].append(_kv_out(rs, bs))
            outs["s_win"].append(_kv_out(rw, bs))
            outs["s_v"].append(v.reshape(bs, ss, A_WIDTH))
            hp, hs = matmul_ws([a_p, b_p], [a_s, b_s], w_out_even, i, D_MODEL, bn=1024, res=(hp, hs), name="even_out")
        else:
            zx = SSM_INNER + SSM_CONV_DIM
            w_dt = w_in_odd_t[i, zx:]
            proj_p, proj_s = matmul_ws([xp], [xs], w_in_odd_t, i, zx, bn=1024, w_t=True, name="odd_in")
            dt = matmul_narrow(xp, w_dt, ssm_dt_bias[i])[:, :SSM_HEADS]
            h0 = jnp.zeros((bp, SSM_STATE, SSM_INNER), F32)
            y, ht, *tails = ssd(proj_p, SSM_INNER, proj_p, dt, ssm_a_log[i], ssm_d[i], ssm_norm[i], h0, bp, sp,
                                conv=(ssm_conv_w, ssm_conv_b, i))
            tail = jnp.concatenate(tails, axis=2)
            hp = matmul(y, w_out_odd16, layer=i, bn=512, res=hp, name="odd_out")
            outs["p_sconv"].append(tail[:, SUBLANES - (SSM_CONV - 1):])
            outs["p_sst"].append(_state_from_cols(ht))
            dt = matmul_narrow(xs, w_dt, ssm_dt_bias[i])[:, :SSM_HEADS]
            xin = jnp.concatenate([state_ssm_conv[i], proj_s[:, SSM_INNER:].reshape(bs, ss, -1)], axis=1)
            views = [xin[:, k:k + ss].reshape(ms, -1) for k in range(SSM_CONV)]
            xbc = conv_taps(views, ssm_conv_w[i], ssm_conv_b[i], act="silu", tile=1024)
            cpad = ((0, 0), (0, SSM_CHUNK - ss), (0, 0))
            padrows = lambda a: jnp.pad(a.reshape(bs, ss, -1), cpad).reshape(bs * SSM_CHUNK, -1)
            y, ht = ssd(padrows(xbc), 0, padrows(proj_s[:, :SSM_INNER]), padrows(dt), ssm_a_log[i], ssm_d[i], ssm_norm[i],
                        _state_to_cols(state_ssm[i].astype(F32)), bs, SSM_CHUNK)
            y = y.reshape(bs, SSM_CHUNK, -1)[:, :ss].reshape(ms, -1)
            hs = matmul(y, w_out_odd16, layer=i, bn=1024, res=hs, name="odd_out_s")
            outs["s_sconv"].append(xin[:, ss:])
            outs["s_sst"].append(_state_from_cols(ht))
        xp = rms_cast(hp, norm_ffn[layer])
        xs = rms_cast(hs, norm_ffn[layer])
        act, tail_g, tail_u, hu_g, hu_u = ffn_up(xp, xs, ffn_w_up, ffn_conv_w, ffn_conv_b, layer, bp, sp)
        hp = matmul(act, w_down16, layer=layer, bn=512, res=hp, name="ffn_down")
        outs["p_fconv"].append(jnp.concatenate([tail_g, tail_u], axis=2)[:, SUBLANES - (FFN_CONV - 1):])
        hu = jnp.concatenate([hu_g, hu_u], axis=1)
        xin = jnp.concatenate([state_ffn_conv[layer], hu.reshape(bs, ss, -1)], axis=1)
        views = [xin[:, k:k + ss].reshape(ms, -1) for k in range(FFN_CONV)]
        hc = conv_taps(views, ffn_conv_w[layer], ffn_conv_b[layer], act="none", tile=1024)
        hs = matmul(glu(hc, D_FF, 512), w_down16, layer=layer, bn=512, res=hs, name="ffn_down_s")
        outs["s_fconv"].append(xin[:, ss:])

    st = lambda k: jnp.stack(outs[k])
    return (hp.reshape(bp, sp, D_MODEL), hs.reshape(bs, ss, D_MODEL), st("p_cmp"), st("p_sel"), st("p_win"),
            st("p_sconv"), st("p_sst"), st("p_fconv"), st("s_cmp"), st("s_sel"), st("s_win"), st("s_v"),
            st("s_sconv"), st("s_sst"), st("s_fconv"))
```

```python
import functools

import jax
import jax.numpy as jnp
from jax import lax
from jax.experimental import pallas as pl
from jax.experimental.pallas import tpu as pltpu

F32 = jnp.float32
BF16 = jnp.bfloat16

D_MODEL = 2048
A_WIDTH = D_MODEL // 2
A_GROUPS = 8
A_CHUNK = 128
NSA_HEADS = 8
HEAD_DIM = 128
KV_HEADS = 2
NSA_GROUP = NSA_HEADS // KV_HEADS
B_WIDTH = NSA_HEADS * HEAD_DIM
KV_COLS = 2 * KV_HEADS * HEAD_DIM
KV_SLOTS = 2 * KV_HEADS
CMP_BLOCK = 64
SEL_TOPN = 16
WINDOW = 512
PAGE_SIZE = 128
ROPE_THETA = 10000.0
ATTN_SCALE = HEAD_DIM ** -0.5
LOG2E = 1.4426950408889634
SSM_INNER = 2 * D_MODEL
SSM_HEAD_DIM = 64
SSM_HEADS = SSM_INNER // SSM_HEAD_DIM
SSM_GROUPS = 8
SSM_GROUP_HEADS = SSM_HEADS // SSM_GROUPS
SSM_GROUP_WIDTH = SSM_INNER // SSM_GROUPS
SSM_STATE = 128
SSM_CONV = 4
SSM_CHUNK = 128
SSM_CONV_DIM = SSM_INNER + 2 * SSM_GROUPS * SSM_STATE
D_FF = 5632
FFN_CONV = 3
EPS = 1e-6
NEG = -1e30
FORCE = 1e4
TINY = 1e-30

VMEM_LIMIT_BYTES = 56 * 1024 * 1024
LANES = 128
SUBLANES = 8

E_MAIN = 2 * A_WIDTH + B_WIDTH + 3 * KV_COLS
N_GATES = 3 * NSA_HEADS
GATES_PER_KV = 3 * NSA_GROUP


def _cparams(sem):
    return pltpu.CompilerParams(dimension_semantics=sem, vmem_limit_bytes=VMEM_LIMIT_BYTES)


def _dot(a, b):
    return jnp.dot(a, b, preferred_element_type=F32)


def _dot_nt(a, b):
    return lax.dot_general(a, b, (((1,), (1,)), ((), ())), preferred_element_type=F32)


def _iota(shape, dim):
    return lax.broadcasted_iota(jnp.int32, shape, dim)


def _split3(x):
    hi = x.astype(BF16)
    r1 = x - hi.astype(F32)
    mid = r1.astype(BF16)
    lo = (r1 - mid.astype(F32)).astype(BF16)
    return hi, mid, lo


def _dot3_l(x, m):
    hi, mid, lo = _split3(x)
    return _dot(hi, m) + _dot(mid, m) + _dot(lo, m)


def _dot3_r(m, x):
    hi, mid, lo = _split3(x)
    return _dot(m, hi) + _dot(m, mid) + _dot(m, lo)


def _dot2_l(x, m):
    hi = x.astype(BF16)
    return _dot(hi, m) + _dot((x - hi.astype(F32)).astype(BF16), m)


def _rms_kernel(x_ref, g_ref, o_ref):
    x = x_ref[...]
    ms = jnp.mean(x * x, axis=-1, keepdims=True)
    o_ref[...] = (x * lax.rsqrt(ms + EPS) * g_ref[...]).astype(o_ref.dtype)


def rms_cast(x, g):
    m, d = x.shape
    tr = min(m, 1024)
    return pl.pallas_call(
        _rms_kernel,
        grid=(m // tr,),
        in_specs=[pl.BlockSpec((tr, d), lambda i: (i, 0)), pl.BlockSpec((1, d), lambda i: (0, 0))],
        out_specs=pl.BlockSpec((tr, d), lambda i: (i, 0)),
        out_shape=jax.ShapeDtypeStruct((m, d), BF16),
        compiler_params=_cparams(("parallel",)),
        name="rms_cast",
    )(x, g.reshape(1, d))


def _mm_kernel(x_ref, w_ref, o_ref):
    o_ref[...] = _dot(x_ref[...], w_ref[0]).astype(o_ref.dtype)


def _mm_res_kernel(x_ref, w_ref, r_ref, o_ref):
    o_ref[...] = (r_ref[...] + _dot(x_ref[...], w_ref[0])).astype(o_ref.dtype)


def matmul(x, w, *, bn, layer=0, res=None, out_dtype=F32, name="mm"):
    m, k = x.shape
    if w.ndim == 2:
        w = w[None]
    n = w.shape[2]
    bm = min(m, 1024)
    in_specs = [pl.BlockSpec((bm, k), lambda i, j: (i, 0)), pl.BlockSpec((1, k, bn), lambda i, j: (layer, 0, j))]
    args = [x, w]
    kern = _mm_kernel
    if res is not None:
        in_specs.append(pl.BlockSpec((bm, bn), lambda i, j: (i, j)))
        args.append(res)
        kern = _mm_res_kernel
    return pl.pallas_call(
        kern,
        grid=(m // bm, n // bn),
        in_specs=in_specs,
        out_specs=pl.BlockSpec((bm, bn), lambda i, j: (i, j)),
        out_shape=jax.ShapeDtypeStruct((m, n), out_dtype),
        compiler_params=_cparams(("parallel", "parallel")),
        name=name,
    )(*args)


WS_BM = 1024


def _mm_ws_kernel(*refs, n_parts, has_res, w_t):
    xp, xs, w_ref = refs[:n_parts], refs[n_parts:2 * n_parts], refs[2 * n_parts]
    pos = 2 * n_parts + 1
    if has_res:
        r_ref, rs_ref = refs[pos], refs[pos + 1]
        pos += 2
    o_ref, os_ref, wb_ref = refs[pos], refs[pos + 1], refs[pos + 2]

    def mm(parts):
        acc, k0 = None, 0
        for p in parts:
            kp = p.shape[1]
            t = _dot(p[...], wb_ref[k0:k0 + kp, :])
            acc = t if acc is None else acc + t
            k0 += kp
        return acc

    @pl.when(pl.program_id(1) == 0)
    def _():
        wb_ref[...] = (w_ref[0].T if w_t else w_ref[0]).astype(BF16)
        ys = mm(xs)
        if has_res:
            ys = rs_ref[...] + ys
        os_ref[...] = ys.astype(os_ref.dtype)

    y = mm(xp)
    if has_res:
        y = r_ref[...] + y
    o_ref[...] = y.astype(o_ref.dtype)


def matmul_ws(xp_parts, xs_parts, w, layer, n_cols, *, bn, w_t=False, res=None, name="mm_ws"):
    mp = xp_parts[0].shape[0]
    ms = xs_parts[0].shape[0]
    k = w.shape[2] if w_t else w.shape[1]
    n_parts = len(xp_parts)
    in_specs = [pl.BlockSpec((WS_BM, p.shape[1]), lambda j, i: (i, 0)) for p in xp_parts]
    in_specs += [pl.BlockSpec((ms, p.shape[1]), lambda j, i: (0, 0)) for p in xs_parts]
    in_specs.append(pl.BlockSpec((1, bn, k), lambda j, i: (layer, j, 0)) if w_t
                    else pl.BlockSpec((1, k, bn), lambda j, i: (layer, 0, j)))
    args = list(xp_parts) + list(xs_parts) + [w]
    if res is not None:
        in_specs += [pl.BlockSpec((WS_BM, bn), lambda j, i: (i, j)), pl.BlockSpec((ms, bn), lambda j, i: (0, j))]
        args += list(res)
    return pl.pallas_call(
        functools.partial(_mm_ws_kernel, n_parts=n_parts, has_res=res is not None, w_t=w_t),
        grid=(n_cols // bn, mp // WS_BM),
        in_specs=in_specs,
        out_specs=[pl.BlockSpec((WS_BM, bn), lambda j, i: (i, j)), pl.BlockSpec((ms, bn), lambda j, i: (0, j))],
        out_shape=[jax.ShapeDtypeStruct((mp, n_cols), F32), jax.ShapeDtypeStruct((ms, n_cols), F32)],
        scratch_shapes=[pltpu.VMEM((k, bn), BF16)],
        compiler_params=_cparams(("parallel", "arbitrary")),
        name=name,
    )(*args)


def _mm_narrow_kernel(x_ref, wt_ref, b_ref, o_ref, *, softplus):
    y = _dot_nt(x_ref[...], wt_ref[...].astype(BF16))
    if softplus:
        y = jax.nn.softplus(y + b_ref[...])
    o_ref[...] = y


def matmul_narrow(x, wt, bias=None):
    m, k = x.shape
    n = wt.shape[0]
    bm = min(m, 1024)
    b = jnp.zeros((LANES,), F32) if bias is None else jnp.pad(bias.astype(F32), (0, LANES - n))
    return pl.pallas_call(
        functools.partial(_mm_narrow_kernel, softplus=bias is not None),
        grid=(m // bm,),
        in_specs=[pl.BlockSpec((bm, k), lambda i: (i, 0)), pl.BlockSpec((LANES, k), lambda i: (0, 0)),
                  pl.BlockSpec((1, LANES), lambda i: (0, 0))],
        out_specs=pl.BlockSpec((bm, LANES), lambda i: (i, 0)),
        out_shape=jax.ShapeDtypeStruct((m, LANES), F32),
        compiler_params=_cparams(("parallel",)),
        name="mm_narrow",
    )(x, jnp.pad(wt, ((0, LANES - n), (0, 0))), b.reshape(1, LANES))


def _head_rms(x, gain):
    ms = jnp.mean(x * x, axis=-1, keepdims=True)
    return x * lax.rsqrt(ms + EPS) * gain


def _rope(x, cosf, sinf):
    return x * cosf + pltpu.roll(x, HEAD_DIM // 2, axis=1) * sinf


def _even_prep_kernel(p_ref, gl_ref, cos_ref, sin_ref, vg_ref, qn_ref, kn_ref,
                      u_ref, v_ref, q_ref, qr_ref, rc_ref, rs_ref, rw_ref, gate_ref, *mxu_refs):
    cosf = cos_ref[...]
    sinf = sin_ref[...]
    u_ref[...] = jax.nn.gelu(p_ref[:, 0:A_WIDTH]).astype(u_ref.dtype)
    for g in range(A_GROUPS):
        sl = slice(g * LANES, (g + 1) * LANES)
        vg = jax.nn.gelu(p_ref[:, A_WIDTH + g * LANES:A_WIDTH + (g + 1) * LANES])
        v_ref[:, sl] = _head_rms(vg, vg_ref[:, sl])
    qgain = qn_ref[...]
    for h in range(NSA_HEADS):
        sl = slice(h * LANES, (h + 1) * LANES)
        q = _head_rms(p_ref[:, 2 * A_WIDTH + h * LANES:2 * A_WIDTH + (h + 1) * LANES], qgain)
        q_ref[:, sl] = q.astype(q_ref.dtype)
        q_rot = _rope(q, cosf, sinf)
        if mxu_refs:
            q_rot = q_rot * EXP2_SCALE
        qr_ref[:, sl] = q_rot.astype(qr_ref.dtype)
    base = 2 * A_WIDTH + B_WIDTH
    for which, row_ref in enumerate((rc_ref, rs_ref, rw_ref)):
        gain = kn_ref[which:which + 1, :]
        off = base + which * KV_COLS
        k16_ref, vt_ref = (mxu_refs[which - 1], mxu_refs[which + 1]) if (mxu_refs and which > 0) else (None, None)
        stored_ref = (mxu_refs[4] if which == 0 else row_ref) if mxu_refs else None
        tr = p_ref.shape[0]
        for h in range(KV_HEADS):
            sl = slice(h * LANES, (h + 1) * LANES)
            k = _head_rms(p_ref[:, off + h * LANES:off + (h + 1) * LANES], gain)
            if which > 0:
                k = _rope(k, cosf, sinf)
            vals = p_ref[:, off + (KV_HEADS + h) * LANES:off + (KV_HEADS + h + 1) * LANES]
            if stored_ref is not None:
                stored_ref[pl.ds(h, tr, stride=KV_SLOTS), :] = k
                stored_ref[pl.ds(KV_HEADS + h, tr, stride=KV_SLOTS), :] = vals
            if stored_ref is not row_ref:
                row_ref[:, sl] = k
                row_ref[:, (KV_HEADS + h) * LANES:(KV_HEADS + h + 1) * LANES] = vals
            if k16_ref is not None:
                k16_ref[:, sl] = k.astype(BF16)
                vt_ref[0, h] = vals.T.astype(BF16)
    gate_ref[...] = jax.nn.sigmoid(gl_ref[...])


def even_prep(proj, gl, cosf, sinf, v_gain, q_gain, k_gain, *, for_mxu):
    m = proj.shape[0]
    tr = min(m, SEL_KC)
    row = lambda w: pl.BlockSpec((tr, w), lambda i: (i, 0))
    full = lambda r, w: pl.BlockSpec((r, w), lambda i: (0, 0))
    out_shapes = [
        jax.ShapeDtypeStruct((m, A_WIDTH), BF16),
        jax.ShapeDtypeStruct((m, A_WIDTH), F32),
        jax.ShapeDtypeStruct((m, B_WIDTH), BF16),
        jax.ShapeDtypeStruct((m, B_WIDTH), BF16),
        jax.ShapeDtypeStruct((m, KV_COLS), F32),
        jax.ShapeDtypeStruct((m, KV_COLS), F32),
        jax.ShapeDtypeStruct((m, KV_COLS), F32),
        jax.ShapeDtypeStruct((m, LANES), F32),
    ]
    out_specs = [row(s.shape[1]) for s in out_shapes]
    if for_mxu:
        assert tr == SEL_KC
        stored = jax.ShapeDtypeStruct((m * KV_SLOTS, LANES), F32)
        stored_spec = pl.BlockSpec((tr * KV_SLOTS, LANES), lambda i: (i, 0))
        out_shapes[5:7] = [stored, stored]
        out_specs[5:7] = [stored_spec, stored_spec]
        out_shapes += [jax.ShapeDtypeStruct((m, KV_HEADS * LANES), BF16)] * 2
        out_specs += [row(KV_HEADS * LANES)] * 2
        out_shapes += [jax.ShapeDtypeStruct((m // tr, KV_HEADS, HEAD_DIM, tr), BF16)] * 2
        out_specs += [pl.BlockSpec((1, KV_HEADS, HEAD_DIM, tr), lambda i: (i, 0, 0, 0))] * 2
        out_shapes.append(stored)
        out_specs.append(stored_spec)
    return pl.pallas_call(
        _even_prep_kernel,
        grid=(m // tr,),
        in_specs=[row(E_MAIN), row(LANES), row(LANES), row(LANES),
                  full(1, A_WIDTH), full(1, LANES), full(3, LANES)],
        out_specs=out_specs,
        out_shape=out_shapes,
        compiler_params=_cparams(("parallel",)),
        name="even_prep",
    )(proj, gl, cosf, sinf, v_gain.reshape(1, A_WIDTH), q_gain.reshape(1, LANES), k_gain)


GMLP_CHUNKS = 4


def _gmlp_kernel(u_ref, v_ref, ws_ref, bst_ref, o_ref):
    tril = _iota((A_CHUNK, A_CHUNK), 0) >= _iota((A_CHUNK, A_CHUNK), 1)
    for g in range(A_GROUPS):
        sl = slice(g * LANES, (g + 1) * LANES)
        wm = jnp.where(tril, ws_ref[g], 0.0).astype(BF16)
        bias = bst_ref[:, g:g + 1]
        for c in range(u_ref.shape[0] // A_CHUNK):
            rs = slice(c * A_CHUNK, (c + 1) * A_CHUNK)
            s = _dot(wm, v_ref[rs, sl].astype(BF16)) + bias
            o_ref[rs, sl] = (u_ref[rs, sl].astype(F32) * s).astype(o_ref.dtype)


def gmlp(u, v, ws, bs):
    m = u.shape[0]
    rows = A_CHUNK * GMLP_CHUNKS if m % (A_CHUNK * GMLP_CHUNKS) == 0 else A_CHUNK
    row = pl.BlockSpec((rows, A_WIDTH), lambda i: (i, 0))
    return pl.pallas_call(
        _gmlp_kernel,
        grid=(m // rows,),
        in_specs=[row, row,
                  pl.BlockSpec((A_GROUPS, A_CHUNK, A_CHUNK), lambda i: (0, 0, 0)),
                  pl.BlockSpec((A_CHUNK, A_GROUPS), lambda i: (0, 0))],
        out_specs=row,
        out_shape=jax.ShapeDtypeStruct((m, A_WIDTH), BF16),
        compiler_params=_cparams(("parallel",)),
        name="gmlp",
    )(u, v, ws, bs.T)


def _compress_kernel(x_ref, pool_ref, o_ref):
    pool = pool_ref[...]
    e = jnp.exp(pool - jnp.max(pool, axis=0, keepdims=True))
    w = e / jnp.sum(e, axis=0, keepdims=True)
    x = x_ref[...]
    nb = x.shape[0] // CMP_BLOCK
    o_ref[...] = jnp.sum(x.reshape(nb, CMP_BLOCK, KV_COLS) * w[None], axis=1)


def compress(rows, pool):
    m = rows.shape[0]
    tr = min(m, 1024)
    pool_cols = jnp.concatenate([jnp.repeat(pool.T, LANES, axis=1)] * 2, axis=1)
    return pl.pallas_call(
        _compress_kernel,
        grid=(m // tr,),
        in_specs=[pl.BlockSpec((tr, KV_COLS), lambda i: (i, 0)),
                  pl.BlockSpec((CMP_BLOCK, KV_COLS), lambda i: (0, 0))],
        out_specs=pl.BlockSpec((tr // CMP_BLOCK, KV_COLS), lambda i: (i, 0)),
        out_shape=jax.ShapeDtypeStruct((m // CMP_BLOCK, KV_COLS), F32),
        compiler_params=_cparams(("parallel",)),
        name="compress",
    )(rows, pool_cols)


NSA_TQ = 512
SEL_KC = 512
EXP2_SCALE = ATTN_SCALE * LOG2E


def _kv_gates(gate_ref, h):
    g = gate_ref[...]
    return jnp.where(h == 0, g[:, 0:GATES_PER_KV], g[:, GATES_PER_KV:2 * GATES_PER_KV])


def _pad_rows(x, rows):
    return jnp.concatenate([x, jnp.zeros((rows - x.shape[0], x.shape[1]), x.dtype)], axis=0)


def _nsa_prompt_kernel(qn_ref, qr_ref, kc_ref, vc_ref, ks_ref, vts_ref, kw_ref, vtw_ref, gate_ref, o_ref,
                       acc_ref, osel_ref, *, seq):
    h = pl.program_id(1)
    t = pl.program_id(2)
    nb = seq // CMP_BLOCK
    tq = NSA_TQ
    qpos_t = t * tq + _iota((1, tq), 1)
    blk_t = _iota((nb, 1), 0)

    heads = lambda ref: jnp.concatenate([ref[:, g * LANES:(g + 1) * LANES] for g in range(NSA_GROUP)], axis=0)
    wide = lambda x: jnp.concatenate([x] * NSA_GROUP, axis=1)
    qn4 = heads(qn_ref)
    qr4 = heads(qr_ref)

    kc = kc_ref[0].astype(BF16)
    vc_t = _pad_rows(vc_ref[0], LANES).T.astype(BF16)
    valid_ct = ((blk_t + 1) * CMP_BLOCK - 1) <= t * tq + _iota((1, NSA_GROUP * tq), 1) % tq
    st = jnp.where(valid_ct, _dot_nt(kc, qn4) * ATTN_SCALE, NEG)
    et = jnp.where(valid_ct, jnp.exp(st - jnp.max(st, axis=0, keepdims=True)), 0.0)
    pt = et / jnp.maximum(jnp.sum(et, axis=0, keepdims=True), TINY)
    o_cmp = _dot(vc_t, _pad_rows(pt, LANES).astype(BF16))
    imp_t = pt[:, 0:tq]
    for g in range(1, NSA_GROUP):
        imp_t = imp_t + pt[:, g * tq:(g + 1) * tq]

    cur_t = qpos_t // CMP_BLOCK
    forced = (blk_t == 0) | (blk_t == cur_t)
    score = jnp.where(forced, FORCE, jnp.where(blk_t > cur_t, NEG, imp_t))
    rank = jnp.zeros((nb, tq), jnp.int32)
    for j in range(nb):
        row = score[j:j + 1, :]
        beats = (row > score) | ((row == score) & (blk_t > j))
        rank = rank + beats.astype(jnp.int32)
    sel = _pad_rows((rank < min(SEL_TOPN, nb)).astype(F32), LANES).astype(BF16)

    def attend(lo, hi, k_ref, vt_ref, bias_fn):
        acc_ref[...] = jnp.zeros(acc_ref.shape, F32)

        def body(c, carry):
            m, l = carry
            start = pl.multiple_of(c * SEL_KC, SEL_KC)
            s = _dot_nt(k_ref[pl.ds(start, SEL_KC), :], qr4) + wide(bias_fn(c))
            m_new = jnp.maximum(m, jnp.max(s, axis=0, keepdims=True))
            p = jnp.exp2(s - m_new)
            alpha = jnp.exp2(m - m_new)
            acc_ref[...] = alpha * acc_ref[...] + _dot(vt_ref[c, 0], p.astype(BF16))
            return m_new, alpha * l + jnp.sum(p, axis=0, keepdims=True)

        init = (jnp.full((1, NSA_GROUP * tq), NEG, F32), jnp.zeros((1, NSA_GROUP * tq), F32))
        return lax.fori_loop(lo, hi, body, init)[1]

    def kpos_col(c):
        return c * SEL_KC + _iota((SEL_KC, 1), 0)

    def sel_bias(c):
        expand = ((c * SEL_KC + _iota((SEL_KC, LANES), 0)) // CMP_BLOCK == _iota((SEL_KC, LANES), 1)).astype(BF16)
        chosen = _dot(expand, sel) > 0.5
        return jnp.where(chosen & (kpos_col(c) <= qpos_t), 0.0, NEG)

    def win_bias(c):
        kpos = kpos_col(c)
        return jnp.where((kpos <= qpos_t) & (kpos >= qpos_t - WINDOW), 0.0, NEG)

    hi = (t * tq + tq - 1) // SEL_KC + 1
    l_sel = attend(0, hi, ks_ref, vts_ref, sel_bias)
    osel_ref[...] = acc_ref[...] / l_sel
    l_win = attend(jnp.maximum(t * tq - WINDOW, 0) // SEL_KC, hi, kw_ref, vtw_ref, win_bias)
    o_win = acc_ref[...] / l_win

    g_all = gate_ref[...]
    gate_t = jnp.where(h == 0, g_all, pltpu.roll(g_all, LANES - GATES_PER_KV, axis=1)).T
    for g in range(NSA_GROUP):
        cs = slice(g * tq, (g + 1) * tq)
        out_t = (gate_t[3 * g:3 * g + 1] * o_cmp[:, cs] + gate_t[3 * g + 1:3 * g + 2] * osel_ref[:, cs]
                 + gate_t[3 * g + 2:3 * g + 3] * o_win[:, cs])
        o_ref[:, g * LANES:(g + 1) * LANES] = out_t.T.astype(o_ref.dtype)


def nsa_prompt(qn, qr, kcmp, ks16, vts16, kw16, vtw16, gate, bsz, seq):
    nt = seq // NSA_TQ
    nb = seq // CMP_BLOCK
    nc = seq // SEL_KC
    assert nb <= LANES and seq % SEL_KC == 0
    qspec = pl.BlockSpec((NSA_TQ, NSA_GROUP * LANES), lambda b, h, t: (b * nt + t, h))
    cmp_k = pl.BlockSpec((1, nb, LANES), lambda b, h, t: (b, 0, h))
    cmp_v = pl.BlockSpec((1, nb, LANES), lambda b, h, t: (b, 0, KV_HEADS + h))
    key = pl.BlockSpec((seq, LANES), lambda b, h, t: (b, h))
    val_t = pl.BlockSpec((nc, 1, HEAD_DIM, SEL_KC), lambda b, h, t: (b, h, 0, 0))
    kc3 = kcmp.reshape(bsz, nb, KV_COLS)
    return pl.pallas_call(
        functools.partial(_nsa_prompt_kernel, seq=seq),
        grid=(bsz, KV_HEADS, nt),
        in_specs=[qspec, qspec, cmp_k, cmp_v, key, val_t, key, val_t,
                  pl.BlockSpec((NSA_TQ, LANES), lambda b, h, t: (b * nt + t, 0))],
        out_specs=qspec,
        out_shape=jax.ShapeDtypeStruct((bsz * seq, B_WIDTH), BF16),
        scratch_shapes=[pltpu.VMEM((HEAD_DIM, NSA_GROUP * NSA_TQ), F32),
                        pltpu.VMEM((HEAD_DIM, NSA_GROUP * NSA_TQ), F32)],
        compiler_params=_cparams(("parallel", "parallel", "arbitrary")),
        name="nsa_prompt",
    )(qn, qr, kc3, kc3, ks16, vts16, kw16, vtw16, gate)


PAGES_PER_STEP = 32
PAGE_ROWS = PAGE_SIZE * KV_SLOTS
BLOCK_ROWS = CMP_BLOCK * KV_SLOTS
BLOCKS_PER_PAGE = PAGE_SIZE // CMP_BLOCK


def _pool_past_kernel(pt_ref, *refs, pps):
    page_refs, pool_ref, o_ref = refs[:pps], refs[pps], refs[pps + 1]
    tiles = BLOCK_ROWS // SUBLANES
    pool = pool_ref[...].reshape(tiles, SUBLANES, LANES)
    fold = lambda a, op: op(a, pltpu.roll(a, KV_SLOTS, axis=0))
    mx = fold(jnp.max(pool, axis=0), jnp.maximum)
    e = jnp.exp(pool - mx[None])
    w = e / fold(jnp.sum(e, axis=0), jnp.add)[None]
    first = _iota((SUBLANES, LANES), 0) < KV_SLOTS
    for k, ref in enumerate(page_refs):
        x = ref[0].reshape(BLOCKS_PER_PAGE, tiles, SUBLANES, LANES)
        sums = [fold(jnp.sum(x[b] * w, axis=0), jnp.add) for b in range(BLOCKS_PER_PAGE)]
        o_ref[0, k * SUBLANES:(k + 1) * SUBLANES, :] = jnp.where(first, sums[0], sums[1])


def pool_past(cache, layer_off, page_table, pool):
    bsz, n_pages = page_table.shape
    pps = min(PAGES_PER_STEP, n_pages)
    steps = n_pages // pps
    assert BLOCKS_PER_PAGE * KV_SLOTS == SUBLANES
    pool_rows = jnp.broadcast_to(jnp.tile(pool.T, (1, 2)).reshape(BLOCK_ROWS, 1), (BLOCK_ROWS, LANES))

    def page_spec(k):
        return pl.BlockSpec((1, PAGE_ROWS, LANES),
                            lambda b, s, pt: (layer_off + pt[b * n_pages + s * pps + k], 0, 0))

    grid_spec = pltpu.PrefetchScalarGridSpec(
        num_scalar_prefetch=1,
        grid=(bsz, steps),
        in_specs=[page_spec(k) for k in range(pps)]
        + [pl.BlockSpec((BLOCK_ROWS, LANES), lambda b, s, pt: (0, 0))],
        out_specs=pl.BlockSpec((1, pps * SUBLANES, LANES), lambda b, s, pt: (b, s, 0)),
    )
    return pl.pallas_call(
        functools.partial(_pool_past_kernel, pps=pps),
        grid_spec=grid_spec,
        out_shape=jax.ShapeDtypeStruct((bsz, n_pages * SUBLANES, LANES), F32),
        compiler_params=_cparams(("parallel", "arbitrary")),
        name="pool_past",
    )(page_table.reshape(-1), *([cache] * pps), pool_rows)


def _sample_select_kernel(q_ref, kc_ref, oc_ref, idx_ref, score_ref, *, dec):
    bsz, rows = kc_ref.shape[0], kc_ref.shape[1]
    n_q = NSA_HEADS * dec
    rows_per_kv = NSA_GROUP * dec
    assert rows_per_kv == 2 * SUBLANES and dec <= SUBLANES // 2
    lane = _iota((n_q, rows), 1)
    is_key = lane % KV_SLOTS == _iota((n_q, 1), 0) // rows_per_kv
    lane8 = _iota((SUBLANES, rows), 1)
    for b in range(bsz):
        x = kc_ref[b].astype(BF16)
        s = jnp.where(is_key, _dot_nt(q_ref[b], x) * ATTN_SCALE, NEG)
        e = jnp.where(is_key, jnp.exp(s - jnp.max(s, axis=-1, keepdims=True)), 0.0)
        p = e / jnp.maximum(jnp.sum(e, axis=-1, keepdims=True), TINY)
        oc_ref[b] = _dot(pltpu.roll(p, KV_HEADS, axis=1).astype(BF16), x)
        for h in range(KV_HEADS):
            t = p[h * rows_per_kv:h * rows_per_kv + SUBLANES] + p[h * rows_per_kv + SUBLANES:(h + 1) * rows_per_kv]
            imp = t + pltpu.roll(t, SUBLANES // 2, axis=0)
            score = jnp.where((lane8 % KV_SLOTS == h) & (lane8 >= KV_SLOTS), imp, -jnp.inf)
            r0 = (b * KV_HEADS + h) * dec
            score_ref[r0:r0 + dec, :] = score[0:dec]
    score = score_ref[...]
    lane_all = _iota(score.shape, 1)
    slot = _iota((score.shape[0], SEL_TOPN), 1)
    picked = jnp.zeros((score.shape[0], SEL_TOPN), jnp.int32)
    for r in range(SEL_TOPN - 2):
        best = jnp.max(score, axis=-1, keepdims=True)
        arg = jnp.min(jnp.where(score == best, lane_all, rows), axis=-1, keepdims=True)
        picked = jnp.where(slot == r + 1, arg // KV_SLOTS, picked)
        score = jnp.where(lane_all == arg, -jnp.inf, score)
    idx_ref[...] = picked


def sample_select(qn, kcmp_past):
    bsz, dec, _ = qn.shape
    rows = kcmp_past.shape[1]
    n_sel = bsz * KV_HEADS * dec
    q_rows = qn.reshape(bsz, dec, NSA_HEADS, HEAD_DIM).swapaxes(1, 2).reshape(bsz, NSA_HEADS * dec, HEAD_DIM)
    o_cmp, idx = pl.pallas_call(
        functools.partial(_sample_select_kernel, dec=dec),
        out_shape=[jax.ShapeDtypeStruct((bsz, NSA_HEADS * dec, HEAD_DIM), F32),
                   jax.ShapeDtypeStruct((n_sel, SEL_TOPN), jnp.int32)],
        scratch_shapes=[pltpu.VMEM((n_sel, rows), F32)],
        compiler_params=pltpu.CompilerParams(vmem_limit_bytes=VMEM_LIMIT_BYTES),
        name="sample_select",
    )(q_rows, kcmp_past)
    o_cmp = o_cmp.reshape(bsz, NSA_HEADS, dec, HEAD_DIM).swapaxes(1, 2).reshape(bsz, dec, B_WIDTH)
    return o_cmp, idx.reshape(bsz, KV_HEADS * dec, SEL_TOPN)


def _rows_to_tile(rows):
    r = _iota((SUBLANES, rows[0].shape[1]), 0)
    out = jnp.zeros((SUBLANES, rows[0].shape[1]), rows[0].dtype)
    for i, x in enumerate(rows):
        out = jnp.where(r == i, jnp.broadcast_to(x, out.shape), out)
    return out


def _sample_attn_kernel(idx_ref, pt_ref, *refs, dec):
    n = SEL_TOPN
    blk_refs = refs[:dec * n]
    (qr_ref, ksn_ref, vsn_ref, wbuf_ref, kwn_ref, vwn_ref, gate_ref, oc_ref, o_ref) = refs[dec * n:]
    h = pl.program_id(1)
    q_all = jnp.concatenate(
        [_rows_to_tile([qr_ref[0, qi, :, g * LANES:(g + 1) * LANES].astype(F32) for g in range(NSA_GROUP)])
         for qi in range(dec)], axis=0).astype(BF16)
    rowq = _iota((dec * SUBLANES, 1), 0) // SUBLANES
    newpos = _iota((1, dec), 1)

    def attend(x_old, mask_old, k_new, v_new):
        s_old = jnp.where(mask_old, _dot_nt(q_all, x_old) * ATTN_SCALE, NEG)
        mask_new = newpos <= rowq
        s_new = jnp.where(mask_new, _dot_nt(q_all, k_new) * ATTN_SCALE, NEG)
        m = jnp.maximum(jnp.max(s_old, axis=-1, keepdims=True), jnp.max(s_new, axis=-1, keepdims=True))
        p_old = jnp.where(mask_old, jnp.exp(s_old - m), 0.0)
        p_new = jnp.where(mask_new, jnp.exp(s_new - m), 0.0)
        l = jnp.sum(p_old, axis=-1, keepdims=True) + jnp.sum(p_new, axis=-1, keepdims=True)
        pv = _dot(pltpu.roll(p_old, KV_HEADS, axis=1).astype(BF16), x_old)
        return (pv + _dot(p_new.astype(BF16), v_new)) / l

    x_sel = jnp.concatenate([r[0].astype(BF16) for r in blk_refs], axis=0)
    per_q = n * BLOCK_ROWS
    lane = _iota((1, dec * per_q), 1)
    mask_sel = (lane // per_q == rowq) & (lane % KV_SLOTS == h) & (lane % per_q < (n - 1) * BLOCK_ROWS)
    o_sel = attend(x_sel, mask_sel, ksn_ref[0].astype(BF16), vsn_ref[0].astype(BF16))

    wrows = wbuf_ref.shape[1]
    wb = wrows // KV_SLOTS
    lane_w = _iota((1, wrows), 1)
    mask_win = (lane_w % KV_SLOTS == h) & (lane_w // KV_SLOTS >= rowq + (wb - WINDOW))
    o_win = attend(wbuf_ref[0].astype(BF16), mask_win, kwn_ref[0].astype(BF16), vwn_ref[0].astype(BF16))

    for qi in range(dec):
        gate = _kv_gates(gate_ref.at[0, qi], h)
        outs = []
        for g in range(NSA_GROUP):
            r = qi * SUBLANES + g
            outs.append(gate[:, 3 * g:3 * g + 1] * oc_ref[0, qi, :, g * LANES:(g + 1) * LANES]
                        + gate[:, 3 * g + 1:3 * g + 2] * o_sel[r:r + 1]
                        + gate[:, 3 * g + 2:3 * g + 3] * o_win[r:r + 1])
        o_ref[0, qi] = jnp.concatenate(outs, axis=1).astype(o_ref.dtype)


def sample_attn(idx, page_table, cache_sel, sel_off, cache_win, win_off, qr, rows_s, rows_w, gate, o_cmp):
    bsz, dec, _ = rows_s.shape
    n_pages = page_table.shape[1]
    grp = NSA_GROUP * LANES

    def bspec(q, k):
        def imap(b, h, idx_r, pt_r):
            blk = idx_r[((b * KV_HEADS + h) * dec + q) * SEL_TOPN + k]
            page = sel_off + pt_r[b * n_pages + blk // BLOCKS_PER_PAGE]
            return (page * BLOCKS_PER_PAGE + blk % BLOCKS_PER_PAGE, 0, 0)
        return pl.BlockSpec((1, BLOCK_ROWS, LANES), imap)

    qmap = lambda b, h, idx_r, pt_r: (b, 0, 0, h)
    newk = pl.BlockSpec((1, dec, LANES), lambda b, h, idx_r, pt_r: (b, 0, h))
    newv = pl.BlockSpec((1, dec, LANES), lambda b, h, idx_r, pt_r: (b, 0, KV_HEADS + h))
    wbuf = pl.BlockSpec((1, cache_win.shape[1], LANES), lambda b, h, idx_r, pt_r: (win_off + b, 0, 0))
    grid_spec = pltpu.PrefetchScalarGridSpec(
        num_scalar_prefetch=2,
        grid=(bsz, KV_HEADS),
        in_specs=[bspec(q, k) for q in range(dec) for k in range(SEL_TOPN)]
        + [pl.BlockSpec((1, dec, 1, grp), qmap), newk, newv, wbuf, newk, newv,
           pl.BlockSpec((1, dec, 1, LANES), lambda b, h, idx_r, pt_r: (b, 0, 0, 0)),
           pl.BlockSpec((1, dec, 1, grp), qmap)],
        out_specs=pl.BlockSpec((1, dec, 1, grp), qmap),
    )
    q4 = lambda a: a.reshape(bsz, dec, 1, a.shape[-1])
    return pl.pallas_call(
        functools.partial(_sample_attn_kernel, dec=dec),
        grid_spec=grid_spec,
        out_shape=jax.ShapeDtypeStruct((bsz, dec, 1, B_WIDTH), BF16),
        compiler_params=_cparams(("parallel", "arbitrary")),
        name="sample_attn",
    )(idx.reshape(-1), page_table.reshape(-1), *([cache_sel] * (dec * SEL_TOPN)),
      q4(qr), rows_s, rows_s, cache_win, rows_w, rows_w, q4(gate), q4(o_cmp))


def _shift_rows(x, prev, k):
    xr = pltpu.roll(x, k, axis=0)
    pr = pltpu.roll(prev, k, axis=0)
    top = jnp.where(_iota((SUBLANES, 1), 0) < k, pr, xr[:SUBLANES])
    return jnp.concatenate([top, xr[SUBLANES:]], axis=0)


def _taps_kernel(*refs, width, act):
    x_refs, w_ref, b_ref, o_ref = refs[:width], refs[width], refs[width + 1], refs[width + 2]
    y = b_ref[...] + x_refs[0][...] * w_ref[0:1, :]
    for k in range(1, width):
        y = y + x_refs[k][...] * w_ref[k:k + 1, :]
    if act == "silu":
        o_ref[...] = y * jax.nn.sigmoid(y)
    else:
        o_ref[...] = y


def conv_taps(views, w, b, *, act, tile):
    width = len(views)
    m, n = views[0].shape
    col = pl.BlockSpec((m, tile), lambda j: (0, j))
    return pl.pallas_call(
        functools.partial(_taps_kernel, width=width, act=act),
        grid=(n // tile,),
        in_specs=[col] * width + [pl.BlockSpec((width, tile), lambda j: (0, j)),
                                  pl.BlockSpec((1, tile), lambda j: (0, j))],
        out_specs=col,
        out_shape=jax.ShapeDtypeStruct((m, n), F32),
        compiler_params=_cparams(("parallel",)),
        name="conv_taps_" + act,
    )(*views, w, b.reshape(1, -1))


def _conv_glu_kernel(*refs, width):
    g_refs, u_refs = refs[:width], refs[width:2 * width]
    wg_ref, wu_ref, bg_ref, bu_ref, o_ref = refs[2 * width:]

    def conv(x_refs, w_ref, b_ref):
        y = b_ref[...] + x_refs[0][...] * w_ref[0:1, :]
        for k in range(1, width):
            y = y + x_refs[k][...] * w_ref[k:k + 1, :]
        return y

    g = conv(g_refs, wg_ref, bg_ref)
    o_ref[...] = (g * jax.nn.sigmoid(g) * conv(u_refs, wu_ref, bu_ref)).astype(o_ref.dtype)


def conv_glu(views, w, b, half, tile):
    width = len(views)
    m = views[0].shape[0]
    nt = half // tile
    gate = lambda r: pl.BlockSpec((r, tile), lambda j: (0, j))
    value = lambda r: pl.BlockSpec((r, tile), lambda j: (0, nt + j))
    b2 = b.reshape(1, -1)
    return pl.pallas_call(
        functools.partial(_conv_glu_kernel, width=width),
        grid=(nt,),
        in_specs=[gate(m)] * width + [value(m)] * width + [gate(width), value(width), gate(1), value(1)],
        out_specs=gate(m),
        out_shape=jax.ShapeDtypeStruct((m, half), BF16),
        compiler_params=_cparams(("parallel",)),
        name="conv_glu",
    )(*views, *views, w, w, b2, b2)


SSD_GPS = SSM_GROUPS


def _ssd_kernel(*refs, n_chunks, fuse_conv):
    c = pl.program_id(2)
    q = SSM_CHUNK
    e_heads = SSM_GROUP_HEADS
    gw = SSM_GROUP_WIDTH
    if fuse_conv:
        (xr_ref, br_ref, cr_ref, z_ref, cwx_ref, cwb_ref, cwc_ref, cbx_ref, cbb_ref, cbc_ref,
         dt_ref, dtt_ref, al_ref, alt_ref, dsk_ref, ng_ref, h0_ref,
         y_ref, ht_ref, tx_ref, tb_ref, tc_ref, st_ref, hx_ref, hb_ref, hc_ref) = refs

        def conv_act(raw_ref, hist_ref, cw_ref, cb_ref, tail_ref):
            x = raw_ref[...]
            prev = jnp.where(c == 0, 0.0, hist_ref[...])
            y = x * cw_ref[0, SSM_CONV - 1:SSM_CONV, :] + cb_ref[0]
            for d in range(1, SSM_CONV):
                y = y + _shift_rows(x, prev, d) * cw_ref[0, SSM_CONV - 1 - d:SSM_CONV - d, :]
            last = x[q - SUBLANES:, :]
            hist_ref[...] = last
            tail_ref[0] = last
            return y * jax.nn.sigmoid(y)

        xs_all = conv_act(xr_ref, hx_ref, cwx_ref, cbx_ref, tx_ref)
        bm_all = conv_act(br_ref, hb_ref, cwb_ref, cbb_ref, tb_ref)
        cm_all = conv_act(cr_ref, hc_ref, cwc_ref, cbc_ref, tc_ref)
    else:
        (xs_ref, bm_ref, cm_ref, z_ref, dt_ref, dtt_ref, al_ref, alt_ref, dsk_ref, ng_ref, h0_ref,
         y_ref, ht_ref, st_ref) = refs
        xs_all, bm_all, cm_all = xs_ref[...], bm_ref[...], cm_ref[...]

    @pl.when(c == 0)
    def _():
        st_ref[...] = h0_ref[0]

    ri = _iota((q, q), 0)
    ci = _iota((q, q), 1)
    tril = ri >= ci
    tril_b = tril.astype(BF16)
    triu_b = (ri <= ci).astype(BF16)
    expand = (_iota((e_heads, gw), 0) == _iota((e_heads, gw), 1) // SSM_HEAD_DIM).astype(BF16)
    lane = _iota((q, 2 * SSM_HEAD_DIM), 1)

    acs_all = _dot3_r(tril_b, dt_ref[...] * (-jnp.exp(al_ref[...])))
    acs_t_all = _dot3_l(dtt_ref[...] * (-jnp.exp(alt_ref[...])), triu_b)

    for k in range(SSD_GPS):
        gs = slice(k * gw, (k + 1) * gw)
        ns = slice(k * SSM_STATE, (k + 1) * SSM_STATE)
        hs = slice(k * e_heads, (k + 1) * e_heads)
        dt = dt_ref[:, hs]
        acs = acs_all[:, hs]
        acs_t = acs_t_all[hs, :]
        tot = acs[q - 1:q, :]

        xs = xs_all[:, gs]
        xdt = xs * _dot2_l(dt, expand)
        xdt_b = xdt.astype(BF16)
        bm = bm_all[:, ns]
        cm_b = cm_all[:, ns].astype(BF16)
        cb = _dot_nt(cm_b, bm.astype(BF16))

        y_parts = []
        for pair in range(e_heads // 2):
            cols = slice(pair * 2 * SSM_HEAD_DIM, (pair + 1) * 2 * SSM_HEAD_DIM)
            res = []
            for e in (2 * pair, 2 * pair + 1):
                seg = jnp.where(tril, acs[:, e:e + 1] - acs_t[e:e + 1, :], NEG)
                mix = (cb * jnp.exp(seg)).astype(BF16)
                res.append(_dot(mix, xdt_b[:, cols]))
            y_parts.append(jnp.where(lane < SSM_HEAD_DIM, res[0], res[1]))
        y = jnp.concatenate(y_parts, axis=1)

        state = st_ref[:, gs]
        ex = _dot2_l(jnp.concatenate([jnp.exp(acs), jnp.exp(tot - acs)], axis=0), expand)
        y = y + _dot(cm_b, state.astype(BF16)) * ex[0:q]
        contrib = _dot(bm.T.astype(BF16), (xdt * ex[q:2 * q]).astype(BF16))
        st_ref[:, gs] = state * ex[q - 1:q] + contrib

        y = y + xs * dsk_ref[:, gs]
        z = z_ref[:, gs]
        y = y * (z * jax.nn.sigmoid(z))
        ms = jnp.mean(y * y, axis=-1, keepdims=True)
        y_ref[:, gs] = (y * lax.rsqrt(ms + EPS) * ng_ref[:, gs]).astype(y_ref.dtype)

    @pl.when(c == n_chunks - 1)
    def _():
        ht_ref[0] = st_ref[...]


def ssd(xbc, xbc_col0, zsrc, dt, a_log, d_skip, norm_g, h0, bsz, seq, conv=None):
    nc = seq // SSM_CHUNK
    gps = SSD_GPS
    gw = SSM_GROUP_WIDTH * gps
    nw = SSM_STATE * gps
    e = SSM_GROUP_HEADS
    m = bsz * seq
    assert gps == SSM_GROUPS
    dt_t = dt.T
    al = a_log.astype(F32).reshape(1, SSM_HEADS)
    al_t = a_log.astype(F32).reshape(SSM_HEADS, 1)
    dsk = jnp.repeat(d_skip.astype(F32), SSM_HEAD_DIM).reshape(1, SSM_INNER)
    bc = SSM_GROUPS * SSM_STATE
    x_off = xbc_col0 // gw
    b_off = (xbc_col0 + SSM_INNER) // nw
    c_off = (xbc_col0 + SSM_INNER + bc) // nw
    rowmap = lambda col: (lambda b, g, c: (b * nc + c, col(g)))
    xbc_specs = [pl.BlockSpec((SSM_CHUNK, gw), rowmap(lambda g: x_off + g)),
                 pl.BlockSpec((SSM_CHUNK, nw), rowmap(lambda g: b_off + g)),
                 pl.BlockSpec((SSM_CHUNK, nw), rowmap(lambda g: c_off + g)),
                 pl.BlockSpec((SSM_CHUNK, gw), rowmap(lambda g: g))]
    rest_specs = [pl.BlockSpec((SSM_CHUNK, SSM_HEADS), lambda b, g, c: (b * nc + c, 0)),
                  pl.BlockSpec((SSM_HEADS, SSM_CHUNK), lambda b, g, c: (0, b * nc + c)),
                  pl.BlockSpec((1, SSM_HEADS), lambda b, g, c: (0, 0)),
                  pl.BlockSpec((SSM_HEADS, 1), lambda b, g, c: (0, 0)),
                  pl.BlockSpec((1, gw), lambda b, g, c: (0, g)),
                  pl.BlockSpec((1, gw), lambda b, g, c: (0, g)),
                  pl.BlockSpec((1, SSM_STATE, gw), lambda b, g, c: (b, 0, g))]
    rest_args = [dt, dt_t, al, al_t, dsk, norm_g.astype(F32).reshape(1, SSM_INNER), h0]
    out_specs = [pl.BlockSpec((SSM_CHUNK, gw), rowmap(lambda g: g)),
                 pl.BlockSpec((1, SSM_STATE, gw), lambda b, g, c: (b, 0, g))]
    out_shape = [jax.ShapeDtypeStruct((m, SSM_INNER), BF16),
                 jax.ShapeDtypeStruct((bsz, SSM_STATE, SSM_INNER), F32)]
    scratch = [pltpu.VMEM((SSM_STATE, gw), F32)]
    conv_specs, conv_args = [], []
    if conv is not None:
        cw, cb, layer = conv
        cb3 = cb.reshape(cb.shape[0], 1, -1)
        offs = ((gw, 0), (nw, SSM_INNER // nw), (nw, (SSM_INNER + bc) // nw))
        conv_specs = [pl.BlockSpec((1, SSM_CONV, w), lambda b, g, c, o=o: (layer, 0, o + g)) for w, o in offs]
        conv_specs += [pl.BlockSpec((1, 1, w), lambda b, g, c, o=o: (layer, 0, o + g)) for w, o in offs]
        conv_args = [cw, cw, cw, cb3, cb3, cb3]
        for w, total in ((gw, SSM_INNER), (nw, bc), (nw, bc)):
            out_specs.append(pl.BlockSpec((1, SUBLANES, w), lambda b, g, c: (b, 0, g)))
            out_shape.append(jax.ShapeDtypeStruct((bsz, SUBLANES, total), F32))
            scratch.append(pltpu.VMEM((SUBLANES, w), F32))
    return pl.pallas_call(
        functools.partial(_ssd_kernel, n_chunks=nc, fuse_conv=conv is not None),
        grid=(bsz, SSM_GROUPS // gps, nc),
        in_specs=xbc_specs + conv_specs + rest_specs,
        out_specs=out_specs,
        out_shape=out_shape,
        scratch_shapes=scratch,
        compiler_params=_cparams(("parallel", "parallel", "arbitrary")),
        name="ssd",
    )(xbc, xbc, xbc, zsrc, *conv_args, *rest_args)


def _state_to_cols(h):
    b = h.shape[0]
    return h.transpose(0, 3, 1, 2).reshape(b, SSM_STATE, SSM_INNER)


def _state_from_cols(s):
    b = s.shape[0]
    return s.reshape(b, SSM_STATE, SSM_HEADS, SSM_HEAD_DIM).transpose(0, 2, 3, 1)


FFN_BM = 1024
FFN_SUB = 256
FFN_TILE = 512
FFN_TILES = D_FF // FFN_TILE


def _conv3(h, prev, cw_ref, cb_ref):
    return (_shift_rows(h, prev, 2) * cw_ref[0, 0:1, :] + _shift_rows(h, prev, 1) * cw_ref[0, 1:2, :]
            + h * cw_ref[0, 2:3, :] + cb_ref[0])


def _ffn_up_kernel(x_ref, xs_ref, wg_ref, wu_ref, cwg_ref, cwu_ref, cbg_ref, cbu_ref,
                   act_ref, tg_ref, tu_ref, sg_ref, su_ref, wgb_ref, wub_ref, cg_ref, cu_ref, *, tiles_per_seq):
    i = pl.program_id(1)

    @pl.when(i == 0)
    def _():
        wgb_ref[...] = wg_ref[0].astype(BF16)
        wub_ref[...] = wu_ref[0].astype(BF16)
        sg_ref[...] = _dot(xs_ref[...], wgb_ref[...])
        su_ref[...] = _dot(xs_ref[...], wub_ref[...])

    @pl.when(i % tiles_per_seq == 0)
    def _():
        cg_ref[...] = jnp.zeros_like(cg_ref)
        cu_ref[...] = jnp.zeros_like(cu_ref)

    for r in range(FFN_BM // FFN_SUB):
        rs = slice(r * FFN_SUB, (r + 1) * FFN_SUB)
        x = x_ref[rs, :]
        hg = _dot(x, wgb_ref[...])
        hu = _dot(x, wub_ref[...])
        g = _conv3(hg, cg_ref[...], cwg_ref, cbg_ref)
        u = _conv3(hu, cu_ref[...], cwu_ref, cbu_ref)
        act_ref[rs, :] = (g * jax.nn.sigmoid(g) * u).astype(act_ref.dtype)
        cg_ref[...] = hg[FFN_SUB - SUBLANES:, :]
        cu_ref[...] = hu[FFN_SUB - SUBLANES:, :]
    tg_ref[0] = cg_ref[...]
    tu_ref[0] = cu_ref[...]


def ffn_up(xp, xs, w_up, conv_w, conv_b, layer, bsz, seq):
    m, k = xp.shape
    ms = xs.shape[0]
    tiles_per_seq = seq // FFN_BM
    nt = FFN_TILES
    wspec = lambda off: pl.BlockSpec((1, k, FFN_TILE), lambda j, i: (layer, 0, off + j))
    cwspec = lambda off: pl.BlockSpec((1, FFN_CONV, FFN_TILE), lambda j, i: (layer, 0, off + j))
    cbspec = lambda off: pl.BlockSpec((1, 1, FFN_TILE), lambda j, i: (layer, 0, off + j))
    tail = pl.BlockSpec((1, SUBLANES, FFN_TILE), lambda j, i: (i // tiles_per_seq, 0, j))
    samp = pl.BlockSpec((ms, FFN_TILE), lambda j, i: (0, j))
    cb3 = conv_b.reshape(conv_b.shape[0], 1, -1)
    carry = pltpu.VMEM((SUBLANES, FFN_TILE), F32)
    return pl.pallas_call(
        functools.partial(_ffn_up_kernel, tiles_per_seq=tiles_per_seq),
        grid=(nt, m // FFN_BM),
        in_specs=[pl.BlockSpec((FFN_BM, k), lambda j, i: (i, 0)), pl.BlockSpec((ms, k), lambda j, i: (0, 0)),
                  wspec(0), wspec(nt), cwspec(0), cwspec(nt), cbspec(0), cbspec(nt)],
        out_specs=[pl.BlockSpec((FFN_BM, FFN_TILE), lambda j, i: (i, j)), tail, tail, samp, samp],
        out_shape=[jax.ShapeDtypeStruct((m, D_FF), BF16),
                   jax.ShapeDtypeStruct((bsz, SUBLANES, D_FF), F32), jax.ShapeDtypeStruct((bsz, SUBLANES, D_FF), F32),
                   jax.ShapeDtypeStruct((ms, D_FF), F32), jax.ShapeDtypeStruct((ms, D_FF), F32)],
        scratch_shapes=[pltpu.VMEM((k, FFN_TILE), BF16), pltpu.VMEM((k, FFN_TILE), BF16), carry, carry],
        compiler_params=_cparams(("parallel", "arbitrary")),
        name="ffn_up",
    )(xp, xs, w_up, w_up, conv_w, conv_w, cb3, cb3)


def _rope_tables(pos):
    half = HEAD_DIM // 2
    inv = ROPE_THETA ** (-jnp.arange(half, dtype=F32) / half)
    ang = pos.astype(F32)[:, None] * inv[None, :]
    cos, sin = jnp.cos(ang), jnp.sin(ang)
    return jnp.concatenate([cos, cos], axis=1), jnp.concatenate([-sin, sin], axis=1)


def _kv_out(rows, bsz):
    return rows.reshape(bsz, -1, 2, KV_HEADS, HEAD_DIM)


def kernel(x_prompt, x_sample, cache_kv_cmp, cache_kv_sel, cache_kv_win, state_ssm_conv, state_ssm, state_ffn_conv, page_table, norm_mix, norm_ffn, w_in_even, w_out_even, gmlp_v_norm, gmlp_ws, gmlp_bs, q_norm, k_norm, cmp_pool, w_in_odd, ssm_conv_w, ssm_conv_b, ssm_dt_bias, ssm_a_log, ssm_d, ssm_norm, w_out_odd, ffn_w_up, ffn_conv_w, ffn_conv_b, ffn_w_down):
    bp, sp, _ = x_prompt.shape
    bs, ss, _ = x_sample.shape
    depth = norm_mix.shape[0]
    n_pool = cache_kv_cmp.shape[1]
    n_pages = page_table.shape[1]
    past_len = n_pages * PAGE_SIZE
    wb = cache_kv_win.shape[2]
    mp, ms = bp * sp, bs * ss

    cos_p, sin_p = _rope_tables(jnp.arange(sp, dtype=jnp.int32))
    cos_p, sin_p = jnp.tile(cos_p, (bp, 1)), jnp.tile(sin_p, (bp, 1))
    cos_s, sin_s = _rope_tables(past_len + jnp.arange(ss, dtype=jnp.int32))
    cos_s, sin_s = jnp.tile(cos_s, (bs, 1)), jnp.tile(sin_s, (bs, 1))

    cache_c = cache_kv_cmp.reshape(-1, PAGE_ROWS, LANES)
    cache_s = cache_kv_sel.reshape(-1, BLOCK_ROWS, LANES)
    cache_w = cache_kv_win.reshape(-1, wb * KV_SLOTS, LANES)

    w_down16 = ffn_w_down.astype(BF16)
    w_out_odd16 = w_out_odd.astype(BF16)
    w_in_even_t = jnp.swapaxes(w_in_even, 1, 2)
    w_in_odd_t = jnp.swapaxes(w_in_odd, 1, 2)

    hp = x_prompt.reshape(mp, D_MODEL)
    hs = x_sample.reshape(ms, D_MODEL)
    outs = {k: [] for k in ("p_cmp", "p_sel", "p_win", "s_cmp", "s_sel", "s_win", "s_v",
                            "p_sconv", "p_sst", "s_sconv", "s_sst", "p_fconv", "s_fconv")}

    for layer in range(depth):
        i = layer // 2
        xp = rms_cast(hp, norm_mix[layer])
        xs = rms_cast(hs, norm_mix[layer])
        if layer % 2 == 0:
            w_gate = w_in_even_t[i, E_MAIN:]
            proj_p, proj_s = matmul_ws([xp], [xs], w_in_even_t, i, E_MAIN, bn=768, w_t=True, name="even_in")
            gl = matmul_narrow(xp, w_gate)
            u, v, qn, qr, rc, rs, rw, gate, ks16, kw16, vts16, vtw16, rc_stored = even_prep(
                proj_p, gl, cos_p, sin_p, gmlp_v_norm[i], q_norm[i], k_norm[i], for_mxu=True)
            a_p = gmlp(u, v, gmlp_ws[i], gmlp_bs[i])
            kcmp = compress(rc, cmp_pool[i])
            b_p = nsa_prompt(qn, qr, kcmp, ks16, vts16, kw16, vtw16, gate, bp, sp)
            outs["p_cmp"].append(_kv_out(rc_stored, bp))
            outs["p_sel"].append(_kv_out(rs, bp))
            outs["p_win"].append(_kv_out(rw, bp)[:, sp - min(WINDOW, sp):])
            gl = matmul_narrow(xs, w_gate)
            u, v, qn, qr, rc, rs, rw, gate = even_prep(
                proj_s, gl, cos_s, sin_s, gmlp_v_norm[i], q_norm[i], k_norm[i], for_mxu=False)
            lpad = ((0, 0), (0, A_CHUNK - ss), (0, 0))
            a_s = gmlp(jnp.pad(u.reshape(bs, ss, -1), lpad).reshape(bs * A_CHUNK, -1),
                       jnp.pad(v.reshape(bs, ss, -1), lpad).reshape(bs * A_CHUNK, -1),
                       gmlp_ws[i], gmlp_bs[i]).reshape(bs, A_CHUNK, -1)[:, :ss].reshape(ms, -1)
            kc_past = pool_past(cache_c, i * n_pool, page_table, cmp_pool[i])
            o_cmp, idx = sample_select(qn.reshape(bs, ss, -1), kc_past)
            b_s = sample_attn(idx, page_table, cache_s, i * n_pool, cache_w, i * bs,
                              qr.reshape(bs, ss, -1), rs.reshape(bs, ss, -1), rw.reshape(bs, ss, -1),
                              gate.reshape(bs, ss, -1), o_cmp).reshape(ms, B_WIDTH)
            outs["s_cmp"].append(_kv_out(rc, bs))
            outs["s_sel"].append(_kv_out(rs, bs))
            outs["s_win"].append(_kv_out(rw, bs))
            outs["s_v"].append(v.reshape(bs, ss, A_WIDTH))
            hp, hs = matmul_ws([a_p, b_p], [a_s, b_s], w_out_even, i, D_MODEL, bn=1024, res=(hp, hs), name="even_out")
        else:
            zx = SSM_INNER + SSM_CONV_DIM
            w_dt = w_in_odd_t[i, zx:]
            proj_p, proj_s = matmul_ws([xp], [xs], w_in_odd_t, i, zx, bn=1024, w_t=True, name="odd_in")
            dt = matmul_narrow(xp, w_dt, ssm_dt_bias[i])[:, :SSM_HEADS]
            h0 = jnp.zeros((bp, SSM_STATE, SSM_INNER), F32)
            y, ht, *tails = ssd(proj_p, SSM_INNER, proj_p, dt, ssm_a_log[i], ssm_d[i], ssm_norm[i], h0, bp, sp,
                                conv=(ssm_conv_w, ssm_conv_b, i))
            tail = jnp.concatenate(tails, axis=2)
            hp = matmul(y, w_out_odd16, layer=i, bn=512, res=hp, name="odd_out")
            outs["p_sconv"].append(tail[:, SUBLANES - (SSM_CONV - 1):])
            outs["p_sst"].append(_state_from_cols(ht))
            dt = matmul_narrow(xs, w_dt, ssm_dt_bias[i])[:, :SSM_HEADS]
            xin = jnp.concatenate([state_ssm_conv[i], proj_s[:, SSM_INNER:].reshape(bs, ss, -1)], axis=1)
            views = [xin[:, k:k + ss].reshape(ms, -1) for k in range(SSM_CONV)]
            xbc = conv_taps(views, ssm_conv_w[i], ssm_conv_b[i], act="silu", tile=1024)
            cpad = ((0, 0), (0, SSM_CHUNK - ss), (0, 0))
            padrows = lambda a: jnp.pad(a.reshape(bs, ss, -1), cpad).reshape(bs * SSM_CHUNK, -1)
            y, ht = ssd(padrows(xbc), 0, padrows(proj_s[:, :SSM_INNER]), padrows(dt), ssm_a_log[i], ssm_d[i], ssm_norm[i],
                        _state_to_cols(state_ssm[i].astype(F32)), bs, SSM_CHUNK)
            y = y.reshape(bs, SSM_CHUNK, -1)[:, :ss].reshape(ms, -1)
            hs = matmul(y, w_out_odd16, layer=i, bn=1024, res=hs, name="odd_out_s")
            outs["s_sconv"].append(xin[:, ss:])
            outs["s_sst"].append(_state_from_cols(ht))
        xp = rms_cast(hp, norm_ffn[layer])
        xs = rms_cast(hs, norm_ffn[layer])
        act, tail_g, tail_u, hu_g, hu_u = ffn_up(xp, xs, ffn_w_up, ffn_conv_w, ffn_conv_b, layer, bp, sp)
        hp = matmul(act, w_down16, layer=layer, bn=512, res=hp, name="ffn_down")
        outs["p_fconv"].append(jnp.concatenate([tail_g, tail_u], axis=2)[:, SUBLANES - (FFN_CONV - 1):])
        hu = jnp.concatenate([hu_g, hu_u], axis=1)
        xin = jnp.concatenate([state_ffn_conv[layer], hu.reshape(bs, ss, -1)], axis=1)
        views = [xin[:, k:k + ss].reshape(ms, -1) for k in range(FFN_CONV)]
        act_s = conv_glu(views, ffn_conv_w[layer], ffn_conv_b[layer], D_FF, 512)
        hs = matmul(act_s, w_down16, layer=layer, bn=512, res=hs, name="ffn_down_s")
        outs["s_fconv"].append(xin[:, ss:])

    st = lambda k: jnp.stack(outs[k])
    return (hp.reshape(bp, sp, D_MODEL), hs.reshape(bs, ss, D_MODEL), st("p_cmp"), st("p_sel"), st("p_win"),
            st("p_sconv"), st("p_sst"), st("p_fconv"), st("s_cmp"), st("s_sel"), st("s_win"), st("s_v"),
            st("s_sconv"), st("s_sst"), st("s_fconv"))
```
